```python
import numpy as np
import jax
import jax.numpy as jnp
from jax import lax

D_MODEL = 2048
BATCH = 4
SEQ = 8192
DEPTH = 1

MEM_LEN = 256
EPS = 1e-6
NEG = -1e30
BIG = 1e30
NSA_HEADS = 8
NSA_KV_HEADS = 2
NSA_GROUP = NSA_HEADS // NSA_KV_HEADS
NSA_DK = 128
NSA_DV = 128
CMP_LEN = 32
CMP_STRIDE = 16
CMP_HIDDEN = 256
SEL_BLOCK = 64
SEL_TOP = 16
WINDOW = 512
NSA_QBLOCK = 64
GLA_HEADS = 4
GLA_DK = 128
GLA_DV = 256
GLA_GATE_RANK = 16
GLA_GATE_NORM = 16.0
GLA_CHUNK = 64
D_MIX = NSA_HEADS * NSA_DV + GLA_HEADS * GLA_DV
MEM_HEADS = 4
MEM_DH = 128
D_FF = -(-8 * D_MODEL // (3 * 256)) * 256
IN_SIZES = (
    NSA_HEADS * NSA_DK,
    NSA_KV_HEADS * NSA_DK, NSA_KV_HEADS * NSA_DV,
    NSA_KV_HEADS * NSA_DK, NSA_KV_HEADS * NSA_DV,
    NSA_KV_HEADS * NSA_DK, NSA_KV_HEADS * NSA_DV,
    3 * NSA_HEADS,
    GLA_HEADS * GLA_DK, GLA_HEADS * GLA_DK,
    GLA_HEADS * GLA_DV,
    GLA_GATE_RANK,
    GLA_HEADS * GLA_DV,
)
D_IN = sum(IN_SIZES)

kernel_name = 'hymba_nsa_gla_alibi_memory_block'


def rms_norm(u, g):
    uf = u.astype(jnp.float32)
    y = uf * lax.rsqrt(jnp.mean(uf * uf, axis=-1, keepdims=True) + EPS)
    return (y * g.astype(jnp.float32)).astype(u.dtype)


def alibi_slopes(n):
    return jnp.exp2(-8.0 * jnp.arange(1, n + 1, dtype=jnp.float32) / n)


def split_points():
    return np.cumsum(np.array(IN_SIZES))[:-1].tolist()


def nsa_mixer(q, k_cmp, v_cmp, k_slc, v_slc, k_win, v_win, gates,
              g_q, g_kc, g_ks, g_kw, pe_k, pe_v, w_ck1, w_ck2, w_cv1, w_cv2):
    f32 = jnp.float32
    B, T = q.shape[0], q.shape[1]
    HKV, G, QB = NSA_KV_HEADS, NSA_GROUP, NSA_QBLOCK
    n_cmp = (T - CMP_LEN) // CMP_STRIDE + 1
    n_sel = T // SEL_BLOCK
    n_top = min(SEL_TOP, n_sel)
    n_qb = T // QB
    ratio = SEL_BLOCK // CMP_STRIDE
    lead = CMP_LEN // CMP_STRIDE - 1

    tok = jnp.arange(n_cmp)[:, None] * CMP_STRIDE + jnp.arange(CMP_LEN)[None, :]

    def compress(u, pe, w1, w2):
        blk = u[:, tok] + pe[None, None, :, None, :]
        blk = blk.transpose(0, 1, 3, 2, 4).reshape(B, n_cmp, HKV, -1)
        return (jax.nn.silu(blk @ w1) @ w2).transpose(0, 2, 1, 3)

    kc = rms_norm(compress(k_cmp, pe_k, w_ck1, w_ck2), g_kc)
    vc = compress(v_cmp, pe_v, w_cv1, w_cv2)
    cmp_end = jnp.arange(n_cmp) * CMP_STRIDE + (CMP_LEN - 1)
    cmp_mid = jnp.arange(n_cmp).astype(f32) * CMP_STRIDE + 0.5 * (CMP_LEN - 1)

    r = np.arange(ratio + lead)
    st = CMP_STRIDE * (r - lead)
    ov = jnp.asarray((np.minimum(st + CMP_LEN, SEL_BLOCK) - np.maximum(st, 0)) / CMP_STRIDE, f32)
    sel_idx = jnp.arange(n_sel)[:, None] * ratio + jnp.asarray(r)[None, :]

    ksb = rms_norm(k_slc, g_ks).reshape(B, n_sel, SEL_BLOCK, HKV, NSA_DK).transpose(0, 3, 1, 2, 4)
    vsb = v_slc.reshape(B, n_sel, SEL_BLOCK, HKV, NSA_DV).transpose(0, 3, 1, 2, 4)
    pad_t = ((0, 0), (0, 0), (WINDOW, 0), (0, 0))
    kwp = jnp.pad(rms_norm(k_win, g_kw).transpose(0, 2, 1, 3), pad_t)
    vwp = jnp.pad(v_win.transpose(0, 2, 1, 3), pad_t)

    slopes = alibi_slopes(NSA_HEADS).reshape(HKV, G)
    sl5 = slopes[None, :, :, None, None]
    sl6 = slopes[None, :, :, None, None, None]
    jb = jnp.arange(n_sel)
    bi = jnp.arange(B)[:, None, None, None]
    hi = jnp.arange(HKV)[None, :, None, None]

    qn = rms_norm(q, g_q) * (NSA_DK ** -0.5)
    qs = qn.reshape(B, n_qb, QB, HKV, G, NSA_DK).transpose(1, 0, 3, 4, 2, 5)
    gs = gates.reshape(B, n_qb, QB, HKV, G, 3).transpose(1, 0, 3, 4, 2, 5)

    def block(args):
        c, qb, gb = args
        t = c * QB + jnp.arange(QB)
        s = jnp.einsum('bkgtd,bknd->bkgtn', qb, kc).astype(f32)
        vis = t[:, None] >= cmp_end[None, :]
        s = jnp.where(vis, s - sl5 * (t[:, None].astype(f32) - cmp_mid[None, :]), NEG)
        p = jax.nn.softmax(s, axis=-1)
        p = jnp.where(jnp.any(vis, axis=-1)[:, None], p, 0.0)
        o_cmp = jnp.einsum('bkgtn,bknd->bkgtd', p.astype(vc.dtype), vc)
        imp = jnp.pad(p.sum(2), ((0, 0), (0, 0), (0, 0), (lead, lead)))
        imp = jnp.einsum('bktjr,r->bktj', imp[..., sel_idx], ov)
        cur = (t // SEL_BLOCK)[:, None]
        forced = (jb == 0) | (jb == cur) | (jb == cur - 1)
        score = jnp.where(forced, BIG, jnp.where(jb <= cur, imp, NEG))
        top_s, top_i = lax.top_k(score, n_top)
        ks_g = ksb[bi, hi, top_i]
        vs_g = vsb[bi, hi, top_i]
        pos = top_i[..., None] * SEL_BLOCK + jnp.arange(SEL_BLOCK)
        tt = t[:, None, None]
        ok = (top_s > 0.5 * NEG)[..., None] & (pos <= tt)
        s = jnp.einsum('bkgtd,bktnsd->bkgtns', qb, ks_g).astype(f32)
        s = jnp.where(ok[:, :, None], s - sl6 * (tt - pos).astype(f32)[:, :, None], NEG)
        p = jax.nn.softmax(s.reshape(s.shape[:4] + (-1,)), axis=-1).reshape(s.shape)
        o_slc = jnp.einsum('bkgtns,bktnsd->bkgtd', p.astype(vs_g.dtype), vs_g)
        kw = lax.dynamic_slice_in_dim(kwp, c * QB, QB + WINDOW, axis=2)
        vw = lax.dynamic_slice_in_dim(vwp, c * QB, QB + WINDOW, axis=2)
        pos_w = c * QB - WINDOW + jnp.arange(QB + WINDOW)
        dw = t[:, None] - pos_w[None, :]
        okw = (dw >= 0) & (dw < WINDOW) & (pos_w >= 0)[None, :]
        s = jnp.einsum('bkgtd,bksd->bkgts', qb, kw).astype(f32)
        s = jnp.where(okw, s - sl5 * dw.astype(f32), NEG)
        p = jax.nn.softmax(s, axis=-1)
        o_win = jnp.einsum('bkgts,bksd->bkgtd', p.astype(vw.dtype), vw)
        return gb[..., 0:1] * o_cmp + gb[..., 1:2] * o_slc + gb[..., 2:3] * o_win

    out = lax.map(block, (jnp.arange(n_qb), qs, gs))
    return out.transpose(1, 0, 4, 2, 3, 5).reshape(B, T, NSA_HEADS, NSA_DV)


def gla_mixer(q, k, v, log_a):
    B, T, H, DK = q.shape
    DV = v.shape[-1]
    C = GLA_CHUNK
    n = T // C

    def chunks(u):
        return u.astype(jnp.float32).reshape(B, n, C, H, -1).transpose(1, 0, 3, 2, 4)

    causal = jnp.tril(jnp.ones((C, C), bool))[:, :, None]

    def step(S, inp):
        qc, kc, vc, ac = inp
        b = jnp.cumsum(ac, axis=2)
        decay = jnp.exp(jnp.where(causal, b[:, :, :, None, :] - b[:, :, None, :, :], -jnp.inf))
        attn = jnp.einsum('bhid,bhjd,bhijd->bhij', qc, kc, decay)
        o = (jnp.einsum('bhij,bhjv->bhiv', attn, vc)
             + jnp.einsum('bhid,bhdv->bhiv', qc * jnp.exp(b), S))
        b_end = b[:, :, -1:, :]
        S = (S * jnp.exp(b_end[:, :, 0, :, None])
             + jnp.einsum('bhjd,bhjv->bhdv', kc * jnp.exp(b_end - b), vc))
        return S, o

    S0 = jnp.zeros((B, H, DK, DV), jnp.float32)
    _, o = lax.scan(step, S0, (chunks(q), chunks(k), chunks(v), chunks(log_a)))
    return o.transpose(1, 0, 3, 2, 4).reshape(B, T, H, DV)


def hybrid_layer(x, mem, g_mix, w_in, b_nsa_gate, g_q, g_kc, g_ks, g_kw, pe_k, pe_v,
                 w_ck1, w_ck2, w_cv1, w_cv2, g_nsa_out, w_gk2, b_gk, g_gla_out, w_out,
                 g_cross, g_mem, w_cq, w_ck, w_cv, g_cq, g_ck, w_co, g_ffn, w_gu, w_down):
    B, T, _ = x.shape
    M = mem.shape[1]
    h = rms_norm(x, g_mix)
    (q, k_c, v_c, k_s, v_s, k_w, v_w, g_logit,
     q_l, k_l, v_l, a_l, r_l) = jnp.split(h @ w_in, split_points(), axis=-1)

    def heads(u, nh):
        return u.reshape(B, T, nh, -1)

    hkv = NSA_KV_HEADS
    gates = jax.nn.sigmoid(g_logit + b_nsa_gate).reshape(B, T, NSA_HEADS, 3)
    o_nsa = nsa_mixer(heads(q, NSA_HEADS), heads(k_c, hkv), heads(v_c, hkv), heads(k_s, hkv),
                      heads(v_s, hkv), heads(k_w, hkv), heads(v_w, hkv), gates,
                      g_q, g_kc, g_ks, g_kw, pe_k, pe_v, w_ck1, w_ck2, w_cv1, w_cv2)
    o_nsa = rms_norm(o_nsa, g_nsa_out).reshape(B, T, -1)

    log_a = jax.nn.log_sigmoid((a_l @ w_gk2 + b_gk).astype(jnp.float32)) / GLA_GATE_NORM
    o_gla = gla_mixer(heads(q_l, GLA_HEADS) * (GLA_DK ** -0.5), heads(k_l, GLA_HEADS),
                      heads(v_l, GLA_HEADS), heads(log_a, GLA_HEADS))
    o_gla = rms_norm(o_gla, g_gla_out).astype(x.dtype).reshape(B, T, -1) * jax.nn.silu(r_l)
    x = x + jnp.concatenate([o_nsa, o_gla], axis=-1) @ w_out

    hq = rms_norm(x, g_cross)
    hm = rms_norm(mem, g_mem)
    cq = rms_norm((hq @ w_cq).reshape(B, T, MEM_HEADS, MEM_DH), g_cq) * (MEM_DH ** -0.5)
    ck = rms_norm((hm @ w_ck).reshape(B, M, MEM_HEADS, MEM_DH), g_ck)
    cv = (hm @ w_cv).reshape(B, M, MEM_HEADS, MEM_DH)
    s = jnp.einsum('bthd,bmhd->bhtm', cq, ck).astype(jnp.float32)
    p = jax.nn.softmax(s, axis=-1).astype(cv.dtype)
    x = x + jnp.einsum('bhtm,bmhd->bthd', p, cv).reshape(B, T, -1) @ w_co

    hf = rms_norm(x, g_ffn)
    gg, uu = jnp.split(hf @ w_gu, 2, axis=-1)
    return x + (jax.nn.silu(gg) * uu) @ w_down


def setup_inputs(seed: int = 0) -> dict:
    key = jax.random.key(seed)
    keys = iter(jax.random.split(key, 40))
    L = DEPTH

    def nrm(shape, scale):
        return scale * jax.random.normal(next(keys), shape, jnp.float32)

    def gain(n):
        return 1.0 + nrm((L, n), 0.05)

    return {
        'x': nrm((BATCH, SEQ, D_MODEL), 1.0),
        'mem': nrm((BATCH, MEM_LEN, D_MODEL), 1.0),
        'g_mix': gain(D_MODEL),
        'w_in': nrm((L, D_MODEL, D_IN), D_MODEL ** -0.5),
        'b_nsa_gate': nrm((L, 3 * NSA_HEADS), 0.1),
        'g_q': gain(NSA_DK),
        'g_kc': gain(NSA_DK),
        'g_ks': gain(NSA_DK),
        'g_kw': gain(NSA_DK),
        'pe_k': nrm((L, CMP_LEN, NSA_DK), 0.5),
        'pe_v': nrm((L, CMP_LEN, NSA_DV), 0.5),
        'w_ck1': nrm((L, CMP_LEN * NSA_DK, CMP_HIDDEN), (CMP_LEN * NSA_DK) ** -0.5),
        'w_ck2': nrm((L, CMP_HIDDEN, NSA_DK), CMP_HIDDEN ** -0.5),
        'w_cv1': nrm((L, CMP_LEN * NSA_DV, CMP_HIDDEN), (CMP_LEN * NSA_DV) ** -0.5),
        'w_cv2': nrm((L, CMP_HIDDEN, NSA_DV), CMP_HIDDEN ** -0.5),
        'g_nsa_out': gain(NSA_DV),
        'w_gk2': nrm((L, GLA_GATE_RANK, GLA_HEADS * GLA_DK), GLA_GATE_RANK ** -0.5),
        'b_gk': nrm((L, GLA_HEADS * GLA_DK), 0.1),
        'g_gla_out': gain(GLA_DV),
        'w_out': nrm((L, D_MIX, D_MODEL), D_MIX ** -0.5),
        'g_cross': gain(D_MODEL),
        'g_mem': gain(D_MODEL),
        'w_cq': nrm((L, D_MODEL, MEM_HEADS * MEM_DH), D_MODEL ** -0.5),
        'w_ck': nrm((L, D_MODEL, MEM_HEADS * MEM_DH), D_MODEL ** -0.5),
        'w_cv': nrm((L, D_MODEL, MEM_HEADS * MEM_DH), D_MODEL ** -0.5),
        'g_cq': gain(MEM_DH),
        'g_ck': gain(MEM_DH),
        'w_co': nrm((L, MEM_HEADS * MEM_DH, D_MODEL), (MEM_HEADS * MEM_DH) ** -0.5),
        'g_ffn': gain(D_MODEL),
        'w_gu': nrm((L, D_MODEL, 2 * D_FF), D_MODEL ** -0.5),
        'w_down': nrm((L, D_FF, D_MODEL), D_FF ** -0.5),
    }


def reference(x, mem, g_mix, w_in, b_nsa_gate, g_q, g_kc, g_ks, g_kw, pe_k, pe_v,
              w_ck1, w_ck2, w_cv1, w_cv2, g_nsa_out, w_gk2, b_gk, g_gla_out, w_out,
              g_cross, g_mem, w_cq, w_ck, w_cv, g_cq, g_ck, w_co, g_ffn, w_gu, w_down):
    for l in range(DEPTH):
        x = hybrid_layer(x, mem, g_mix[l], w_in[l], b_nsa_gate[l], g_q[l], g_kc[l], g_ks[l],
                         g_kw[l], pe_k[l], pe_v[l], w_ck1[l], w_ck2[l], w_cv1[l], w_cv2[l],
                         g_nsa_out[l], w_gk2[l], b_gk[l], g_gla_out[l], w_out[l],
                         g_cross[l], g_mem[l], w_cq[l], w_ck[l], w_cv[l], g_cq[l], g_ck[l],
                         w_co[l], g_ffn[l], w_gu[l], w_down[l])
    return x
```

```python
import functools

import numpy as np
import jax
import jax.numpy as jnp
from jax import lax
from jax.experimental import pallas as pl
from jax.experimental.pallas import tpu as pltpu

F32 = jnp.float32
BF16 = jnp.bfloat16

D_MODEL = 2048
MEM_LEN = 256
EPS = 1e-6
NSA_HEADS = 8
NSA_KV_HEADS = 2
NSA_GROUP = NSA_HEADS // NSA_KV_HEADS
NSA_DK = 128
NSA_DV = 128
CMP_LEN = 32
CMP_STRIDE = 16
CMP_HIDDEN = 256
SEL_BLOCK = 64
SEL_TOP = 16
WINDOW = 512
GLA_HEADS = 4
GLA_DK = 128
GLA_DV = 256
GLA_GATE_RANK = 16
GLA_GATE_NORM = 16.0
GLA_CHUNK = 64
MEM_HEADS = 4
MEM_DH = 128
D_FF = -(-8 * D_MODEL // (3 * 256)) * 256
D_NSA_OUT = NSA_HEADS * NSA_DV
D_GLA_OUT = GLA_HEADS * GLA_DV

LANES = 128
MXU_DIM = 256
VMEM_BYTES_V7X = 64 * 1024 * 1024

COL_Q = 0
COL_KV = COL_Q + NSA_HEADS * NSA_DK
COL_QL = COL_KV + 6 * NSA_KV_HEADS * NSA_DK
COL_KL = COL_QL + GLA_HEADS * GLA_DK
COL_VL = COL_KL + GLA_HEADS * GLA_DK
COL_RL = COL_VL + GLA_HEADS * GLA_DV
COL_MISC = COL_RL + GLA_HEADS * GLA_DV
MISC_GATE = 0
MISC_AL = 32
D_IN_PACKED = 6144
IN_TILE_N = 2048


def _cparams(semantics, vmem_mb):
    return pltpu.CompilerParams(dimension_semantics=semantics,
                                vmem_limit_bytes=vmem_mb * 1024 * 1024)


def _rms(u, g):
    return u * lax.rsqrt(jnp.mean(u * u, axis=-1, keepdims=True) + EPS) * g


def _inproj_kernel(x_ref, g_ref, w_ref, o_ref, h_ref):
    @pl.when(pl.program_id(1) == 0)
    def _():
        h_ref[...] = _rms(x_ref[...], g_ref[...]).astype(BF16)

    o_ref[...] = jnp.dot(h_ref[...], w_ref[...], preferred_element_type=F32)


def _in_proj(x2d, g_mix, w_packed, tm=512):
    n = x2d.shape[0]
    return pl.pallas_call(
        _inproj_kernel,
        grid=(n // tm, D_IN_PACKED // IN_TILE_N),
        in_specs=[
            pl.BlockSpec((tm, D_MODEL), lambda i, j: (i, 0)),
            pl.BlockSpec((1, D_MODEL), lambda i, j: (0, 0)),
            pl.BlockSpec((D_MODEL, IN_TILE_N), lambda i, j: (0, j)),
        ],
        out_specs=pl.BlockSpec((tm, IN_TILE_N), lambda i, j: (i, j)),
        out_shape=jax.ShapeDtypeStruct((n, D_IN_PACKED), F32),
        scratch_shapes=[pltpu.VMEM((tm, D_MODEL), BF16)],
        compiler_params=_cparams(("parallel", "arbitrary"), 48),
    )(x2d, g_mix.reshape(1, D_MODEL), w_packed)


GATE_ROWS = 16


def _gate_layout(u):
    pos = np.arange(MISC_AL)
    k, r = pos // GATE_ROWS, pos % GATE_ROWS
    br, g = r // NSA_GROUP, r % NSA_GROUP
    used = r < 3 * NSA_GROUP
    src = np.where(used, (k * NSA_GROUP + g) * 3 + br, 0)
    return jnp.where(jnp.asarray(used), jnp.take(u, jnp.asarray(src), axis=-1), 0.0)


def _pack_w_in(w_in):
    sizes = (NSA_HEADS * NSA_DK,) + (NSA_KV_HEADS * NSA_DK,) * 6 + (3 * NSA_HEADS,) + (
        GLA_HEADS * GLA_DK, GLA_HEADS * GLA_DK, GLA_HEADS * GLA_DV, GLA_GATE_RANK, GLA_HEADS * GLA_DV)
    offs = np.concatenate([[0], np.cumsum(sizes)])
    seg = lambda i: w_in[:, offs[i]:offs[i + 1]]
    zeros = lambda n: jnp.zeros((D_MODEL, n), w_in.dtype)
    misc = jnp.concatenate([_gate_layout(seg(7)), seg(11),
                            zeros(LANES - MISC_AL - GLA_GATE_RANK)], axis=1)
    cols = [seg(0)] + [seg(i) for i in range(1, 7)] + [seg(8), seg(9), seg(10), seg(12), misc,
                                                      zeros(D_IN_PACKED - COL_MISC - LANES)]
    return jnp.concatenate(cols, axis=1).astype(BF16)


def _memkv_kernel(mem_ref, gm_ref, wk_ref, wv_ref, gk_ref, ck_ref, cv_ref):
    hm = _rms(mem_ref[...], gm_ref[...]).astype(BF16)
    k = jnp.dot(hm, wk_ref[...], preferred_element_type=F32)
    v = jnp.dot(hm, wv_ref[...], preferred_element_type=F32)
    for h in range(MEM_HEADS):
        sl = slice(h * MEM_DH, (h + 1) * MEM_DH)
        ck_ref[h] = _rms(k[:, sl], gk_ref[...]).astype(BF16)
        cv_ref[h] = v[:, sl].astype(BF16)


def _mem_kv(mem, g_mem, w_ck, w_cv, g_ck):
    b, m, _ = mem.shape
    dm = MEM_HEADS * MEM_DH
    out = jax.ShapeDtypeStruct((b, MEM_HEADS, m, MEM_DH), BF16)
    return pl.pallas_call(
        _memkv_kernel,
        grid=(b,),
        in_specs=[
            pl.BlockSpec((None, m, D_MODEL), lambda i: (i, 0, 0)),
            pl.BlockSpec((1, D_MODEL), lambda i: (0, 0)),
            pl.BlockSpec((D_MODEL, dm), lambda i: (0, 0)),
            pl.BlockSpec((D_MODEL, dm), lambda i: (0, 0)),
            pl.BlockSpec((1, MEM_DH), lambda i: (0, 0)),
        ],
        out_specs=[pl.BlockSpec((None, MEM_HEADS, m, MEM_DH), lambda i: (i, 0, 0, 0))] * 2,
        out_shape=[out, out],
        compiler_params=_cparams(("parallel",), 32),
    )(mem, g_mem.reshape(1, D_MODEL), w_ck.astype(BF16), w_cv.astype(BF16), g_ck.reshape(1, MEM_DH))


def _outx_kernel(x_ref, nsa_ref, gla_ref, wo1_ref, wo2_ref, gc_ref, wcq_ref, gcq_ref,
                 ck_ref, cv_ref, wco_ref, o_ref):
    x1 = (x_ref[...]
          + jnp.dot(nsa_ref[...], wo1_ref[...], preferred_element_type=F32)
          + jnp.dot(gla_ref[...], wo2_ref[...], preferred_element_type=F32))
    hq = _rms(x1, gc_ref[...]).astype(BF16)
    cq = jnp.dot(hq, wcq_ref[...], preferred_element_type=F32)
    outs = []
    for h in range(MEM_HEADS):
        c = _rms(cq[:, h * MEM_DH:(h + 1) * MEM_DH], gcq_ref[...]) * (MEM_DH ** -0.5)
        s = lax.dot_general(c.astype(BF16), ck_ref[h], (((1,), (1,)), ((), ())),
                            preferred_element_type=F32)
        p = jnp.exp(s - jnp.max(s, axis=-1, keepdims=True))
        p = p / jnp.sum(p, axis=-1, keepdims=True)
        outs.append(jnp.dot(p.astype(BF16), cv_ref[h], preferred_element_type=F32))
    oc = jnp.concatenate(outs, axis=-1).astype(BF16)
    o_ref[...] = x1 + jnp.dot(oc, wco_ref[...], preferred_element_type=F32)


def _out_cross(x2d, o_nsa, o_gla, w_out, g_cross, w_cq, g_cq, ck, cv, w_co, seq, tm=256):
    n = x2d.shape[0]
    tiles_per_batch = seq // tm
    dm = MEM_HEADS * MEM_DH
    m = ck.shape[2]
    full = lambda shape: pl.BlockSpec(shape, lambda i: (0,) * len(shape))
    kv_spec = pl.BlockSpec((None, MEM_HEADS, m, MEM_DH), lambda i: (i // tiles_per_batch, 0, 0, 0))
    w_out = w_out.astype(BF16)
    return pl.pallas_call(
        _outx_kernel,
        grid=(n // tm,),
        in_specs=[
            pl.BlockSpec((tm, D_MODEL), lambda i: (i, 0)),
            pl.BlockSpec((tm, D_NSA_OUT), lambda i: (i, 0)),
            pl.BlockSpec((tm, D_GLA_OUT), lambda i: (i, 0)),
            full((D_NSA_OUT, D_MODEL)),
            full((D_GLA_OUT, D_MODEL)),
            full((1, D_MODEL)),
            full((D_MODEL, dm)),
            full((1, MEM_DH)),
            kv_spec,
            kv_spec,
            full((dm, D_MODEL)),
        ],
        out_specs=pl.BlockSpec((tm, D_MODEL), lambda i: (i, 0)),
        out_shape=jax.ShapeDtypeStruct((n, D_MODEL), F32),
        compiler_params=_cparams(("parallel",), 48),
    )(x2d, o_nsa, o_gla, w_out[:D_NSA_OUT], w_out[D_NSA_OUT:], g_cross.reshape(1, D_MODEL),
      w_cq.astype(BF16), g_cq.reshape(1, MEM_DH), ck, cv, w_co.astype(BF16))


def _ffn_kernel(x_ref, g_ref, wg_ref, wu_ref, wd_ref, o_ref, h_ref):
    @pl.when(pl.program_id(1) == 0)
    def _():
        x = x_ref[...]
        h_ref[...] = _rms(x, g_ref[...]).astype(BF16)
        o_ref[...] = x

    h = h_ref[...]
    g = jnp.dot(h, wg_ref[...], preferred_element_type=F32)
    u = jnp.dot(h, wu_ref[...], preferred_element_type=F32)
    a = (g * jax.nn.sigmoid(g) * u).astype(BF16)
    o_ref[...] += jnp.dot(a, wd_ref[...], preferred_element_type=F32)


def _ffn(x2d, g_ffn, w_gu, w_down, tm=512, tf=512):
    n = x2d.shape[0]
    w_gu = w_gu.astype(BF16)
    return pl.pallas_call(
        _ffn_kernel,
        grid=(n // tm, D_FF // tf),
        in_specs=[
            pl.BlockSpec((tm, D_MODEL), lambda i, j: (i, 0)),
            pl.BlockSpec((1, D_MODEL), lambda i, j: (0, 0)),
            pl.BlockSpec((D_MODEL, tf), lambda i, j: (0, j)),
            pl.BlockSpec((D_MODEL, tf), lambda i, j: (0, j)),
            pl.BlockSpec((tf, D_MODEL), lambda i, j: (j, 0)),
        ],
        out_specs=pl.BlockSpec((tm, D_MODEL), lambda i, j: (i, 0)),
        out_shape=jax.ShapeDtypeStruct((n, D_MODEL), F32),
        scratch_shapes=[pltpu.VMEM((tm, D_MODEL), BF16)],
        compiler_params=_cparams(("parallel", "arbitrary"), 48),
    )(x2d, g_ffn.reshape(1, D_MODEL), w_gu[:, :D_FF], w_gu[:, D_FF:], w_down.astype(BF16))


def _split3(x):
    hi = x.astype(BF16)
    r = x - hi.astype(F32)
    mid = r.astype(BF16)
    lo = (r - mid.astype(F32)).astype(BF16)
    return hi, mid, lo


def _dot_exact_lhs(a_bf16, x):
    return sum(jnp.dot(a_bf16, p, preferred_element_type=F32) for p in _split3(x))


def _dot_split(a, w):
    a_hi, a_lo, _ = _split3(a)
    w_hi, w_lo, _ = _split3(w)
    d = lambda p, q: jnp.dot(p, q, preferred_element_type=F32)
    return d(a_hi, w_hi) + d(a_hi, w_lo) + d(a_lo, w_hi)


_NT = (((1,), (1,)), ((), ()))
_TN = (((0,), (0,)), ((), ()))
GLA_DIRECT = 8


def _gla_kernel(ql_ref, kl_ref, vl_ref, rl_ref, misc_ref, w2_ref, bg_ref, go_ref, tri_ref,
                o_ref, st_ref, la_ref):
    c_len = GLA_CHUNK

    @pl.when(pl.program_id(2) == 0)
    def _():
        st_ref[...] = jnp.zeros_like(st_ref)

    x = _dot_split(misc_ref[...], w2_ref[...]) + bg_ref[...]
    la_ref[...] = (jnp.minimum(x, 0.0) - jnp.log(1.0 + jnp.exp(-jnp.abs(x)))) * (1.0 / GLA_GATE_NORM)

    row = lax.broadcasted_iota(jnp.int32, (c_len, 1), 0)
    col = lax.broadcasted_iota(jnp.int32, (1, c_len), 1)
    tri = tri_ref[...]

    def chunk(c, carry):
        r0 = pl.multiple_of(c * c_len, c_len)
        rows = pl.ds(r0, c_len)
        la = la_ref[rows, :]
        q = ql_ref[rows, :] * (GLA_DK ** -0.5)
        k = kl_ref[rows, :]
        v = vl_ref[rows, :].astype(BF16)
        b = _dot_exact_lhs(tri, la)

        attn = jnp.zeros((c_len, c_len), F32)
        s = c_len
        while s > GLA_DIRECT:
            half = s // 2
            ref = jnp.concatenate(
                [jnp.broadcast_to(b[m0 + half - 1:m0 + half, :], (s, GLA_DK))
                 for m0 in range(0, c_len, s)], axis=0)
            second = (row % s) >= half
            qs = jnp.where(second, q * jnp.exp(jnp.minimum(b - ref, 0.0)), 0.0)
            ks = jnp.where(second, 0.0, k * jnp.exp(jnp.minimum(ref - b, 0.0)))
            a_s = lax.dot_general(qs.astype(BF16), ks.astype(BF16), _NT, preferred_element_type=F32)
            attn += jnp.where((row // s) == (col // s), a_s, 0.0)
            s = half
        for d in range(GLA_DIRECT):
            k_sh = k if d == 0 else pltpu.roll(k, d, 0)
            b_sh = b if d == 0 else pltpu.roll(b, d, 0)
            w = q * k_sh * jnp.exp(jnp.minimum(b - b_sh, 0.0))
            cd = jnp.sum(w, axis=-1, keepdims=True)
            hit = ((row % GLA_DIRECT) >= d) & (col == row - d)
            attn += jnp.where(hit, cd, 0.0)

        st = st_ref[...]
        b_end = b[c_len - 1:c_len, :]
        o = (jnp.dot(attn.astype(BF16), v, preferred_element_type=F32)
             + lax.dot_general((q * jnp.exp(b)).astype(BF16), st.astype(BF16), _NT,
                               preferred_element_type=F32))
        kd = (k * jnp.exp(b_end - b)).astype(BF16)
        st_ref[...] = st * jnp.exp(b_end) + lax.dot_general(v, kd, _TN, preferred_element_type=F32)

        r = rl_ref[rows, :]
        o_ref[rows, :] = (_rms(o, go_ref[...]) * (r * jax.nn.sigmoid(r))).astype(o_ref.dtype)
        return carry

    lax.fori_loop(0, ql_ref.shape[0] // c_len, chunk, 0)


def _gla(proj, w_gk2, b_gk, g_gla_out, batch, seq, tc=512):
    n = proj.shape[0]
    nt = seq // tc
    w2 = jnp.zeros((LANES, GLA_HEADS * GLA_DK), F32).at[MISC_AL:MISC_AL + GLA_GATE_RANK].set(w_gk2)
    tri = jnp.asarray(np.tril(np.ones((GLA_CHUNK, GLA_CHUNK), np.float32)), BF16)
    rows = lambda b, h, i: b * nt + i
    return pl.pallas_call(
        _gla_kernel,
        grid=(batch, GLA_HEADS, nt),
        in_specs=[
            pl.BlockSpec((tc, GLA_DK), lambda b, h, i: (rows(b, h, i), COL_QL // GLA_DK + h)),
            pl.BlockSpec((tc, GLA_DK), lambda b, h, i: (rows(b, h, i), COL_KL // GLA_DK + h)),
            pl.BlockSpec((tc, GLA_DV), lambda b, h, i: (rows(b, h, i), COL_VL // GLA_DV + h)),
            pl.BlockSpec((tc, GLA_DV), lambda b, h, i: (rows(b, h, i), COL_RL // GLA_DV + h)),
            pl.BlockSpec((tc, LANES), lambda b, h, i: (rows(b, h, i), COL_MISC // LANES)),
            pl.BlockSpec((LANES, GLA_DK), lambda b, h, i: (0, h)),
            pl.BlockSpec((1, GLA_DK), lambda b, h, i: (0, h)),
            pl.BlockSpec((1, GLA_DV), lambda b, h, i: (0, 0)),
            pl.BlockSpec((GLA_CHUNK, GLA_CHUNK), lambda b, h, i: (0, 0)),
        ],
        out_specs=pl.BlockSpec((tc, GLA_DV), lambda b, h, i: (rows(b, h, i), h)),
        out_shape=jax.ShapeDtypeStruct((n, D_GLA_OUT), BF16),
        scratch_shapes=[pltpu.VMEM((GLA_DV, GLA_DK), F32), pltpu.VMEM((tc, GLA_DK), F32)],
        compiler_params=_cparams(("parallel", "parallel", "arbitrary"), 32),
    )(proj, proj, proj, proj, proj, w2, b_gk.reshape(1, -1), g_gla_out.reshape(1, GLA_DV), tri)


NSA_TQ = 128
NSA_NQ = NSA_TQ * NSA_GROUP
NEG = -1e30
SLC_CHUNK = 512
WIN_KEYS = WINDOW + NSA_TQ


def _nsa_prep_kernel(q_ref, cmp_ref, slc_ref, win_ref, misc_ref, gq_ref, gks_ref, gkw_ref, bg_ref,
                     qt_ref, kcm_ref, vcm_ref, ks_ref, vst_ref, kw_ref, vwt_ref, gt_ref):
    q = q_ref[...]
    for k in range(NSA_KV_HEADS):
        for g in range(NSA_GROUP):
            h = k * NSA_GROUP + g
            qn = _rms(q[:, h * NSA_DK:(h + 1) * NSA_DK], gq_ref[...]) * (NSA_DK ** -0.5)
            qt_ref[k, :, g * NSA_TQ:(g + 1) * NSA_TQ] = qn.T.astype(BF16)
        ksl = slice(k * NSA_DK, (k + 1) * NSA_DK)
        vsl = slice((NSA_KV_HEADS + k) * NSA_DK, (NSA_KV_HEADS + k + 1) * NSA_DK)
        kcm_ref[k] = cmp_ref[:, ksl]
        vcm_ref[k] = cmp_ref[:, vsl]
        ks_ref[k] = _rms(slc_ref[:, ksl], gks_ref[...]).astype(BF16)
        vst_ref[k] = slc_ref[:, vsl].T.astype(BF16)
        kw_ref[k] = _rms(win_ref[:, ksl], gkw_ref[...]).astype(BF16)
        vwt_ref[k] = win_ref[:, vsl].T.astype(BF16)
    gates = jax.nn.sigmoid(misc_ref[...] + bg_ref[...])
    gt_ref[...] = gates.T[:NSA_KV_HEADS * GATE_ROWS, :]


def _nsa_prep(proj, g_q, g_ks, g_kw, b_gate, batch, seq):
    nqt = seq // NSA_TQ
    hk = NSA_KV_HEADS
    pair = 2 * hk * NSA_DK
    rows = lambda b, i: b * nqt + i
    vec = lambda n: pl.BlockSpec((1, n), lambda b, i: (0, 0))
    bias = jnp.zeros((1, LANES), F32).at[0, :MISC_AL].set(_gate_layout(b_gate))
    tok = lambda dt: jax.ShapeDtypeStruct((batch, hk, seq, NSA_DK), dt)
    tr = jax.ShapeDtypeStruct((batch, hk, nqt, NSA_DV, NSA_TQ), BF16)
    tok_spec = pl.BlockSpec((None, hk, NSA_TQ, NSA_DK), lambda b, i: (b, 0, i, 0))
    tr_spec = pl.BlockSpec((None, hk, None, NSA_DV, NSA_TQ), lambda b, i: (b, 0, i, 0, 0))
    return pl.pallas_call(
        _nsa_prep_kernel,
        grid=(batch, nqt),
        in_specs=[
            pl.BlockSpec((NSA_TQ, NSA_HEADS * NSA_DK), lambda b, i: (rows(b, i), 0)),
            pl.BlockSpec((NSA_TQ, pair), lambda b, i: (rows(b, i), COL_KV // pair)),
            pl.BlockSpec((NSA_TQ, pair), lambda b, i: (rows(b, i), COL_KV // pair + 1)),
            pl.BlockSpec((NSA_TQ, pair), lambda b, i: (rows(b, i), COL_KV // pair + 2)),
            pl.BlockSpec((NSA_TQ, LANES), lambda b, i: (rows(b, i), COL_MISC // LANES)),
            vec(NSA_DK), vec(NSA_DK), vec(NSA_DK), vec(LANES),
        ],
        out_specs=[
            pl.BlockSpec((None, hk, None, NSA_DK, NSA_NQ), lambda b, i: (b, 0, i, 0, 0)),
            tok_spec, tok_spec, tok_spec, tr_spec, tok_spec, tr_spec,
            pl.BlockSpec((None, hk * GATE_ROWS, NSA_TQ), lambda b, i: (b, 0, i)),
        ],
        out_shape=[
            jax.ShapeDtypeStruct((batch, hk, nqt, NSA_DK, NSA_NQ), BF16),
            tok(F32), tok(F32), tok(BF16), tr, tok(BF16), tr,
            jax.ShapeDtypeStruct((batch, hk * GATE_ROWS, seq), F32),
        ],
        compiler_params=_cparams(("parallel", "parallel"), 32),
    )(proj, proj, proj, proj, proj, g_q.reshape(1, -1), g_ks.reshape(1, -1), g_kw.reshape(1, -1), bias)


def _compress_kernel(uk_ref, uv_ref, pek_ref, pev_ref, wk1_ref, wk2_ref, wv1_ref, wv2t_ref, gk_ref,
                     kc_ref, vct_ref):
    def hidden(u_ref, pe_ref, w1_ref):
        u = u_ref[...]
        n = u.shape[0]
        lo = jnp.dot((u + pe_ref[0:1, :]).astype(BF16), w1_ref[0], preferred_element_type=F32)
        hi = jnp.dot((u + pe_ref[1:2, :]).astype(BF16), w1_ref[1], preferred_element_type=F32)
        hid = lo + pltpu.roll(hi, n - 1, 0)
        return (hid * jax.nn.sigmoid(hid)).astype(BF16)

    kc = jnp.dot(hidden(uk_ref, pek_ref, wk1_ref), wk2_ref[...], preferred_element_type=F32)
    kc_ref[...] = _rms(kc, gk_ref[...]).astype(BF16)
    vct_ref[...] = lax.dot_general(wv2t_ref[...], hidden(uv_ref, pev_ref, wv1_ref), _NT,
                                   preferred_element_type=F32).astype(BF16)


def _compress(k_tok, v_tok, pe_k, pe_v, w_ck1, w_ck2, w_cv1, w_cv2, g_kc):
    batch, hk, seq, d = k_tok.shape
    nc = seq // CMP_STRIDE
    flat = CMP_STRIDE * d
    halves = CMP_LEN // CMP_STRIDE
    uk = k_tok.reshape(batch, hk, nc, flat)
    uv = v_tok.reshape(batch, hk, nc, flat)
    u_spec = pl.BlockSpec((None, None, nc, flat), lambda b, k: (b, k, 0, 0))
    full = lambda shape: pl.BlockSpec(shape, lambda b, k: (0,) * len(shape))
    return pl.pallas_call(
        _compress_kernel,
        grid=(batch, hk),
        in_specs=[u_spec, u_spec, full((halves, flat)), full((halves, flat)),
                  full((halves, flat, CMP_HIDDEN)), full((CMP_HIDDEN, d)),
                  full((halves, flat, CMP_HIDDEN)), full((d, CMP_HIDDEN)), full((1, d))],
        out_specs=[pl.BlockSpec((None, None, nc, d), lambda b, k: (b, k, 0, 0)),
                   pl.BlockSpec((None, None, d, nc), lambda b, k: (b, k, 0, 0))],
        out_shape=[jax.ShapeDtypeStruct((batch, hk, nc, d), BF16),
                   jax.ShapeDtypeStruct((batch, hk, d, nc), BF16)],
        compiler_params=_cparams(("parallel", "parallel"), 48),
    )(uk, uv, pe_k.reshape(halves, flat), pe_v.reshape(halves, flat),
      w_ck1.astype(BF16).reshape(halves, flat, CMP_HIDDEN), w_ck2.astype(BF16),
      w_cv1.astype(BF16).reshape(halves, flat, CMP_HIDDEN), w_cv2.T.astype(BF16), g_kc.reshape(1, d))


def _query_rows(qt):
    lane = lax.broadcasted_iota(jnp.int32, (1, NSA_NQ), 1)
    t = qt * NSA_TQ + (lane & (NSA_TQ - 1))
    return t


def _topk_mask(imp, cur, n_top):
    ns, nt = imp.shape
    j = lax.broadcasted_iota(jnp.int32, (ns, nt), 0)
    forced = (j == 0) | (j == cur) | (j == cur - 1)
    visible = j <= cur
    work = jnp.where(visible & jnp.logical_not(forced), imp, -1.0)

    def extract(_, work):
        m = jnp.max(work, axis=0, keepdims=True)
        cand = (work == m) & (m > -0.5)
        idx = jnp.min(jnp.where(cand, j, ns), axis=0, keepdims=True)
        return jnp.where(j == idx, -2.0, work)

    work = lax.fori_loop(0, n_top - 3, extract, work)
    return (forced & visible) | (work == -2.0)


def _nsa_sel_kernel(qt_ref, kc_ref, vct_ref, mmat_ref, slope_ref, sel_ref, flag_ref, oc_ref, *, n_top):
    qt = pl.program_id(2)
    t0 = qt * NSA_TQ
    t = _query_rows(qt)
    slope = slope_ref[...]
    nc = kc_ref.shape[0]
    s = jnp.dot(kc_ref[...], qt_ref[...], preferred_element_type=F32)
    ci = lax.broadcasted_iota(jnp.int32, (nc, 1), 0)
    vis = t >= ci * CMP_STRIDE + (CMP_LEN - 1)
    mid = (ci * CMP_STRIDE - t0).astype(F32) + 0.5 * (CMP_LEN - 1)
    s = jnp.where(vis, s + mid * slope, NEG)
    m = jnp.max(s, axis=0, keepdims=True)
    p = jnp.where(vis, jnp.exp(s - m), 0.0)
    l = jnp.sum(p, axis=0, keepdims=True)
    p = p / jnp.where(l > 0.0, l, 1.0)
    oc_ref[...] = jnp.dot(vct_ref[...], p.astype(BF16), preferred_element_type=F32)
    psum = sum(p[:, g * NSA_TQ:(g + 1) * NSA_TQ] for g in range(NSA_GROUP))
    imp = _dot_exact_lhs(mmat_ref[...], psum)
    cur = lax.shift_right_logical(t[:, :NSA_TQ], 6)
    sel = _topk_mask(imp, cur, n_top).astype(F32)
    sel_ref[...] = sel
    flag_ref[...] = jnp.max(sel.T, axis=0, keepdims=True).astype(jnp.int32)


def _nsa_select(qt_arr, kc, vct, batch, seq):
    hk = NSA_KV_HEADS
    nqt = seq // NSA_TQ
    nc = seq // CMP_STRIDE
    ns = seq // SEL_BLOCK
    n_cmp = nc - (CMP_LEN // CMP_STRIDE - 1)
    n_top = min(SEL_TOP, ns)
    ratio, lead = SEL_BLOCK // CMP_STRIDE, CMP_LEN // CMP_STRIDE - 1
    mm = np.zeros((ns, nc), np.float32)
    for r in range(ratio + lead):
        st = CMP_STRIDE * (r - lead)
        ov = (min(st + CMP_LEN, SEL_BLOCK) - max(st, 0)) / CMP_STRIDE
        for jb in range(ns):
            i = jb * ratio + r - lead
            if 0 <= i < n_cmp:
                mm[jb, i] += ov
    slopes = jnp.exp2(-8.0 * jnp.arange(1, NSA_HEADS + 1, dtype=F32) / NSA_HEADS)
    slopes = jnp.repeat(slopes.reshape(hk, 1, NSA_GROUP), NSA_TQ, axis=2)
    sel, flags, oc = pl.pallas_call(
        functools.partial(_nsa_sel_kernel, n_top=n_top),
        grid=(batch, hk, nqt),
        in_specs=[
            pl.BlockSpec((None, None, None, NSA_DK, NSA_NQ), lambda b, k, i: (b, k, i, 0, 0)),
            pl.BlockSpec((None, None, nc, NSA_DK), lambda b, k, i: (b, k, 0, 0)),
            pl.BlockSpec((None, None, NSA_DV, nc), lambda b, k, i: (b, k, 0, 0)),
            pl.BlockSpec((ns, nc), lambda b, k, i: (0, 0)),
            pl.BlockSpec((None, 1, NSA_NQ), lambda b, k, i: (k, 0, 0)),
        ],
        out_specs=[
            pl.BlockSpec((None, None, None, ns, NSA_TQ), lambda b, k, i: (b, k, i, 0, 0)),
            pl.BlockSpec((None, None, None, 1, ns), lambda b, k, i: (b, k, i, 0, 0)),
            pl.BlockSpec((None, None, None, NSA_DV, NSA_NQ), lambda b, k, i: (b, k, i, 0, 0)),
        ],
        out_shape=[
            jax.ShapeDtypeStruct((batch, hk, nqt, ns, NSA_TQ), F32),
            jax.ShapeDtypeStruct((batch, hk, nqt, 1, ns), jnp.int32),
            jax.ShapeDtypeStruct((batch, hk, nqt, NSA_DV, NSA_NQ), F32),
        ],
        compiler_params=_cparams(("parallel", "parallel", "parallel"), 48),
    )(qt_arr, kc, vct, jnp.asarray(mm, BF16), slopes)
    return sel, flags, oc, slopes


def _nsa_attn_kernel(flag_ref, qt_ref, ks_ref, vst_ref, kw_ref, vwt_ref, sel_ref, oc_ref, gt_ref,
                     slope_ref, go_ref, o_ref, m_ref, l_ref, acc_ref):
    qt = pl.program_id(2)
    t0 = qt * NSA_TQ
    t = _query_rows(qt)
    slope = slope_ref[...]
    q = qt_ref[...]

    m_ref[...] = jnp.full_like(m_ref, NEG)
    l_ref[...] = jnp.zeros_like(l_ref)
    acc_ref[...] = jnp.zeros_like(acc_ref)
    blocks = SLC_CHUNK // SEL_BLOCK
    sub = SLC_CHUNK // NSA_TQ
    ki = lax.broadcasted_iota(jnp.int32, (SLC_CHUNK, 1), 0)

    def chunk(c, carry):
        active = flag_ref[0, c * blocks]
        for r in range(1, blocks):
            active = active | flag_ref[0, c * blocks + r]

        @pl.when(active > 0)
        def _():
            k0 = pl.multiple_of(c * SLC_CHUNK, SLC_CHUNK)
            s = jnp.dot(ks_ref[pl.ds(k0, SLC_CHUNK), :], q, preferred_element_type=F32)
            pos = k0 + ki
            sel8 = sel_ref[pl.ds(pl.multiple_of(c * blocks, blocks), blocks), :]
            mask = jnp.concatenate(
                [jnp.broadcast_to(sel8[r:r + 1, :], (SEL_BLOCK, NSA_TQ)) for r in range(blocks)], axis=0)
            mask = jnp.concatenate([mask] * NSA_GROUP, axis=1)
            valid = (mask > 0.5) & (pos <= t)
            s = jnp.where(valid, s + (pos - t0).astype(F32) * slope, NEG)
            m_old = m_ref[...]
            m_new = jnp.maximum(m_old, jnp.max(s, axis=0, keepdims=True))
            p = jnp.where(valid, jnp.exp(s - m_new), 0.0)
            alpha = jnp.exp(m_old - m_new)
            l_ref[...] = alpha * l_ref[...] + jnp.sum(p, axis=0, keepdims=True)
            vt = jnp.concatenate([vst_ref[c * sub + i] for i in range(sub)], axis=1)
            acc_ref[...] = alpha * acc_ref[...] + jnp.dot(vt, p.astype(BF16), preferred_element_type=F32)
            m_ref[...] = m_new

        return carry

    lax.fori_loop(0, t0 // SLC_CHUNK + 1, chunk, 0)

    wsub = WIN_KEYS // NSA_TQ
    wb = jnp.maximum(qt - WINDOW // NSA_TQ, 0)
    k0 = pl.multiple_of(wb * NSA_TQ, NSA_TQ)
    s = jnp.dot(kw_ref[pl.ds(k0, WIN_KEYS), :], q, preferred_element_type=F32)
    pos = k0 + lax.broadcasted_iota(jnp.int32, (WIN_KEYS, 1), 0)
    dw = t - pos
    valid = (dw >= 0) & (dw < WINDOW)
    s = jnp.where(valid, s + (pos - t0).astype(F32) * slope, NEG)
    p = jnp.where(valid, jnp.exp(s - jnp.max(s, axis=0, keepdims=True)), 0.0)
    l_w = jnp.sum(p, axis=0, keepdims=True)
    vt = jnp.concatenate([vwt_ref[wb + i] for i in range(wsub)], axis=1)
    o_w = jnp.dot(vt, p.astype(BF16), preferred_element_type=F32) / l_w

    def gate(br):
        return jnp.concatenate([gt_ref[br * NSA_GROUP + g:br * NSA_GROUP + g + 1, :]
                                for g in range(NSA_GROUP)], axis=1)

    o = gate(0) * oc_ref[...] + gate(1) * (acc_ref[...] / l_ref[...]) + gate(2) * o_w
    o = o * lax.rsqrt(jnp.mean(o * o, axis=0, keepdims=True) + EPS) * go_ref[...]
    for g in range(NSA_GROUP):
        sl = slice(g * NSA_TQ, (g + 1) * NSA_TQ)
        o_ref[:, g * NSA_DV:(g + 1) * NSA_DV] = o[:, sl].T.astype(o_ref.dtype)


def _nsa_attend(flags, qt_arr, ks, vst, kw, vwt, sel, oc, gates_t, slopes, g_nsa_out, batch, seq):
    hk = NSA_KV_HEADS
    nqt = seq // NSA_TQ
    ns = seq // SEL_BLOCK
    per_tile = lambda shape: pl.BlockSpec((None, None, None) + shape, lambda b, k, i: (b, k, i, 0, 0))
    tok = pl.BlockSpec((None, None, seq, NSA_DK), lambda b, k, i: (b, k, 0, 0))
    tr = pl.BlockSpec((None, None, nqt, NSA_DV, NSA_TQ), lambda b, k, i: (b, k, 0, 0, 0))
    return pl.pallas_call(
        _nsa_attn_kernel,
        grid=(batch, hk, nqt),
        in_specs=[
            pl.BlockSpec((None, None, None, 1, ns), lambda b, k, i: (b, k, i, 0, 0),
                         memory_space=pltpu.SMEM),
            per_tile((NSA_DK, NSA_NQ)),
            tok, tr, tok, tr,
            per_tile((ns, NSA_TQ)),
            per_tile((NSA_DV, NSA_NQ)),
            pl.BlockSpec((None, GATE_ROWS, NSA_TQ), lambda b, k, i: (b, k, i)),
            pl.BlockSpec((None, 1, NSA_NQ), lambda b, k, i: (k, 0, 0)),
            pl.BlockSpec((NSA_DV, 1), lambda b, k, i: (0, 0)),
        ],
        out_specs=pl.BlockSpec((NSA_TQ, NSA_GROUP * NSA_DV), lambda b, k, i: (b * nqt + i, k)),
        out_shape=jax.ShapeDtypeStruct((batch * seq, D_NSA_OUT), BF16),
        scratch_shapes=[pltpu.VMEM((1, NSA_NQ), F32), pltpu.VMEM((1, NSA_NQ), F32),
                        pltpu.VMEM((NSA_DV, NSA_NQ), F32)],
        compiler_params=_cparams(("parallel", "parallel", "arbitrary"), 48),
    )(flags, qt_arr, ks, vst, kw, vwt, sel, oc, gates_t, slopes, g_nsa_out.reshape(NSA_DV, 1))


def _nsa(proj, b_gate, g_q, g_kc, g_ks, g_kw, pe_k, pe_v, w_ck1, w_ck2, w_cv1, w_cv2, g_nsa_out,
         batch, seq):
    qt_arr, k_tok, v_tok, ks, vst, kw, vwt, gates_t = _nsa_prep(proj, g_q, g_ks, g_kw, b_gate, batch, seq)
    kc, vct = _compress(k_tok, v_tok, pe_k, pe_v, w_ck1, w_ck2, w_cv1, w_cv2, g_kc)
    sel, flags, oc, slopes = _nsa_select(qt_arr, kc, vct, batch, seq)
    return _nsa_attend(flags, qt_arr, ks, vst, kw, vwt, sel, oc, gates_t, slopes, g_nsa_out, batch, seq)


def kernel(x, mem, g_mix, w_in, b_nsa_gate, g_q, g_kc, g_ks, g_kw, pe_k, pe_v, w_ck1, w_ck2,
           w_cv1, w_cv2, g_nsa_out, w_gk2, b_gk, g_gla_out, w_out, g_cross, g_mem, w_cq, w_ck,
           w_cv, g_cq, g_ck, w_co, g_ffn, w_gu, w_down):
    batch, seq, _ = x.shape
    x2d = x.reshape(batch * seq, D_MODEL)
    l = 0
    proj = _in_proj(x2d, g_mix[l], _pack_w_in(w_in[l]))
    o_gla = _gla(proj, w_gk2[l], b_gk[l], g_gla_out[l], batch, seq)
    o_nsa = _nsa(proj, b_nsa_gate[l], g_q[l], g_kc[l], g_ks[l], g_kw[l], pe_k[l], pe_v[l],
                 w_ck1[l], w_ck2[l], w_cv1[l], w_cv2[l], g_nsa_out[l], batch, seq)
    ck, cv = _mem_kv(mem, g_mem[l], w_ck[l], w_cv[l], g_ck[l])
    x2d = _out_cross(x2d, o_nsa, o_gla, w_out[l], g_cross[l], w_cq[l], g_cq[l], ck, cv,
                     w_co[l], seq)
    x2d = _ffn(x2d, g_ffn[l], w_gu[l], w_down[l])
    return x2d.reshape(batch, seq, D_MODEL)
```

```python
import functools

import numpy as np
import jax
import jax.numpy as jnp
from jax import lax
from jax.experimental import pallas as pl
from jax.experimental.pallas import tpu as pltpu

F32 = jnp.float32
BF16 = jnp.bfloat16

D_MODEL = 2048
MEM_LEN = 256
EPS = 1e-6
NSA_HEADS = 8
NSA_KV_HEADS = 2
NSA_GROUP = NSA_HEADS // NSA_KV_HEADS
NSA_DK = 128
NSA_DV = 128
CMP_LEN = 32
CMP_STRIDE = 16
CMP_HIDDEN = 256
SEL_BLOCK = 64
SEL_TOP = 16
WINDOW = 512
GLA_HEADS = 4
GLA_DK = 128
GLA_DV = 256
GLA_GATE_RANK = 16
GLA_GATE_NORM = 16.0
GLA_CHUNK = 64
MEM_HEADS = 4
MEM_DH = 128
D_FF = -(-8 * D_MODEL // (3 * 256)) * 256
D_NSA_OUT = NSA_HEADS * NSA_DV
D_GLA_OUT = GLA_HEADS * GLA_DV

LANES = 128
MXU_DIM = 256
VMEM_BYTES_V7X = 64 * 1024 * 1024

COL_Q = 0
COL_KV = COL_Q + NSA_HEADS * NSA_DK
COL_QL = COL_KV + 6 * NSA_KV_HEADS * NSA_DK
COL_KL = COL_QL + GLA_HEADS * GLA_DK
COL_VL = COL_KL + GLA_HEADS * GLA_DK
COL_RL = COL_VL + GLA_HEADS * GLA_DV
COL_MISC = COL_RL + GLA_HEADS * GLA_DV
MISC_GATE = 0
MISC_AL = 32
D_IN_PACKED = 6144
IN_TILE_N = 2048


def _cparams(semantics, vmem_mb):
    return pltpu.CompilerParams(dimension_semantics=semantics,
                                vmem_limit_bytes=vmem_mb * 1024 * 1024)


def _rms(u, g):
    return u * lax.rsqrt(jnp.mean(u * u, axis=-1, keepdims=True) + EPS) * g


def _inproj_kernel(x_ref, g_ref, w_ref, o_ref, h_ref):
    @pl.when(pl.program_id(1) == 0)
    def _():
        h_ref[...] = _rms(x_ref[...], g_ref[...]).astype(BF16)

    o_ref[...] = jnp.dot(h_ref[...], w_ref[...], preferred_element_type=F32)


def _in_proj(x2d, g_mix, w_packed, tm=512):
    n = x2d.shape[0]
    return pl.pallas_call(
        _inproj_kernel,
        name="in_proj",
        grid=(n // tm, D_IN_PACKED // IN_TILE_N),
        in_specs=[
            pl.BlockSpec((tm, D_MODEL), lambda i, j: (i, 0)),
            pl.BlockSpec((1, D_MODEL), lambda i, j: (0, 0)),
            pl.BlockSpec((D_MODEL, IN_TILE_N), lambda i, j: (0, j)),
        ],
        out_specs=pl.BlockSpec((tm, IN_TILE_N), lambda i, j: (i, j)),
        out_shape=jax.ShapeDtypeStruct((n, D_IN_PACKED), F32),
        scratch_shapes=[pltpu.VMEM((tm, D_MODEL), BF16)],
        compiler_params=_cparams(("parallel", "arbitrary"), 48),
    )(x2d, g_mix.reshape(1, D_MODEL), w_packed)


GATE_ROWS = 16


def _gate_layout(u):
    pos = np.arange(MISC_AL)
    k, r = pos // GATE_ROWS, pos % GATE_ROWS
    br, g = r // NSA_GROUP, r % NSA_GROUP
    used = r < 3 * NSA_GROUP
    src = np.where(used, (k * NSA_GROUP + g) * 3 + br, 0)
    return jnp.where(jnp.asarray(used), jnp.take(u, jnp.asarray(src), axis=-1), 0.0)


def _pack_w_in(w_in):
    sizes = (NSA_HEADS * NSA_DK,) + (NSA_KV_HEADS * NSA_DK,) * 6 + (3 * NSA_HEADS,) + (
        GLA_HEADS * GLA_DK, GLA_HEADS * GLA_DK, GLA_HEADS * GLA_DV, GLA_GATE_RANK, GLA_HEADS * GLA_DV)
    offs = np.concatenate([[0], np.cumsum(sizes)])
    seg = lambda i: w_in[:, offs[i]:offs[i + 1]]
    zeros = lambda n: jnp.zeros((D_MODEL, n), w_in.dtype)
    misc = jnp.concatenate([_gate_layout(seg(7)), seg(11),
                            zeros(LANES - MISC_AL - GLA_GATE_RANK)], axis=1)
    cols = [seg(0)] + [seg(i) for i in range(1, 7)] + [seg(8), seg(9), seg(10), seg(12), misc,
                                                      zeros(D_IN_PACKED - COL_MISC - LANES)]
    return jnp.concatenate(cols, axis=1).astype(BF16)


def _memkv_kernel(mem_ref, gm_ref, wk_ref, wv_ref, gk_ref, ck_ref, cv_ref):
    hm = _rms(mem_ref[...], gm_ref[...]).astype(BF16)
    k = jnp.dot(hm, wk_ref[...], preferred_element_type=F32)
    v = jnp.dot(hm, wv_ref[...], preferred_element_type=F32)
    for h in range(MEM_HEADS):
        sl = slice(h * MEM_DH, (h + 1) * MEM_DH)
        ck_ref[h] = _rms(k[:, sl], gk_ref[...]).astype(BF16)
        cv_ref[h] = v[:, sl].astype(BF16)


def _mem_kv(mem, g_mem, w_ck, w_cv, g_ck):
    b, m, _ = mem.shape
    dm = MEM_HEADS * MEM_DH
    out = jax.ShapeDtypeStruct((b, MEM_HEADS, m, MEM_DH), BF16)
    return pl.pallas_call(
        _memkv_kernel,
        name="mem_kv",
        grid=(b,),
        in_specs=[
            pl.BlockSpec((None, m, D_MODEL), lambda i: (i, 0, 0)),
            pl.BlockSpec((1, D_MODEL), lambda i: (0, 0)),
            pl.BlockSpec((D_MODEL, dm), lambda i: (0, 0)),
            pl.BlockSpec((D_MODEL, dm), lambda i: (0, 0)),
            pl.BlockSpec((1, MEM_DH), lambda i: (0, 0)),
        ],
        out_specs=[pl.BlockSpec((None, MEM_HEADS, m, MEM_DH), lambda i: (i, 0, 0, 0))] * 2,
        out_shape=[out, out],
        compiler_params=_cparams(("parallel",), 32),
    )(mem, g_mem.reshape(1, D_MODEL), w_ck.astype(BF16), w_cv.astype(BF16), g_ck.reshape(1, MEM_DH))


def _outx_kernel(x_ref, nsa_ref, gla_ref, wo1_ref, wo2_ref, gc_ref, wcq_ref, gcq_ref,
                 ck_ref, cv_ref, wco_ref, o_ref):
    x1 = (x_ref[...]
          + jnp.dot(nsa_ref[...], wo1_ref[...], preferred_element_type=F32)
          + jnp.dot(gla_ref[...], wo2_ref[...], preferred_element_type=F32))
    hq = _rms(x1, gc_ref[...]).astype(BF16)
    cq = jnp.dot(hq, wcq_ref[...], preferred_element_type=F32)
    outs = []
    for h in range(MEM_HEADS):
        c = _rms(cq[:, h * MEM_DH:(h + 1) * MEM_DH], gcq_ref[...]) * (MEM_DH ** -0.5)
        s = lax.dot_general(c.astype(BF16), ck_ref[h], (((1,), (1,)), ((), ())),
                            preferred_element_type=F32)
        p = jnp.exp(s - jnp.max(s, axis=-1, keepdims=True))
        p = p / jnp.sum(p, axis=-1, keepdims=True)
        outs.append(jnp.dot(p.astype(BF16), cv_ref[h], preferred_element_type=F32))
    oc = jnp.concatenate(outs, axis=-1).astype(BF16)
    o_ref[...] = x1 + jnp.dot(oc, wco_ref[...], preferred_element_type=F32)


def _out_cross(x2d, o_nsa, o_gla, w_out, g_cross, w_cq, g_cq, ck, cv, w_co, seq, tm=256):
    n = x2d.shape[0]
    tiles_per_batch = seq // tm
    dm = MEM_HEADS * MEM_DH
    m = ck.shape[2]
    full = lambda shape: pl.BlockSpec(shape, lambda i: (0,) * len(shape))
    kv_spec = pl.BlockSpec((None, MEM_HEADS, m, MEM_DH), lambda i: (i // tiles_per_batch, 0, 0, 0))
    w_out = w_out.astype(BF16)
    return pl.pallas_call(
        _outx_kernel,
        name="out_cross",
        grid=(n // tm,),
        in_specs=[
            pl.BlockSpec((tm, D_MODEL), lambda i: (i, 0)),
            pl.BlockSpec((tm, D_NSA_OUT), lambda i: (i, 0)),
            pl.BlockSpec((tm, D_GLA_OUT), lambda i: (i, 0)),
            full((D_NSA_OUT, D_MODEL)),
            full((D_GLA_OUT, D_MODEL)),
            full((1, D_MODEL)),
            full((D_MODEL, dm)),
            full((1, MEM_DH)),
            kv_spec,
            kv_spec,
            full((dm, D_MODEL)),
        ],
        out_specs=pl.BlockSpec((tm, D_MODEL), lambda i: (i, 0)),
        out_shape=jax.ShapeDtypeStruct((n, D_MODEL), F32),
        compiler_params=_cparams(("parallel",), 48),
    )(x2d, o_nsa, o_gla, w_out[:D_NSA_OUT], w_out[D_NSA_OUT:], g_cross.reshape(1, D_MODEL),
      w_cq.astype(BF16), g_cq.reshape(1, MEM_DH), ck, cv, w_co.astype(BF16))


def _ffn_kernel(x_ref, g_ref, wg_ref, wu_ref, wd_ref, o_ref, h_ref):
    @pl.when(pl.program_id(1) == 0)
    def _():
        x = x_ref[...]
        h_ref[...] = _rms(x, g_ref[...]).astype(BF16)
        o_ref[...] = x

    h = h_ref[...]
    g = jnp.dot(h, wg_ref[...], preferred_element_type=F32)
    u = jnp.dot(h, wu_ref[...], preferred_element_type=F32)
    a = (g * jax.nn.sigmoid(g) * u).astype(BF16)
    o_ref[...] += jnp.dot(a, wd_ref[...], preferred_element_type=F32)


def _ffn(x2d, g_ffn, w_gu, w_down, tm=512, tf=512):
    n = x2d.shape[0]
    w_gu = w_gu.astype(BF16)
    return pl.pallas_call(
        _ffn_kernel,
        name="ffn",
        grid=(n // tm, D_FF // tf),
        in_specs=[
            pl.BlockSpec((tm, D_MODEL), lambda i, j: (i, 0)),
            pl.BlockSpec((1, D_MODEL), lambda i, j: (0, 0)),
            pl.BlockSpec((D_MODEL, tf), lambda i, j: (0, j)),
            pl.BlockSpec((D_MODEL, tf), lambda i, j: (0, j)),
            pl.BlockSpec((tf, D_MODEL), lambda i, j: (j, 0)),
        ],
        out_specs=pl.BlockSpec((tm, D_MODEL), lambda i, j: (i, 0)),
        out_shape=jax.ShapeDtypeStruct((n, D_MODEL), F32),
        scratch_shapes=[pltpu.VMEM((tm, D_MODEL), BF16)],
        compiler_params=_cparams(("parallel", "arbitrary"), 48),
    )(x2d, g_ffn.reshape(1, D_MODEL), w_gu[:, :D_FF], w_gu[:, D_FF:], w_down.astype(BF16))


def _split3(x):
    hi = x.astype(BF16)
    r = x - hi.astype(F32)
    mid = r.astype(BF16)
    lo = (r - mid.astype(F32)).astype(BF16)
    return hi, mid, lo


def _dot_exact_lhs(a_bf16, x):
    return sum(jnp.dot(a_bf16, p, preferred_element_type=F32) for p in _split3(x))


def _dot_split(a, w):
    a_hi, a_lo, _ = _split3(a)
    w_hi, w_lo, _ = _split3(w)
    d = lambda p, q: jnp.dot(p, q, preferred_element_type=F32)
    return d(a_hi, w_hi) + d(a_hi, w_lo) + d(a_lo, w_hi)


_NT = (((1,), (1,)), ((), ()))
_TN = (((0,), (0,)), ((), ()))
GLA_DIRECT = 8


def _gla_kernel(ql_ref, kl_ref, vl_ref, rl_ref, misc_ref, w2_ref, bg_ref, go_ref, band_ref,
                o_ref, st_ref):
    c_len = GLA_CHUNK
    tc = ql_ref.shape[0]
    n_chunks = tc // c_len

    @pl.when(pl.program_id(2) == 0)
    def _():
        st_ref[...] = jnp.zeros_like(st_ref)

    x = _dot_split(misc_ref[...], w2_ref[...]) + bg_ref[...]
    la = (jnp.minimum(x, 0.0) - jnp.log(1.0 + jnp.exp(-jnp.abs(x)))) * (1.0 / GLA_GATE_NORM)
    pos = lax.broadcasted_iota(jnp.int32, (tc, 1), 0) & (c_len - 1)
    b = la
    sh = 1
    while sh < c_len:
        b = b + jnp.where(pos >= sh, pltpu.roll(b, sh, 0), 0.0)
        sh *= 2
    q = ql_ref[...] * (GLA_DK ** -0.5)
    k = kl_ref[...]
    b3 = b.reshape(n_chunks, c_len, GLA_DK)

    def chunk_row(r, n):
        return jnp.broadcast_to(b3[:, r:r + 1, :], (n_chunks, n, GLA_DK))

    level_q, level_k, level_s = [], [], []
    s = c_len
    while s > GLA_DIRECT:
        half = s // 2
        ref = jnp.concatenate([chunk_row(m0 + half - 1, s) for m0 in range(0, c_len, s)],
                              axis=1).reshape(tc, GLA_DK)
        second = (pos & (s - 1)) >= half
        level_q.append(jnp.where(second, q * jnp.exp(jnp.minimum(b - ref, 0.0)), 0.0).astype(BF16))
        level_k.append(jnp.where(second, 0.0, k * jnp.exp(jnp.minimum(ref - b, 0.0))).astype(BF16))
        level_s.append(s)
        s = half
    group = lambda u: u.reshape(tc // GLA_DIRECT, GLA_DIRECT, GLA_DK)
    band = jnp.zeros((tc, c_len), F32)
    for d in range(GLA_DIRECT):
        k_sh = k if d == 0 else pltpu.roll(group(k), d, 1).reshape(tc, GLA_DK)
        b_sh = b if d == 0 else pltpu.roll(group(b), d, 1).reshape(tc, GLA_DK)
        cd = jnp.sum(q * k_sh * jnp.exp(jnp.minimum(b - b_sh, 0.0)), axis=-1, keepdims=True)
        band = band + cd * band_ref[d]
    b_end = chunk_row(c_len - 1, c_len).reshape(tc, GLA_DK)
    q_dec = (q * jnp.exp(b)).astype(BF16)
    k_dec = (k * jnp.exp(b_end - b)).astype(BF16)
    s_dec = jnp.exp(b_end)
    v = vl_ref[...].astype(BF16)
    row = lax.broadcasted_iota(jnp.int32, (c_len, 1), 0)
    col = lax.broadcasted_iota(jnp.int32, (1, c_len), 1)

    st = st_ref[...]
    for c in range(n_chunks):
        sl = slice(c * c_len, (c + 1) * c_len)
        attn = band[sl]
        for qs, ks, s in zip(level_q, level_k, level_s):
            a_s = lax.dot_general(qs[sl], ks[sl], _NT, preferred_element_type=F32)
            attn = attn + (a_s if s == c_len else jnp.where((row // s) == (col // s), a_s, 0.0))
        o = (jnp.dot(attn.astype(BF16), v[sl], preferred_element_type=F32)
             + lax.dot_general(q_dec[sl], st.astype(BF16), _NT, preferred_element_type=F32))
        st = (st * s_dec[c * c_len:c * c_len + 1]
              + lax.dot_general(v[sl], k_dec[sl], _TN, preferred_element_type=F32))
        r = rl_ref[sl, :]
        o_ref[sl, :] = (_rms(o, go_ref[...]) * (r * jax.nn.sigmoid(r))).astype(o_ref.dtype)
    st_ref[...] = st


def _gla(proj, w_gk2, b_gk, g_gla_out, batch, seq, tc=512):
    n = proj.shape[0]
    nt = seq // tc
    w2 = jnp.zeros((LANES, GLA_HEADS * GLA_DK), F32).at[MISC_AL:MISC_AL + GLA_GATE_RANK].set(w_gk2)
    i_pos = np.arange(tc)[:, None] % GLA_CHUNK
    d_off = np.arange(GLA_DIRECT)[:, None, None]
    band = ((np.arange(GLA_CHUNK)[None, :] == i_pos - d_off) & (i_pos % GLA_DIRECT >= d_off))
    band = jnp.asarray(band, F32)
    rows = lambda b, h, i: b * nt + i
    return pl.pallas_call(
        _gla_kernel,
        name="gla",
        grid=(batch, GLA_HEADS, nt),
        in_specs=[
            pl.BlockSpec((tc, GLA_DK), lambda b, h, i: (rows(b, h, i), COL_QL // GLA_DK + h)),
            pl.BlockSpec((tc, GLA_DK), lambda b, h, i: (rows(b, h, i), COL_KL // GLA_DK + h)),
            pl.BlockSpec((tc, GLA_DV), lambda b, h, i: (rows(b, h, i), COL_VL // GLA_DV + h)),
            pl.BlockSpec((tc, GLA_DV), lambda b, h, i: (rows(b, h, i), COL_RL // GLA_DV + h)),
            pl.BlockSpec((tc, LANES), lambda b, h, i: (rows(b, h, i), COL_MISC // LANES)),
            pl.BlockSpec((LANES, GLA_DK), lambda b, h, i: (0, h)),
            pl.BlockSpec((1, GLA_DK), lambda b, h, i: (0, h)),
            pl.BlockSpec((1, GLA_DV), lambda b, h, i: (0, 0)),
            pl.BlockSpec((GLA_DIRECT, tc, GLA_CHUNK), lambda b, h, i: (0, 0, 0)),
        ],
        out_specs=pl.BlockSpec((tc, GLA_DV), lambda b, h, i: (rows(b, h, i), h)),
        out_shape=jax.ShapeDtypeStruct((n, D_GLA_OUT), BF16),
        scratch_shapes=[pltpu.VMEM((GLA_DV, GLA_DK), F32)],
        compiler_params=_cparams(("parallel", "parallel", "arbitrary"), 32),
    )(proj, proj, proj, proj, proj, w2, b_gk.reshape(1, -1), g_gla_out.reshape(1, GLA_DV), band)


NSA_TQ = 128
NSA_NQ = NSA_TQ * NSA_GROUP
NEG = -1e30
SLC_CHUNK = 512
WIN_KEYS = WINDOW + NSA_TQ
CMP_CHUNK = 128
FEAT = 16


def _nsa_prep_kernel(q_ref, cmp_ref, slc_ref, win_ref, misc_ref, gq_ref, gks_ref, gkw_ref, bg_ref,
                     qt_ref, kcm_ref, vcm_ref, ks_ref, vst_ref, kw_ref, vwt_ref, gt_ref):
    q = q_ref[...]
    for k in range(NSA_KV_HEADS):
        for g in range(NSA_GROUP):
            h = k * NSA_GROUP + g
            qn = _rms(q[:, h * NSA_DK:(h + 1) * NSA_DK], gq_ref[...]) * (NSA_DK ** -0.5)
            qt_ref[k, :, g * NSA_TQ:(g + 1) * NSA_TQ] = qn.T.astype(BF16)
        ksl = slice(k * NSA_DK, (k + 1) * NSA_DK)
        vsl = slice((NSA_KV_HEADS + k) * NSA_DK, (NSA_KV_HEADS + k + 1) * NSA_DK)
        kcm_ref[k] = cmp_ref[:, ksl]
        vcm_ref[k] = cmp_ref[:, vsl]
        ks_ref[k] = _rms(slc_ref[:, ksl], gks_ref[...]).astype(BF16)
        vst_ref[k] = slc_ref[:, vsl].T.astype(BF16)
        kw_ref[k] = _rms(win_ref[:, ksl], gkw_ref[...]).astype(BF16)
        vwt_ref[k] = win_ref[:, vsl].T.astype(BF16)
    gates = jax.nn.sigmoid(misc_ref[...] + bg_ref[...])
    gt_ref[...] = gates.T[:NSA_KV_HEADS * GATE_ROWS, :]


def _nsa_prep(proj, g_q, g_ks, g_kw, b_gate, batch, seq):
    nqt = seq // NSA_TQ
    hk = NSA_KV_HEADS
    pair = 2 * hk * NSA_DK
    rows = lambda b, i: b * nqt + i
    vec = lambda n: pl.BlockSpec((1, n), lambda b, i: (0, 0))
    bias = jnp.zeros((1, LANES), F32).at[0, :MISC_AL].set(_gate_layout(b_gate))
    tok = lambda dt: jax.ShapeDtypeStruct((batch, hk, seq, NSA_DK), dt)
    tr = jax.ShapeDtypeStruct((batch, hk, nqt, NSA_DV, NSA_TQ), BF16)
    tok_spec = pl.BlockSpec((None, hk, NSA_TQ, NSA_DK), lambda b, i: (b, 0, i, 0))
    tr_spec = pl.BlockSpec((None, hk, None, NSA_DV, NSA_TQ), lambda b, i: (b, 0, i, 0, 0))
    return pl.pallas_call(
        _nsa_prep_kernel,
        name="nsa_prep",
        grid=(batch, nqt),
        in_specs=[
            pl.BlockSpec((NSA_TQ, NSA_HEADS * NSA_DK), lambda b, i: (rows(b, i), 0)),
            pl.BlockSpec((NSA_TQ, pair), lambda b, i: (rows(b, i), COL_KV // pair)),
            pl.BlockSpec((NSA_TQ, pair), lambda b, i: (rows(b, i), COL_KV // pair + 1)),
            pl.BlockSpec((NSA_TQ, pair), lambda b, i: (rows(b, i), COL_KV // pair + 2)),
            pl.BlockSpec((NSA_TQ, LANES), lambda b, i: (rows(b, i), COL_MISC // LANES)),
            vec(NSA_DK), vec(NSA_DK), vec(NSA_DK), vec(LANES),
        ],
        out_specs=[
            pl.BlockSpec((None, hk, None, NSA_DK, NSA_NQ), lambda b, i: (b, 0, i, 0, 0)),
            tok_spec, tok_spec, tok_spec, tr_spec, tok_spec, tr_spec,
            pl.BlockSpec((None, hk * GATE_ROWS, NSA_TQ), lambda b, i: (b, 0, i)),
        ],
        out_shape=[
            jax.ShapeDtypeStruct((batch, hk, nqt, NSA_DK, NSA_NQ), BF16),
            tok(F32), tok(F32), tok(BF16), tr, tok(BF16), tr,
            jax.ShapeDtypeStruct((batch, hk * GATE_ROWS, seq), F32),
        ],
        compiler_params=_cparams(("parallel", "parallel"), 32),
    )(proj, proj, proj, proj, proj, g_q.reshape(1, -1), g_ks.reshape(1, -1), g_kw.reshape(1, -1), bias)


def _compress_kernel(uk_ref, uv_ref, pek_ref, pev_ref, wk1_ref, wk2_ref, wv1_ref, wv2t_ref, gk_ref,
                     kc_ref, vct_ref):
    def hidden(u_ref, pe_ref, w1_ref):
        u = u_ref[...]
        n = u.shape[0]
        lo = jnp.dot((u + pe_ref[0:1, :]).astype(BF16), w1_ref[0], preferred_element_type=F32)
        hi = jnp.dot((u + pe_ref[1:2, :]).astype(BF16), w1_ref[1], preferred_element_type=F32)
        hid = lo + pltpu.roll(hi, n - 1, 0)
        return (hid * jax.nn.sigmoid(hid)).astype(BF16)

    kc = jnp.dot(hidden(uk_ref, pek_ref, wk1_ref), wk2_ref[...], preferred_element_type=F32)
    kc_ref[...] = _rms(kc, gk_ref[...]).astype(BF16)
    vct = lax.dot_general(wv2t_ref[...], hidden(uv_ref, pev_ref, wv1_ref), _NT,
                          preferred_element_type=F32).astype(BF16)
    for c in range(vct_ref.shape[0]):
        vct_ref[c] = vct[:, c * CMP_CHUNK:(c + 1) * CMP_CHUNK]


def _compress(k_tok, v_tok, pe_k, pe_v, w_ck1, w_ck2, w_cv1, w_cv2, g_kc):
    batch, hk, seq, d = k_tok.shape
    nc = seq // CMP_STRIDE
    flat = CMP_STRIDE * d
    halves = CMP_LEN // CMP_STRIDE
    uk = k_tok.reshape(batch, hk, nc, flat)
    uv = v_tok.reshape(batch, hk, nc, flat)
    u_spec = pl.BlockSpec((None, None, nc, flat), lambda b, k: (b, k, 0, 0))
    full = lambda shape: pl.BlockSpec(shape, lambda b, k: (0,) * len(shape))
    return pl.pallas_call(
        _compress_kernel,
        name="nsa_compress",
        grid=(batch, hk),
        in_specs=[u_spec, u_spec, full((halves, flat)), full((halves, flat)),
                  full((halves, flat, CMP_HIDDEN)), full((CMP_HIDDEN, d)),
                  full((halves, flat, CMP_HIDDEN)), full((d, CMP_HIDDEN)), full((1, d))],
        out_specs=[pl.BlockSpec((None, None, nc, d), lambda b, k: (b, k, 0, 0)),
                   pl.BlockSpec((None, None, nc // CMP_CHUNK, d, CMP_CHUNK),
                                lambda b, k: (b, k, 0, 0, 0))],
        out_shape=[jax.ShapeDtypeStruct((batch, hk, nc, d), BF16),
                   jax.ShapeDtypeStruct((batch, hk, nc // CMP_CHUNK, d, CMP_CHUNK), BF16)],
        compiler_params=_cparams(("parallel", "parallel"), 48),
    )(uk, uv, pe_k.reshape(halves, flat), pe_v.reshape(halves, flat),
      w_ck1.astype(BF16).reshape(halves, flat, CMP_HIDDEN), w_ck2.astype(BF16),
      w_cv1.astype(BF16).reshape(halves, flat, CMP_HIDDEN), w_cv2.T.astype(BF16), g_kc.reshape(1, d))


def _query_rows(qt):
    lane = lax.broadcasted_iota(jnp.int32, (1, NSA_NQ), 1)
    t = qt * NSA_TQ + (lane & (NSA_TQ - 1))
    return t


def _topk_mask(imp, cur, n_top):
    ns, nt = imp.shape
    j = lax.broadcasted_iota(jnp.int32, (ns, nt), 0)
    forced = (j == 0) | (j == cur) | (j == cur - 1)
    visible = j <= cur
    work = jnp.where(visible & jnp.logical_not(forced), imp, -1.0)

    def extract(_, work):
        m = jnp.max(work, axis=0, keepdims=True)
        cand = (work == m) & (m > -0.5)
        idx = jnp.min(jnp.where(cand, j, ns), axis=0, keepdims=True)
        return jnp.where(j == idx, -2.0, work)

    work = lax.fori_loop(0, n_top - 3, extract, work)
    return (forced & visible) | (work == -2.0)


def _feature_rows(rows):
    n = rows[0].shape[1]
    r = lax.broadcasted_iota(jnp.int32, (FEAT // 2, n), 0)
    out = jnp.zeros((FEAT // 2, n), F32)
    for i, row in enumerate(rows):
        out = jnp.where(r == i, row, out)
    return out


def _aug_rhs(q, mask8, bias8):
    return jnp.concatenate([q, jnp.concatenate([mask8, bias8], axis=0).astype(BF16)], axis=0)


def _key_features(n, kind):
    a = np.arange(n)
    f = np.zeros((n, FEAT), np.float32)
    if kind == "cmp":
        f[:, 8], f[:, 9], f[:, 10] = a, 1.0, 1.0
    else:
        f[:, 8], f[:, 9], f[:, 10] = a % SEL_BLOCK, a // SEL_BLOCK, 1.0
        if kind == "slc":
            f[a, a // SEL_BLOCK] = 1.0
    return jnp.asarray(f, BF16)


def _nsa_sel_kernel(qt_ref, kc_ref, kf_ref, vct_ref, mmat_ref, slope_ref, negsel_ref, flag_ref,
                    oc_ref, s_ref, ps_ref, *, n_top):
    qt = pl.program_id(2)
    t0 = qt * NSA_TQ
    t = _query_rows(qt)
    slope = slope_ref[...]
    q = qt_ref[...]
    zeros8 = jnp.zeros((FEAT // 2, NSA_NQ), F32)
    per_tile = NSA_TQ // CMP_STRIDE
    n_chunks = (per_tile * qt + per_tile + CMP_CHUNK - 1) // CMP_CHUNK
    first_edge = jnp.maximum(per_tile * qt - 1, 0) // CMP_CHUNK
    ci0 = lax.broadcasted_iota(jnp.int32, (CMP_CHUNK, 1), 0)

    def scores(c, m):
        r0 = pl.multiple_of(c * CMP_CHUNK, CMP_CHUNK)
        off = (r0 * CMP_STRIDE - t0).astype(F32)
        bias8 = _feature_rows([CMP_STRIDE * slope, off * slope, (0.5 * (CMP_LEN - 1)) * slope])
        lhs = jnp.concatenate([kc_ref[pl.ds(r0, CMP_CHUNK), :], kf_ref[...]], axis=1)
        s = jnp.dot(lhs, _aug_rhs(q, zeros8, bias8), preferred_element_type=F32)

        def masked():
            vis = t >= (r0 + ci0) * CMP_STRIDE + (CMP_LEN - 1)
            return jnp.where(vis, s, NEG)

        s = lax.cond(c >= first_edge, masked, lambda: s)
        s_ref[pl.ds(r0, CMP_CHUNK), :] = s
        return jnp.maximum(m, jnp.max(s, axis=0, keepdims=True))

    m = lax.fori_loop(0, n_chunks, scores, jnp.full((1, NSA_NQ), NEG, F32))
    m = jnp.maximum(m, 0.1 * NEG)

    oc_ref[...] = jnp.zeros_like(oc_ref)

    def probs(c, l):
        rows = pl.ds(pl.multiple_of(c * CMP_CHUNK, CMP_CHUNK), CMP_CHUNK)
        p = jnp.exp(s_ref[rows, :] - m)
        s_ref[rows, :] = p
        oc_ref[...] += jnp.dot(vct_ref[c], p.astype(BF16), preferred_element_type=F32)
        return l + jnp.sum(p, axis=0, keepdims=True)

    l = lax.fori_loop(0, n_chunks, probs, jnp.zeros((1, NSA_NQ), F32))
    inv = 1.0 / jnp.where(l > 0.0, l, 1.0)
    oc_ref[...] = oc_ref[...] * inv

    ps_ref[...] = jnp.zeros_like(ps_ref)

    def group_sum(c, carry):
        rows = pl.ds(pl.multiple_of(c * CMP_CHUNK, CMP_CHUNK), CMP_CHUNK)
        p = s_ref[rows, :] * inv
        ps_ref[rows, :] = sum(p[:, g * NSA_TQ:(g + 1) * NSA_TQ] for g in range(NSA_GROUP))
        return carry

    lax.fori_loop(0, n_chunks, group_sum, 0)
    imp = _dot_exact_lhs(mmat_ref[...], ps_ref[...])
    cur = lax.shift_right_logical(t[:, :NSA_TQ], 6)
    sel = _topk_mask(imp, cur, n_top)
    j = lax.broadcasted_iota(jnp.int32, sel.shape, 0)
    past = sel & (j < 2 * qt)
    negsel_ref[...] = jnp.where(past, 0.0, NEG)
    flag_ref[...] = jnp.max(past.astype(F32).T, axis=0, keepdims=True).astype(jnp.int32)


def _nsa_select(qt_arr, kc, vct, batch, seq):
    hk = NSA_KV_HEADS
    nqt = seq // NSA_TQ
    nc = seq // CMP_STRIDE
    ns = seq // SEL_BLOCK
    n_cmp = nc - (CMP_LEN // CMP_STRIDE - 1)
    n_top = min(SEL_TOP, ns)
    ratio, lead = SEL_BLOCK // CMP_STRIDE, CMP_LEN // CMP_STRIDE - 1
    mm = np.zeros((ns, nc), np.float32)
    for r in range(ratio + lead):
        st = CMP_STRIDE * (r - lead)
        ov = (min(st + CMP_LEN, SEL_BLOCK) - max(st, 0)) / CMP_STRIDE
        for jb in range(ns):
            i = jb * ratio + r - lead
            if 0 <= i < n_cmp:
                mm[jb, i] += ov
    slopes = jnp.exp2(-8.0 * jnp.arange(1, NSA_HEADS + 1, dtype=F32) / NSA_HEADS)
    slopes = jnp.repeat(slopes.reshape(hk, 1, NSA_GROUP), NSA_TQ, axis=2)
    negsel, flags, oc = pl.pallas_call(
        functools.partial(_nsa_sel_kernel, n_top=n_top),
        name="nsa_select",
        grid=(batch, hk, nqt),
        in_specs=[
            pl.BlockSpec((None, None, None, NSA_DK, NSA_NQ), lambda b, k, i: (b, k, i, 0, 0)),
            pl.BlockSpec((None, None, nc, NSA_DK), lambda b, k, i: (b, k, 0, 0)),
            pl.BlockSpec((CMP_CHUNK, FEAT), lambda b, k, i: (0, 0)),
            pl.BlockSpec((None, None, nc // CMP_CHUNK, NSA_DV, CMP_CHUNK), lambda b, k, i: (b, k, 0, 0, 0)),
            pl.BlockSpec((ns, nc), lambda b, k, i: (0, 0)),
            pl.BlockSpec((None, 1, NSA_NQ), lambda b, k, i: (k, 0, 0)),
        ],
        out_specs=[
            pl.BlockSpec((None, None, None, ns, NSA_TQ), lambda b, k, i: (b, k, i, 0, 0)),
            pl.BlockSpec((None, None, None, 1, ns), lambda b, k, i: (b, k, i, 0, 0)),
            pl.BlockSpec((None, None, None, NSA_DV, NSA_NQ), lambda b, k, i: (b, k, i, 0, 0)),
        ],
        out_shape=[
            jax.ShapeDtypeStruct((batch, hk, nqt, ns, NSA_TQ), F32),
            jax.ShapeDtypeStruct((batch, hk, nqt, 1, ns), jnp.int32),
            jax.ShapeDtypeStruct((batch, hk, nqt, NSA_DV, NSA_NQ), F32),
        ],
        scratch_shapes=[pltpu.VMEM((nc, NSA_NQ), F32), pltpu.VMEM((nc, NSA_TQ), F32)],
        compiler_params=_cparams(("parallel", "parallel", "parallel"), 48),
    )(qt_arr, kc, _key_features(CMP_CHUNK, "cmp"), vct, jnp.asarray(mm, BF16), slopes)
    return negsel, flags, oc, slopes


def _nsa_attn_kernel(flag_ref, qt_ref, ks_ref, vst_ref, kw_ref, vwt_ref, negsel_ref, oc_ref, gt_ref,
                     slope_ref, go_ref, kfs_ref, kfw_ref, o_ref, m_ref, l_ref, acc_ref, ow_ref):
    qt = pl.program_id(2)
    t0 = qt * NSA_TQ
    t = _query_rows(qt)
    slope = slope_ref[...]
    q = qt_ref[...]
    zeros8 = jnp.zeros((FEAT // 2, NSA_NQ), F32)
    a = lax.broadcasted_iota(jnp.int32, (NSA_TQ, 1), 0)
    u = t - t0
    causal = jnp.where(a <= u, 0.0, NEG)
    dot = functools.partial(jnp.dot, preferred_element_type=F32)

    lhs = jnp.concatenate([ks_ref[pl.ds(pl.multiple_of(t0, NSA_TQ), NSA_TQ), :], kfw_ref[:NSA_TQ, :]], axis=1)
    s = dot(lhs, _aug_rhs(q, zeros8, _feature_rows([slope, SEL_BLOCK * slope]))) + causal
    m = jnp.max(s, axis=0, keepdims=True)
    p = jnp.exp(s - m)
    m_ref[...] = m
    l_ref[...] = jnp.sum(p, axis=0, keepdims=True)
    acc_ref[...] = dot(vst_ref[qt], p.astype(BF16))

    blocks = SLC_CHUNK // SEL_BLOCK
    sub = SLC_CHUNK // NSA_TQ

    def chunk(c, carry):
        active = flag_ref[0, c * blocks]
        for r in range(1, blocks):
            active = active | flag_ref[0, c * blocks + r]

        @pl.when(active > 0)
        def _():
            k0 = pl.multiple_of(c * SLC_CHUNK, SLC_CHUNK)
            off = (k0 - t0).astype(F32)
            neg8 = negsel_ref[pl.ds(pl.multiple_of(c * blocks, blocks), blocks), :]
            rhs = _aug_rhs(q, jnp.concatenate([neg8] * NSA_GROUP, axis=1),
                           _feature_rows([slope, SEL_BLOCK * slope, off * slope]))
            lhs = jnp.concatenate([ks_ref[pl.ds(k0, SLC_CHUNK), :], kfs_ref[...]], axis=1)
            s = dot(lhs, rhs)
            m_old = m_ref[...]
            m_new = jnp.maximum(m_old, jnp.max(s, axis=0, keepdims=True))
            p = jnp.exp(s - m_new)
            alpha = jnp.exp(m_old - m_new)
            l_ref[...] = alpha * l_ref[...] + jnp.sum(p, axis=0, keepdims=True)
            vt = jnp.concatenate([vst_ref[c * sub + i] for i in range(sub)], axis=1)
            acc_ref[...] = alpha * acc_ref[...] + dot(vt, p.astype(BF16))
            m_ref[...] = m_new

        return carry

    lax.fori_loop(0, (t0 + SLC_CHUNK - 1) // SLC_CHUNK, chunk, 0)

    wsub = WIN_KEYS // NSA_TQ
    wfull = WINDOW // NSA_TQ

    def window(k0, wb, off, mask_fn):
        lhs = jnp.concatenate([kw_ref[pl.ds(k0, WIN_KEYS), :], kfw_ref[...]], axis=1)
        s = mask_fn(dot(lhs, _aug_rhs(q, zeros8, _feature_rows([slope, SEL_BLOCK * slope, off * slope]))))
        p = jnp.exp(s - jnp.max(s, axis=0, keepdims=True))
        vt = jnp.concatenate([vwt_ref[wb + i] for i in range(wsub)], axis=1)
        ow_ref[...] = dot(vt, p.astype(BF16)) / jnp.sum(p, axis=0, keepdims=True)

    @pl.when(qt >= wfull)
    def _():
        def edges(s):
            too_old = jnp.where(a > u, 0.0, NEG)
            return jnp.concatenate([s[:NSA_TQ] + too_old, s[NSA_TQ:WINDOW], s[WINDOW:] + causal], axis=0)

        window(pl.multiple_of(t0 - WINDOW, NSA_TQ), qt - wfull, jnp.float32(-WINDOW), edges)

    @pl.when(qt < wfull)
    def _():
        def generic(s):
            dw = t - lax.broadcasted_iota(jnp.int32, (WIN_KEYS, 1), 0)
            return jnp.where((dw >= 0) & (dw < WINDOW), s, NEG)

        window(0, 0, (-t0).astype(F32), generic)

    def gate(br):
        return jnp.concatenate([gt_ref[br * NSA_GROUP + g:br * NSA_GROUP + g + 1, :]
                                for g in range(NSA_GROUP)], axis=1)

    o = gate(0) * oc_ref[...] + gate(1) * (acc_ref[...] / l_ref[...]) + gate(2) * ow_ref[...]
    o = o * lax.rsqrt(jnp.mean(o * o, axis=0, keepdims=True) + EPS) * go_ref[...]
    for g in range(NSA_GROUP):
        sl = slice(g * NSA_TQ, (g + 1) * NSA_TQ)
        o_ref[:, g * NSA_DV:(g + 1) * NSA_DV] = o[:, sl].T.astype(o_ref.dtype)


def _nsa_attend(flags, qt_arr, ks, vst, kw, vwt, negsel, oc, gates_t, slopes, g_nsa_out, batch, seq):
    hk = NSA_KV_HEADS
    nqt = seq // NSA_TQ
    ns = seq // SEL_BLOCK
    per_tile = lambda shape: pl.BlockSpec((None, None, None) + shape, lambda b, k, i: (b, k, i, 0, 0))
    tok = pl.BlockSpec((None, None, seq, NSA_DK), lambda b, k, i: (b, k, 0, 0))
    tr = pl.BlockSpec((None, None, nqt, NSA_DV, NSA_TQ), lambda b, k, i: (b, k, 0, 0, 0))
    return pl.pallas_call(
        _nsa_attn_kernel,
        name="nsa_attend",
        grid=(batch, hk, nqt),
        in_specs=[
            pl.BlockSpec((None, None, None, 1, ns), lambda b, k, i: (b, k, i, 0, 0),
                         memory_space=pltpu.SMEM),
            per_tile((NSA_DK, NSA_NQ)),
            tok, tr, tok, tr,
            per_tile((ns, NSA_TQ)),
            per_tile((NSA_DV, NSA_NQ)),
            pl.BlockSpec((None, GATE_ROWS, NSA_TQ), lambda b, k, i: (b, k, i)),
            pl.BlockSpec((None, 1, NSA_NQ), lambda b, k, i: (k, 0, 0)),
            pl.BlockSpec((NSA_DV, 1), lambda b, k, i: (0, 0)),
            pl.BlockSpec((SLC_CHUNK, FEAT), lambda b, k, i: (0, 0)),
            pl.BlockSpec((WIN_KEYS, FEAT), lambda b, k, i: (0, 0)),
        ],
        out_specs=pl.BlockSpec((NSA_TQ, NSA_GROUP * NSA_DV), lambda b, k, i: (b * nqt + i, k)),
        out_shape=jax.ShapeDtypeStruct((batch * seq, D_NSA_OUT), BF16),
        scratch_shapes=[pltpu.VMEM((1, NSA_NQ), F32), pltpu.VMEM((1, NSA_NQ), F32),
                        pltpu.VMEM((NSA_DV, NSA_NQ), F32), pltpu.VMEM((NSA_DV, NSA_NQ), F32)],
        compiler_params=_cparams(("parallel", "parallel", "arbitrary"), 48),
    )(flags, qt_arr, ks, vst, kw, vwt, negsel, oc, gates_t, slopes, g_nsa_out.reshape(NSA_DV, 1),
      _key_features(SLC_CHUNK, "slc"), _key_features(WIN_KEYS, "win"))


def _nsa(proj, b_gate, g_q, g_kc, g_ks, g_kw, pe_k, pe_v, w_ck1, w_ck2, w_cv1, w_cv2, g_nsa_out,
         batch, seq):
    assert seq % (CMP_CHUNK * CMP_STRIDE) == 0 and seq >= WIN_KEYS
    qt_arr, k_tok, v_tok, ks, vst, kw, vwt, gates_t = _nsa_prep(proj, g_q, g_ks, g_kw, b_gate, batch, seq)
    kc, vct = _compress(k_tok, v_tok, pe_k, pe_v, w_ck1, w_ck2, w_cv1, w_cv2, g_kc)
    negsel, flags, oc, slopes = _nsa_select(qt_arr, kc, vct, batch, seq)
    return _nsa_attend(flags, qt_arr, ks, vst, kw, vwt, negsel, oc, gates_t, slopes, g_nsa_out, batch, seq)


def kernel(x, mem, g_mix, w_in, b_nsa_gate, g_q, g_kc, g_ks, g_kw, pe_k, pe_v, w_ck1, w_ck2,
           w_cv1, w_cv2, g_nsa_out, w_gk2, b_gk, g_gla_out, w_out, g_cross, g_mem, w_cq, w_ck,
           w_cv, g_cq, g_ck, w_co, g_ffn, w_gu, w_down):
    batch, seq, _ = x.shape
    x2d = x.reshape(batch * seq, D_MODEL)
    l = 0
    proj = _in_proj(x2d, g_mix[l], _pack_w_in(w_in[l]))
    o_gla = _gla(proj, w_gk2[l], b_gk[l], g_gla_out[l], batch, seq)
    o_nsa = _nsa(proj, b_nsa_gate[l], g_q[l], g_kc[l], g_ks[l], g_kw[l], pe_k[l], pe_v[l],
                 w_ck1[l], w_ck2[l], w_cv1[l], w_cv2[l], g_nsa_out[l], batch, seq)
    ck, cv = _mem_kv(mem, g_mem[l], w_ck[l], w_cv[l], g_ck[l])
    x2d = _out_cross(x2d, o_nsa, o_gla, w_out[l], g_cross[l], w_cq[l], g_cq[l], ck, cv,
                     w_co[l], seq)
    x2d = _ffn(x2d, g_ffn[l], w_gu[l], w_down[l])
    return x2d.reshape(batch, seq, D_MODEL)
```

```python
import functools

import numpy as np
import jax
import jax.numpy as jnp
from jax import lax
from jax.experimental import pallas as pl
from jax.experimental.pallas import tpu as pltpu

F32 = jnp.float32
BF16 = jnp.bfloat16

D_MODEL = 2048
MEM_LEN = 256
EPS = 1e-6
NSA_HEADS = 8
NSA_KV_HEADS = 2
NSA_GROUP = NSA_HEADS // NSA_KV_HEADS
NSA_DK = 128
NSA_DV = 128
CMP_LEN = 32
CMP_STRIDE = 16
CMP_HIDDEN = 256
SEL_BLOCK = 64
SEL_TOP = 16
WINDOW = 512
GLA_HEADS = 4
GLA_DK = 128
GLA_DV = 256
GLA_GATE_RANK = 16
GLA_GATE_NORM = 16.0
GLA_CHUNK = 64
MEM_HEADS = 4
MEM_DH = 128
D_FF = -(-8 * D_MODEL // (3 * 256)) * 256
D_NSA_OUT = NSA_HEADS * NSA_DV
D_GLA_OUT = GLA_HEADS * GLA_DV

LANES = 128
MXU_DIM = 256
VMEM_BYTES_V7X = 64 * 1024 * 1024

COL_Q = 0
COL_KV = COL_Q + NSA_HEADS * NSA_DK
COL_QL = COL_KV + 6 * NSA_KV_HEADS * NSA_DK
COL_KL = COL_QL + GLA_HEADS * GLA_DK
COL_VL = COL_KL + GLA_HEADS * GLA_DK
COL_RL = COL_VL + GLA_HEADS * GLA_DV
COL_MISC = COL_RL + GLA_HEADS * GLA_DV
MISC_GATE = 0
MISC_AL = 32
D_IN_PACKED = 6144
IN_TILE_N = 2048


def _cparams(semantics, vmem_mb):
    return pltpu.CompilerParams(dimension_semantics=semantics,
                                vmem_limit_bytes=vmem_mb * 1024 * 1024)


def _rms(u, g):
    return u * lax.rsqrt(jnp.mean(u * u, axis=-1, keepdims=True) + EPS) * g


def _inproj_kernel(x_ref, g_ref, w_ref, o_ref, h_ref):
    @pl.when(pl.program_id(1) == 0)
    def _():
        h_ref[...] = _rms(x_ref[...], g_ref[...]).astype(BF16)

    o_ref[...] = jnp.dot(h_ref[...], w_ref[...], preferred_element_type=F32)


def _in_proj(x2d, g_mix, w_packed, tm=512):
    n = x2d.shape[0]
    return pl.pallas_call(
        _inproj_kernel,
        name="in_proj",
        grid=(n // tm, D_IN_PACKED // IN_TILE_N),
        in_specs=[
            pl.BlockSpec((tm, D_MODEL), lambda i, j: (i, 0)),
            pl.BlockSpec((1, D_MODEL), lambda i, j: (0, 0)),
            pl.BlockSpec((D_MODEL, IN_TILE_N), lambda i, j: (0, j)),
        ],
        out_specs=pl.BlockSpec((tm, IN_TILE_N), lambda i, j: (i, j)),
        out_shape=jax.ShapeDtypeStruct((n, D_IN_PACKED), F32),
        scratch_shapes=[pltpu.VMEM((tm, D_MODEL), BF16)],
        compiler_params=_cparams(("parallel", "arbitrary"), 48),
    )(x2d, g_mix.reshape(1, D_MODEL), w_packed)


GATE_ROWS = 16


def _gate_layout(u):
    pos = np.arange(MISC_AL)
    k, r = pos // GATE_ROWS, pos % GATE_ROWS
    br, g = r // NSA_GROUP, r % NSA_GROUP
    used = r < 3 * NSA_GROUP
    src = np.where(used, (k * NSA_GROUP + g) * 3 + br, 0)
    return jnp.where(jnp.asarray(used), jnp.take(u, jnp.asarray(src), axis=-1), 0.0)


def _pack_w_in(w_in):
    sizes = (NSA_HEADS * NSA_DK,) + (NSA_KV_HEADS * NSA_DK,) * 6 + (3 * NSA_HEADS,) + (
        GLA_HEADS * GLA_DK, GLA_HEADS * GLA_DK, GLA_HEADS * GLA_DV, GLA_GATE_RANK, GLA_HEADS * GLA_DV)
    offs = np.concatenate([[0], np.cumsum(sizes)])
    seg = lambda i: w_in[:, offs[i]:offs[i + 1]]
    zeros = lambda n: jnp.zeros((D_MODEL, n), w_in.dtype)
    misc = jnp.concatenate([_gate_layout(seg(7)), seg(11),
                            zeros(LANES - MISC_AL - GLA_GATE_RANK)], axis=1)
    cols = [seg(0)] + [seg(i) for i in range(1, 7)] + [seg(8), seg(9), seg(10), seg(12), misc,
                                                      zeros(D_IN_PACKED - COL_MISC - LANES)]
    return jnp.concatenate(cols, axis=1).astype(BF16)


def _memkv_kernel(mem_ref, gm_ref, wk_ref, wv_ref, gk_ref, ck_ref, cv_ref):
    hm = _rms(mem_ref[...], gm_ref[...]).astype(BF16)
    k = jnp.dot(hm, wk_ref[...], preferred_element_type=F32)
    v = jnp.dot(hm, wv_ref[...], preferred_element_type=F32)
    for h in range(MEM_HEADS):
        sl = slice(h * MEM_DH, (h + 1) * MEM_DH)
        ck_ref[h] = _rms(k[:, sl], gk_ref[...]).astype(BF16)
        cv_ref[h] = v[:, sl].astype(BF16)


def _mem_kv(mem, g_mem, w_ck, w_cv, g_ck):
    b, m, _ = mem.shape
    dm = MEM_HEADS * MEM_DH
    out = jax.ShapeDtypeStruct((b, MEM_HEADS, m, MEM_DH), BF16)
    return pl.pallas_call(
        _memkv_kernel,
        name="mem_kv",
        grid=(b,),
        in_specs=[
            pl.BlockSpec((None, m, D_MODEL), lambda i: (i, 0, 0)),
            pl.BlockSpec((1, D_MODEL), lambda i: (0, 0)),
            pl.BlockSpec((D_MODEL, dm), lambda i: (0, 0)),
            pl.BlockSpec((D_MODEL, dm), lambda i: (0, 0)),
            pl.BlockSpec((1, MEM_DH), lambda i: (0, 0)),
        ],
        out_specs=[pl.BlockSpec((None, MEM_HEADS, m, MEM_DH), lambda i: (i, 0, 0, 0))] * 2,
        out_shape=[out, out],
        compiler_params=_cparams(("parallel",), 32),
    )(mem, g_mem.reshape(1, D_MODEL), w_ck.astype(BF16), w_cv.astype(BF16), g_ck.reshape(1, MEM_DH))


def _outx_kernel(x_ref, nsa_ref, gla_ref, wo1_ref, wo2_ref, gc_ref, wcq_ref, gcq_ref,
                 ck_ref, cv_ref, wco_ref, o_ref):
    x1 = (x_ref[...]
          + jnp.dot(nsa_ref[...], wo1_ref[...], preferred_element_type=F32)
          + jnp.dot(gla_ref[...], wo2_ref[...], preferred_element_type=F32))
    hq = _rms(x1, gc_ref[...]).astype(BF16)
    cq = jnp.dot(hq, wcq_ref[...], preferred_element_type=F32)
    outs = []
    for h in range(MEM_HEADS):
        c = _rms(cq[:, h * MEM_DH:(h + 1) * MEM_DH], gcq_ref[...]) * (MEM_DH ** -0.5)
        s = lax.dot_general(c.astype(BF16), ck_ref[h], (((1,), (1,)), ((), ())),
                            preferred_element_type=F32)
        p = jnp.exp(s - jnp.max(s, axis=-1, keepdims=True))
        p = p / jnp.sum(p, axis=-1, keepdims=True)
        outs.append(jnp.dot(p.astype(BF16), cv_ref[h], preferred_element_type=F32))
    oc = jnp.concatenate(outs, axis=-1).astype(BF16)
    o_ref[...] = x1 + jnp.dot(oc, wco_ref[...], preferred_element_type=F32)


def _out_cross(x2d, o_nsa, o_gla, w_out, g_cross, w_cq, g_cq, ck, cv, w_co, seq, tm=256):
    n = x2d.shape[0]
    tiles_per_batch = seq // tm
    dm = MEM_HEADS * MEM_DH
    m = ck.shape[2]
    full = lambda shape: pl.BlockSpec(shape, lambda i: (0,) * len(shape))
    kv_spec = pl.BlockSpec((None, MEM_HEADS, m, MEM_DH), lambda i: (i // tiles_per_batch, 0, 0, 0))
    w_out = w_out.astype(BF16)
    return pl.pallas_call(
        _outx_kernel,
        name="out_cross",
        grid=(n // tm,),
        in_specs=[
            pl.BlockSpec((tm, D_MODEL), lambda i: (i, 0)),
            pl.BlockSpec((tm, D_NSA_OUT), lambda i: (i, 0)),
            pl.BlockSpec((tm, D_GLA_OUT), lambda i: (i, 0)),
            full((D_NSA_OUT, D_MODEL)),
            full((D_GLA_OUT, D_MODEL)),
            full((1, D_MODEL)),
            full((D_MODEL, dm)),
            full((1, MEM_DH)),
            kv_spec,
            kv_spec,
            full((dm, D_MODEL)),
        ],
        out_specs=pl.BlockSpec((tm, D_MODEL), lambda i: (i, 0)),
        out_shape=jax.ShapeDtypeStruct((n, D_MODEL), F32),
        compiler_params=_cparams(("parallel",), 48),
    )(x2d, o_nsa, o_gla, w_out[:D_NSA_OUT], w_out[D_NSA_OUT:], g_cross.reshape(1, D_MODEL),
      w_cq.astype(BF16), g_cq.reshape(1, MEM_DH), ck, cv, w_co.astype(BF16))


def _ffn_kernel(x_ref, g_ref, wg_ref, wu_ref, wd_ref, o_ref, h_ref):
    @pl.when(pl.program_id(1) == 0)
    def _():
        x = x_ref[...]
        h_ref[...] = _rms(x, g_ref[...]).astype(BF16)
        o_ref[...] = x

    h = h_ref[...]
    g = jnp.dot(h, wg_ref[...], preferred_element_type=F32)
    u = jnp.dot(h, wu_ref[...], preferred_element_type=F32)
    a = (g * jax.nn.sigmoid(g) * u).astype(BF16)
    o_ref[...] += jnp.dot(a, wd_ref[...], preferred_element_type=F32)


def _ffn(x2d, g_ffn, w_gu, w_down, tm=512, tf=512):
    n = x2d.shape[0]
    w_gu = w_gu.astype(BF16)
    return pl.pallas_call(
        _ffn_kernel,
        name="ffn",
        grid=(n // tm, D_FF // tf),
        in_specs=[
            pl.BlockSpec((tm, D_MODEL), lambda i, j: (i, 0)),
            pl.BlockSpec((1, D_MODEL), lambda i, j: (0, 0)),
            pl.BlockSpec((D_MODEL, tf), lambda i, j: (0, j)),
            pl.BlockSpec((D_MODEL, tf), lambda i, j: (0, j)),
            pl.BlockSpec((tf, D_MODEL), lambda i, j: (j, 0)),
        ],
        out_specs=pl.BlockSpec((tm, D_MODEL), lambda i, j: (i, 0)),
        out_shape=jax.ShapeDtypeStruct((n, D_MODEL), F32),
        scratch_shapes=[pltpu.VMEM((tm, D_MODEL), BF16)],
        compiler_params=_cparams(("parallel", "arbitrary"), 48),
    )(x2d, g_ffn.reshape(1, D_MODEL), w_gu[:, :D_FF], w_gu[:, D_FF:], w_down.astype(BF16))


def _split3(x):
    hi = x.astype(BF16)
    r = x - hi.astype(F32)
    mid = r.astype(BF16)
    lo = (r - mid.astype(F32)).astype(BF16)
    return hi, mid, lo


def _dot_exact_lhs(a_bf16, x):
    return sum(jnp.dot(a_bf16, p, preferred_element_type=F32) for p in _split3(x))


def _dot_split(a, w):
    a_hi, a_lo, _ = _split3(a)
    w_hi, w_lo, _ = _split3(w)
    d = lambda p, q: jnp.dot(p, q, preferred_element_type=F32)
    return d(a_hi, w_hi) + d(a_hi, w_lo) + d(a_lo, w_hi)


_NT = (((1,), (1,)), ((), ()))
_TN = (((0,), (0,)), ((), ()))
GLA_DIRECT = 8


def _gla_kernel(ql_ref, kl_ref, vl_ref, rl_ref, misc_ref, w2_ref, bg_ref, go_ref, band_ref,
                o_ref, st_ref):
    c_len = GLA_CHUNK
    tc = ql_ref.shape[0]
    n_chunks = tc // c_len

    @pl.when(pl.program_id(2) == 0)
    def _():
        st_ref[...] = jnp.zeros_like(st_ref)

    x = _dot_split(misc_ref[...], w2_ref[...]) + bg_ref[...]
    la = (jnp.minimum(x, 0.0) - jnp.log(1.0 + jnp.exp(-jnp.abs(x)))) * (1.0 / GLA_GATE_NORM)
    pos = lax.broadcasted_iota(jnp.int32, (tc, 1), 0) & (c_len - 1)
    b = la
    sh = 1
    while sh < c_len:
        b = b + jnp.where(pos >= sh, pltpu.roll(b, sh, 0), 0.0)
        sh *= 2
    q = ql_ref[...] * (GLA_DK ** -0.5)
    k = kl_ref[...]
    b3 = b.reshape(n_chunks, c_len, GLA_DK)

    def chunk_row(r, n):
        return jnp.broadcast_to(b3[:, r:r + 1, :], (n_chunks, n, GLA_DK))

    level_q, level_k, level_s = [], [], []
    s = c_len
    while s > GLA_DIRECT:
        half = s // 2
        ref = jnp.concatenate([chunk_row(m0 + half - 1, s) for m0 in range(0, c_len, s)],
                              axis=1).reshape(tc, GLA_DK)
        second = (pos & (s - 1)) >= half
        level_q.append(jnp.where(second, q * jnp.exp(jnp.minimum(b - ref, 0.0)), 0.0).astype(BF16))
        level_k.append(jnp.where(second, 0.0, k * jnp.exp(jnp.minimum(ref - b, 0.0))).astype(BF16))
        level_s.append(s)
        s = half
    group = lambda u: u.reshape(tc // GLA_DIRECT, GLA_DIRECT, GLA_DK)
    band = jnp.zeros((tc, c_len), F32)
    for d in range(GLA_DIRECT):
        k_sh = k if d == 0 else pltpu.roll(group(k), d, 1).reshape(tc, GLA_DK)
        b_sh = b if d == 0 else pltpu.roll(group(b), d, 1).reshape(tc, GLA_DK)
        cd = jnp.sum(q * k_sh * jnp.exp(jnp.minimum(b - b_sh, 0.0)), axis=-1, keepdims=True)
        band = band + cd * band_ref[d]
    b_end = chunk_row(c_len - 1, c_len).reshape(tc, GLA_DK)
    q_dec = (q * jnp.exp(b)).astype(BF16)
    k_dec = (k * jnp.exp(b_end - b)).astype(BF16)
    s_dec = jnp.exp(b_end)
    v = vl_ref[...].astype(BF16)
    row = lax.broadcasted_iota(jnp.int32, (c_len, 1), 0)
    col = lax.broadcasted_iota(jnp.int32, (1, c_len), 1)

    st = st_ref[...]
    for c in range(n_chunks):
        sl = slice(c * c_len, (c + 1) * c_len)
        attn = band[sl]
        for qs, ks, s in zip(level_q, level_k, level_s):
            a_s = lax.dot_general(qs[sl], ks[sl], _NT, preferred_element_type=F32)
            attn = attn + (a_s if s == c_len else jnp.where((row // s) == (col // s), a_s, 0.0))
        o = (jnp.dot(attn.astype(BF16), v[sl], preferred_element_type=F32)
             + lax.dot_general(q_dec[sl], st.astype(BF16), _NT, preferred_element_type=F32))
        st = (st * s_dec[c * c_len:c * c_len + 1]
              + lax.dot_general(v[sl], k_dec[sl], _TN, preferred_element_type=F32))
        r = rl_ref[sl, :]
        o_ref[sl, :] = (_rms(o, go_ref[...]) * (r * jax.nn.sigmoid(r))).astype(o_ref.dtype)
    st_ref[...] = st


def _gla(proj, w_gk2, b_gk, g_gla_out, batch, seq, tc=512):
    n = proj.shape[0]
    nt = seq // tc
    w2 = jnp.zeros((LANES, GLA_HEADS * GLA_DK), F32).at[MISC_AL:MISC_AL + GLA_GATE_RANK].set(w_gk2)
    i_pos = np.arange(tc)[:, None] % GLA_CHUNK
    d_off = np.arange(GLA_DIRECT)[:, None, None]
    band = ((np.arange(GLA_CHUNK)[None, :] == i_pos - d_off) & (i_pos % GLA_DIRECT >= d_off))
    band = jnp.asarray(band, F32)
    rows = lambda b, h, i: b * nt + i
    return pl.pallas_call(
        _gla_kernel,
        name="gla",
        grid=(batch, GLA_HEADS, nt),
        in_specs=[
            pl.BlockSpec((tc, GLA_DK), lambda b, h, i: (rows(b, h, i), COL_QL // GLA_DK + h)),
            pl.BlockSpec((tc, GLA_DK), lambda b, h, i: (rows(b, h, i), COL_KL // GLA_DK + h)),
            pl.BlockSpec((tc, GLA_DV), lambda b, h, i: (rows(b, h, i), COL_VL // GLA_DV + h)),
            pl.BlockSpec((tc, GLA_DV), lambda b, h, i: (rows(b, h, i), COL_RL // GLA_DV + h)),
            pl.BlockSpec((tc, LANES), lambda b, h, i: (rows(b, h, i), COL_MISC // LANES)),
            pl.BlockSpec((LANES, GLA_DK), lambda b, h, i: (0, h)),
            pl.BlockSpec((1, GLA_DK), lambda b, h, i: (0, h)),
            pl.BlockSpec((1, GLA_DV), lambda b, h, i: (0, 0)),
            pl.BlockSpec((GLA_DIRECT, tc, GLA_CHUNK), lambda b, h, i: (0, 0, 0)),
        ],
        out_specs=pl.BlockSpec((tc, GLA_DV), lambda b, h, i: (rows(b, h, i), h)),
        out_shape=jax.ShapeDtypeStruct((n, D_GLA_OUT), BF16),
        scratch_shapes=[pltpu.VMEM((GLA_DV, GLA_DK), F32)],
        compiler_params=_cparams(("parallel", "parallel", "arbitrary"), 32),
    )(proj, proj, proj, proj, proj, w2, b_gk.reshape(1, -1), g_gla_out.reshape(1, GLA_DV), band)


NSA_TQ = 128
NSA_NQ = NSA_TQ * NSA_GROUP
NEG = -1e30
SLC_CHUNK = 512
WIN_KEYS = WINDOW + NSA_TQ
CMP_CHUNK = 128
FEAT = 16


def _nsa_prep_kernel(q_ref, cmp_ref, slc_ref, win_ref, misc_ref, gq_ref, gks_ref, gkw_ref, bg_ref,
                     qt_ref, kcm_ref, vcm_ref, ks_ref, vst_ref, kw_ref, vwt_ref, gt_ref):
    q = q_ref[...]
    for k in range(NSA_KV_HEADS):
        for g in range(NSA_GROUP):
            h = k * NSA_GROUP + g
            qn = _rms(q[:, h * NSA_DK:(h + 1) * NSA_DK], gq_ref[...]) * (NSA_DK ** -0.5)
            qt_ref[k, :, g * NSA_TQ:(g + 1) * NSA_TQ] = qn.T.astype(BF16)
        ksl = slice(k * NSA_DK, (k + 1) * NSA_DK)
        vsl = slice((NSA_KV_HEADS + k) * NSA_DK, (NSA_KV_HEADS + k + 1) * NSA_DK)
        kcm_ref[k] = cmp_ref[:, ksl]
        vcm_ref[k] = cmp_ref[:, vsl]
        ks_ref[k] = _rms(slc_ref[:, ksl], gks_ref[...]).astype(BF16)
        vst_ref[k] = slc_ref[:, vsl].T.astype(BF16)
        kw_ref[k] = _rms(win_ref[:, ksl], gkw_ref[...]).astype(BF16)
        vwt_ref[k] = win_ref[:, vsl].T.astype(BF16)
    gates = jax.nn.sigmoid(misc_ref[...] + bg_ref[...])
    gt_ref[...] = gates.T[:NSA_KV_HEADS * GATE_ROWS, :]


def _nsa_prep(proj, g_q, g_ks, g_kw, b_gate, batch, seq):
    nqt = seq // NSA_TQ
    hk = NSA_KV_HEADS
    pair = 2 * hk * NSA_DK
    rows = lambda b, i: b * nqt + i
    vec = lambda n: pl.BlockSpec((1, n), lambda b, i: (0, 0))
    bias = jnp.zeros((1, LANES), F32).at[0, :MISC_AL].set(_gate_layout(b_gate))
    tok = lambda dt: jax.ShapeDtypeStruct((batch, hk, seq, NSA_DK), dt)
    tr = jax.ShapeDtypeStruct((batch, hk, nqt, NSA_DV, NSA_TQ), BF16)
    tok_spec = pl.BlockSpec((None, hk, NSA_TQ, NSA_DK), lambda b, i: (b, 0, i, 0))
    tr_spec = pl.BlockSpec((None, hk, None, NSA_DV, NSA_TQ), lambda b, i: (b, 0, i, 0, 0))
    return pl.pallas_call(
        _nsa_prep_kernel,
        name="nsa_prep",
        grid=(batch, nqt),
        in_specs=[
            pl.BlockSpec((NSA_TQ, NSA_HEADS * NSA_DK), lambda b, i: (rows(b, i), 0)),
            pl.BlockSpec((NSA_TQ, pair), lambda b, i: (rows(b, i), COL_KV // pair)),
            pl.BlockSpec((NSA_TQ, pair), lambda b, i: (rows(b, i), COL_KV // pair + 1)),
            pl.BlockSpec((NSA_TQ, pair), lambda b, i: (rows(b, i), COL_KV // pair + 2)),
            pl.BlockSpec((NSA_TQ, LANES), lambda b, i: (rows(b, i), COL_MISC // LANES)),
            vec(NSA_DK), vec(NSA_DK), vec(NSA_DK), vec(LANES),
        ],
        out_specs=[
            pl.BlockSpec((None, hk, None, NSA_DK, NSA_NQ), lambda b, i: (b, 0, i, 0, 0)),
            tok_spec, tok_spec, tok_spec, tr_spec, tok_spec, tr_spec,
            pl.BlockSpec((None, hk * GATE_ROWS, NSA_TQ), lambda b, i: (b, 0, i)),
        ],
        out_shape=[
            jax.ShapeDtypeStruct((batch, hk, nqt, NSA_DK, NSA_NQ), BF16),
            tok(F32), tok(F32), tok(BF16), tr, tok(BF16), tr,
            jax.ShapeDtypeStruct((batch, hk * GATE_ROWS, seq), F32),
        ],
        compiler_params=_cparams(("parallel", "parallel"), 32),
    )(proj, proj, proj, proj, proj, g_q.reshape(1, -1), g_ks.reshape(1, -1), g_kw.reshape(1, -1), bias)


def _compress_kernel(uk_ref, uv_ref, pek_ref, pev_ref, wk1_ref, wk2_ref, wv1_ref, wv2t_ref, gk_ref,
                     kc_ref, vct_ref):
    def hidden(u_ref, pe_ref, w1_ref):
        u = u_ref[...]
        n = u.shape[0]
        lo = jnp.dot((u + pe_ref[0:1, :]).astype(BF16), w1_ref[0], preferred_element_type=F32)
        hi = jnp.dot((u + pe_ref[1:2, :]).astype(BF16), w1_ref[1], preferred_element_type=F32)
        hid = lo + pltpu.roll(hi, n - 1, 0)
        return (hid * jax.nn.sigmoid(hid)).astype(BF16)

    kc = jnp.dot(hidden(uk_ref, pek_ref, wk1_ref), wk2_ref[...], preferred_element_type=F32)
    kc_ref[...] = _rms(kc, gk_ref[...]).astype(BF16)
    vct = lax.dot_general(wv2t_ref[...], hidden(uv_ref, pev_ref, wv1_ref), _NT,
                          preferred_element_type=F32).astype(BF16)
    for c in range(vct_ref.shape[0]):
        vct_ref[c] = vct[:, c * CMP_CHUNK:(c + 1) * CMP_CHUNK]


def _compress(k_tok, v_tok, pe_k, pe_v, w_ck1, w_ck2, w_cv1, w_cv2, g_kc):
    batch, hk, seq, d = k_tok.shape
    nc = seq // CMP_STRIDE
    flat = CMP_STRIDE * d
    halves = CMP_LEN // CMP_STRIDE
    uk = k_tok.reshape(batch, hk, nc, flat)
    uv = v_tok.reshape(batch, hk, nc, flat)
    u_spec = pl.BlockSpec((None, None, nc, flat), lambda b, k: (b, k, 0, 0))
    full = lambda shape: pl.BlockSpec(shape, lambda b, k: (0,) * len(shape))
    return pl.pallas_call(
        _compress_kernel,
        name="nsa_compress",
        grid=(batch, hk),
        in_specs=[u_spec, u_spec, full((halves, flat)), full((halves, flat)),
                  full((halves, flat, CMP_HIDDEN)), full((CMP_HIDDEN, d)),
                  full((halves, flat, CMP_HIDDEN)), full((d, CMP_HIDDEN)), full((1, d))],
        out_specs=[pl.BlockSpec((None, None, nc, d), lambda b, k: (b, k, 0, 0)),
                   pl.BlockSpec((None, None, nc // CMP_CHUNK, d, CMP_CHUNK),
                                lambda b, k: (b, k, 0, 0, 0))],
        out_shape=[jax.ShapeDtypeStruct((batch, hk, nc, d), BF16),
                   jax.ShapeDtypeStruct((batch, hk, nc // CMP_CHUNK, d, CMP_CHUNK), BF16)],
        compiler_params=_cparams(("parallel", "parallel"), 48),
    )(uk, uv, pe_k.reshape(halves, flat), pe_v.reshape(halves, flat),
      w_ck1.astype(BF16).reshape(halves, flat, CMP_HIDDEN), w_ck2.astype(BF16),
      w_cv1.astype(BF16).reshape(halves, flat, CMP_HIDDEN), w_cv2.T.astype(BF16), g_kc.reshape(1, d))


def _query_rows(qt):
    lane = lax.broadcasted_iota(jnp.int32, (1, NSA_NQ), 1)
    t = qt * NSA_TQ + (lane & (NSA_TQ - 1))
    return t


def _topk_mask(imp, cur, n_top):
    ns, nt = imp.shape
    j = lax.broadcasted_iota(jnp.int32, (ns, nt), 0)
    forced = (j == 0) | (j == cur) | (j == cur - 1)
    visible = j <= cur
    work = jnp.where(visible & jnp.logical_not(forced), imp, -1.0)

    def extract(_, work):
        m = jnp.max(work, axis=0, keepdims=True)
        cand = (work == m) & (m > -0.5)
        idx = jnp.min(jnp.where(cand, j, ns), axis=0, keepdims=True)
        return jnp.where(j == idx, -2.0, work)

    work = lax.fori_loop(0, n_top - 3, extract, work)
    return (forced & visible) | (work == -2.0)


def _feature_rows(rows):
    n = rows[0].shape[1]
    r = lax.broadcasted_iota(jnp.int32, (FEAT // 2, n), 0)
    out = jnp.zeros((FEAT // 2, n), F32)
    for i, row in enumerate(rows):
        out = jnp.where(r == i, row, out)
    return out


def _aug_rhs(q, mask8, bias8):
    return jnp.concatenate([q, jnp.concatenate([mask8, bias8], axis=0).astype(BF16)], axis=0)


def _key_features(n, kind):
    a = np.arange(n)
    f = np.zeros((n, FEAT), np.float32)
    if kind == "cmp":
        f[:, 8], f[:, 9], f[:, 10], f[:, 11] = a % CMP_CHUNK, a // CMP_CHUNK, 1.0, 1.0
    else:
        f[:, 8], f[:, 9], f[:, 10] = a % SEL_BLOCK, a // SEL_BLOCK, 1.0
    if kind == "slc":
        f[a, a // SEL_BLOCK] = 1.0
    if kind == "win":
        f[a, a // NSA_TQ] = 1.0
    if kind == "own":
        f[:NSA_TQ, 10] = 0.0
        f[:NSA_TQ, 9] = a[:NSA_TQ] // SEL_BLOCK
        f[NSA_TQ:, 9] = a[:NSA_TQ] // SEL_BLOCK
        f[a[NSA_TQ:], a[:NSA_TQ] // SEL_BLOCK] = 1.0
    return jnp.asarray(f, BF16)


def _nsa_sel_kernel(qt_ref, kc_ref, kf_ref, vct_ref, mmat_ref, slope_ref, negsel_ref, flag_ref,
                    oc_ref, *, n_top):
    qt = pl.program_id(2)
    t0 = qt * NSA_TQ
    t = _query_rows(qt)
    slope = slope_ref[...]
    q = qt_ref[...]
    zeros8 = jnp.zeros((FEAT // 2, NSA_NQ), F32)
    per_tile = NSA_TQ // CMP_STRIDE
    n_chunks = (per_tile * qt + per_tile + CMP_CHUNK - 1) // CMP_CHUNK
    span = CMP_CHUNK * CMP_STRIDE
    rhs = _aug_rhs(q, zeros8, _feature_rows([CMP_STRIDE * slope, span * slope,
                                             (0.5 * (CMP_LEN - 1)) * slope, (-t0).astype(F32) * slope]))

    def branch(n):
        rows = n * CMP_CHUNK
        edge = rows - min(n, 2) * CMP_CHUNK

        def run():
            lhs = jnp.concatenate([kc_ref[:rows, :], kf_ref[:rows, :]], axis=1)
            s = jnp.dot(lhs, rhs, preferred_element_type=F32)
            ci = edge + lax.broadcasted_iota(jnp.int32, (rows - edge, 1), 0)
            tail = jnp.where(t >= ci * CMP_STRIDE + (CMP_LEN - 1), s[edge:], NEG)
            s = tail if edge == 0 else jnp.concatenate([s[:edge], tail], axis=0)
            m = jnp.maximum(jnp.max(s, axis=0, keepdims=True), 0.1 * NEG)
            p = jnp.exp(s - m)
            l = jnp.sum(p, axis=0, keepdims=True)
            inv = 1.0 / jnp.where(l > 0.0, l, 1.0)
            vt = jnp.concatenate([vct_ref[c] for c in range(n)], axis=1)
            oc = jnp.dot(vt, p.astype(BF16), preferred_element_type=F32) * inv
            p = p * inv
            psum = sum(p[:, g * NSA_TQ:(g + 1) * NSA_TQ] for g in range(NSA_GROUP))
            return oc, _dot_exact_lhs(mmat_ref[:, :rows], psum)

        return run

    oc, imp = lax.switch(n_chunks - 1, [branch(n + 1) for n in range(kc_ref.shape[0] // CMP_CHUNK)])
    oc_ref[...] = oc
    cur = lax.shift_right_logical(t[:, :NSA_TQ], 6)
    sel = _topk_mask(imp, cur, n_top)
    j = lax.broadcasted_iota(jnp.int32, sel.shape, 0)
    past = sel & (j < 2 * qt)
    negsel_ref[...] = jnp.where(past, 0.0, NEG)
    listed = (past & (j >= NSA_TQ // SEL_BLOCK)).astype(F32)
    flag = jnp.max(listed.T, axis=0, keepdims=True)
    ns = flag.shape[1]
    sh = 1
    while sh < SLC_CHUNK // SEL_BLOCK:
        flag = jnp.maximum(flag, pltpu.roll(flag, ns - sh, 1))
        sh *= 2
    flag_ref[...] = flag.astype(jnp.int32)


def _nsa_select(qt_arr, kc, vct, batch, seq):
    hk = NSA_KV_HEADS
    nqt = seq // NSA_TQ
    nc = seq // CMP_STRIDE
    ns = seq // SEL_BLOCK
    n_cmp = nc - (CMP_LEN // CMP_STRIDE - 1)
    n_top = min(SEL_TOP, ns)
    ratio, lead = SEL_BLOCK // CMP_STRIDE, CMP_LEN // CMP_STRIDE - 1
    mm = np.zeros((ns, nc), np.float32)
    for r in range(ratio + lead):
        st = CMP_STRIDE * (r - lead)
        ov = (min(st + CMP_LEN, SEL_BLOCK) - max(st, 0)) / CMP_STRIDE
        for jb in range(ns):
            i = jb * ratio + r - lead
            if 0 <= i < n_cmp:
                mm[jb, i] += ov
    slopes = jnp.exp2(-8.0 * jnp.arange(1, NSA_HEADS + 1, dtype=F32) / NSA_HEADS)
    slopes = jnp.repeat(slopes.reshape(hk, 1, NSA_GROUP), NSA_TQ, axis=2)
    negsel, flags, oc = pl.pallas_call(
        functools.partial(_nsa_sel_kernel, n_top=n_top),
        name="nsa_select",
        grid=(batch, hk, nqt),
        in_specs=[
            pl.BlockSpec((None, None, None, NSA_DK, NSA_NQ), lambda b, k, i: (b, k, i, 0, 0)),
            pl.BlockSpec((None, None, nc, NSA_DK), lambda b, k, i: (b, k, 0, 0)),
            pl.BlockSpec((nc, FEAT), lambda b, k, i: (0, 0)),
            pl.BlockSpec((None, None, nc // CMP_CHUNK, NSA_DV, CMP_CHUNK), lambda b, k, i: (b, k, 0, 0, 0)),
            pl.BlockSpec((ns, nc), lambda b, k, i: (0, 0)),
            pl.BlockSpec((None, 1, NSA_NQ), lambda b, k, i: (k, 0, 0)),
        ],
        out_specs=[
            pl.BlockSpec((None, None, None, ns, NSA_TQ), lambda b, k, i: (b, k, i, 0, 0)),
            pl.BlockSpec((None, None, None, 1, ns), lambda b, k, i: (b, k, i, 0, 0)),
            pl.BlockSpec((None, None, None, NSA_DV, NSA_NQ), lambda b, k, i: (b, k, i, 0, 0)),
        ],
        out_shape=[
            jax.ShapeDtypeStruct((batch, hk, nqt, ns, NSA_TQ), F32),
            jax.ShapeDtypeStruct((batch, hk, nqt, 1, ns), jnp.int32),
            jax.ShapeDtypeStruct((batch, hk, nqt, NSA_DV, NSA_NQ), F32),
        ],
        compiler_params=_cparams(("parallel", "parallel", "parallel"), 48),
    )(qt_arr, kc, _key_features(nc, "cmp"), vct, jnp.asarray(mm, BF16), slopes)
    return negsel, flags, oc, slopes


def _nsa_attn_kernel(flag_ref, qt_ref, ks_ref, vst_ref, kw_ref, vwt_ref, negsel_ref, oc_ref, gt_ref,
                     slope_ref, go_ref, kfs_ref, kfw_ref, kfo_ref, o_ref, m_ref, l_ref, acc_ref,
                     sa_ref, sb_ref, lst_ref):
    qt = pl.program_id(2)
    t0 = qt * NSA_TQ
    t = _query_rows(qt)
    slope = slope_ref[...]
    q = qt_ref[...]
    blocks = SLC_CHUNK // SEL_BLOCK
    a = lax.broadcasted_iota(jnp.int32, (NSA_TQ, 1), 0)
    u = t - t0
    causal = jnp.where(a <= u, 0.0, NEG)
    too_old = jnp.where(a > u, 0.0, NEG)
    dot = functools.partial(jnp.dot, preferred_element_type=F32)
    per_group = lambda rows8: jnp.concatenate([rows8] * NSA_GROUP, axis=1)
    r8 = lax.broadcasted_iota(jnp.int32, (FEAT // 2, NSA_TQ), 0)

    first8 = negsel_ref[:blocks, :]
    lhs = jnp.concatenate(
        [jnp.concatenate([ks_ref[pl.ds(pl.multiple_of(t0, NSA_TQ), NSA_TQ), :], ks_ref[:NSA_TQ, :]], axis=0),
         kfo_ref[...]], axis=1)
    s = dot(lhs, _aug_rhs(q, per_group(first8),
                          _feature_rows([slope, SEL_BLOCK * slope, (-t0).astype(F32) * slope])))
    s = jnp.concatenate([s[:NSA_TQ] + causal, s[NSA_TQ:]], axis=0)
    m = jnp.max(s, axis=0, keepdims=True)
    p = jnp.exp(s - m)
    m_ref[...] = m
    l_ref[...] = jnp.sum(p, axis=0, keepdims=True)
    acc_ref[...] = dot(jnp.concatenate([vst_ref[qt], vst_ref[0]], axis=1), p.astype(BF16))

    wsub = WIN_KEYS // NSA_TQ
    before_start = jnp.where(r8 < WINDOW // NSA_TQ - qt, NEG, 0.0)
    lhs = jnp.concatenate([kw_ref[pl.ds(pl.multiple_of(t0, NSA_TQ), WIN_KEYS), :], kfw_ref[...]], axis=1)
    sw = dot(lhs, _aug_rhs(q, per_group(before_start),
                           _feature_rows([slope, SEL_BLOCK * slope, (-float(WINDOW)) * slope])))
    sw = jnp.concatenate([sw[:NSA_TQ] + too_old, sw[NSA_TQ:WINDOW], sw[WINDOW:] + causal], axis=0)
    pw = jnp.exp(sw - jnp.max(sw, axis=0, keepdims=True))
    vt = jnp.concatenate([vwt_ref[qt + i] for i in range(wsub)], axis=1)
    o_w = dot(vt, pw.astype(BF16)) / jnp.sum(pw, axis=0, keepdims=True)

    sub = SLC_CHUNK // NSA_TQ
    lst_ref[0] = 0

    def listing(c, n):
        lst_ref[n] = c
        return n + (flag_ref[0, c * blocks] > 0).astype(jnp.int32)

    n = lax.fori_loop(0, (t0 + SLC_CHUNK - 1) // SLC_CHUNK, listing, 0)
    lst_ref[n] = lst_ref[jnp.maximum(n - 1, 0)]

    def scores(c, dst_ref):
        k0 = pl.multiple_of(c * SLC_CHUNK, SLC_CHUNK)
        neg8 = negsel_ref[pl.ds(pl.multiple_of(c * blocks, blocks), blocks), :]
        neg8 = jnp.where((r8 < NSA_TQ // SEL_BLOCK) & (c == 0), NEG, neg8)
        rhs = _aug_rhs(q, per_group(neg8),
                       _feature_rows([slope, SEL_BLOCK * slope, (k0 - t0).astype(F32) * slope]))
        lhs = jnp.concatenate([ks_ref[pl.ds(k0, SLC_CHUNK), :], kfs_ref[...]], axis=1)
        dst_ref[...] = dot(lhs, rhs)

    def update(c, src_ref):
        s = src_ref[...]
        m_old = m_ref[...]
        m_new = jnp.maximum(m_old, jnp.max(s, axis=0, keepdims=True))
        p = jnp.exp(s - m_new)
        alpha = jnp.exp(m_old - m_new)
        l_ref[...] = alpha * l_ref[...] + jnp.sum(p, axis=0, keepdims=True)
        vt = jnp.concatenate([vst_ref[c * sub + i] for i in range(sub)], axis=1)
        acc_ref[...] = alpha * acc_ref[...] + dot(vt, p.astype(BF16))
        m_ref[...] = m_new

    @pl.when(n > 0)
    def _():
        scores(lst_ref[0], sa_ref)

    def pair(tp, carry):
        i = 2 * tp
        scores(lst_ref[i + 1], sb_ref)
        update(lst_ref[i], sa_ref)

        @pl.when(i + 1 < n)
        def _():
            scores(lst_ref[i + 2], sa_ref)
            update(lst_ref[i + 1], sb_ref)

        return carry

    lax.fori_loop(0, (n + 1) // 2, pair, 0)

    def gate(br):
        return jnp.concatenate([gt_ref[br * NSA_GROUP + g:br * NSA_GROUP + g + 1, :]
                                for g in range(NSA_GROUP)], axis=1)

    o = gate(0) * oc_ref[...] + gate(1) * (acc_ref[...] / l_ref[...]) + gate(2) * o_w
    o = o * lax.rsqrt(jnp.mean(o * o, axis=0, keepdims=True) + EPS) * go_ref[...]
    for g in range(NSA_GROUP):
        sl = slice(g * NSA_TQ, (g + 1) * NSA_TQ)
        o_ref[:, g * NSA_DV:(g + 1) * NSA_DV] = o[:, sl].T.astype(o_ref.dtype)


def _nsa_attend(flags, qt_arr, ks, vst, kw, vwt, negsel, oc, gates_t, slopes, g_nsa_out, batch, seq):
    hk = NSA_KV_HEADS
    nqt = seq // NSA_TQ
    ns = seq // SEL_BLOCK
    per_tile = lambda shape: pl.BlockSpec((None, None, None) + shape, lambda b, k, i: (b, k, i, 0, 0))
    tok = lambda n: pl.BlockSpec((None, None, n, NSA_DK), lambda b, k, i: (b, k, 0, 0))
    tr = lambda n: pl.BlockSpec((None, None, n, NSA_DV, NSA_TQ), lambda b, k, i: (b, k, 0, 0, 0))
    wpad = WINDOW // NSA_TQ
    kw = jnp.pad(kw, ((0, 0), (0, 0), (WINDOW, 0), (0, 0)))
    vwt = jnp.pad(vwt, ((0, 0), (0, 0), (wpad, 0), (0, 0), (0, 0)))
    return pl.pallas_call(
        _nsa_attn_kernel,
        name="nsa_attend",
        grid=(batch, hk, nqt),
        in_specs=[
            pl.BlockSpec((None, None, None, 1, ns), lambda b, k, i: (b, k, i, 0, 0),
                         memory_space=pltpu.SMEM),
            per_tile((NSA_DK, NSA_NQ)),
            tok(seq), tr(nqt), tok(seq + WINDOW), tr(nqt + wpad),
            per_tile((ns, NSA_TQ)),
            per_tile((NSA_DV, NSA_NQ)),
            pl.BlockSpec((None, GATE_ROWS, NSA_TQ), lambda b, k, i: (b, k, i)),
            pl.BlockSpec((None, 1, NSA_NQ), lambda b, k, i: (k, 0, 0)),
            pl.BlockSpec((NSA_DV, 1), lambda b, k, i: (0, 0)),
            pl.BlockSpec((SLC_CHUNK, FEAT), lambda b, k, i: (0, 0)),
            pl.BlockSpec((WIN_KEYS, FEAT), lambda b, k, i: (0, 0)),
            pl.BlockSpec((2 * NSA_TQ, FEAT), lambda b, k, i: (0, 0)),
        ],
        out_specs=pl.BlockSpec((NSA_TQ, NSA_GROUP * NSA_DV), lambda b, k, i: (b * nqt + i, k)),
        out_shape=jax.ShapeDtypeStruct((batch * seq, D_NSA_OUT), BF16),
        scratch_shapes=[pltpu.VMEM((1, NSA_NQ), F32), pltpu.VMEM((1, NSA_NQ), F32),
                        pltpu.VMEM((NSA_DV, NSA_NQ), F32),
                        pltpu.VMEM((SLC_CHUNK, NSA_NQ), F32), pltpu.VMEM((SLC_CHUNK, NSA_NQ), F32),
                        pltpu.SMEM((seq // SLC_CHUNK + 1,), jnp.int32)],
        compiler_params=_cparams(("parallel", "parallel", "arbitrary"), 48),
    )(flags, qt_arr, ks, vst, kw, vwt, negsel, oc, gates_t, slopes, g_nsa_out.reshape(NSA_DV, 1),
      _key_features(SLC_CHUNK, "slc"), _key_features(WIN_KEYS, "win"), _key_features(2 * NSA_TQ, "own"))


def _nsa(proj, b_gate, g_q, g_kc, g_ks, g_kw, pe_k, pe_v, w_ck1, w_ck2, w_cv1, w_cv2, g_nsa_out,
         batch, seq):
    assert seq % (CMP_CHUNK * CMP_STRIDE) == 0 and seq >= WIN_KEYS
    qt_arr, k_tok, v_tok, ks, vst, kw, vwt, gates_t = _nsa_prep(proj, g_q, g_ks, g_kw, b_gate, batch, seq)
    kc, vct = _compress(k_tok, v_tok, pe_k, pe_v, w_ck1, w_ck2, w_cv1, w_cv2, g_kc)
    negsel, flags, oc, slopes = _nsa_select(qt_arr, kc, vct, batch, seq)
    return _nsa_attend(flags, qt_arr, ks, vst, kw, vwt, negsel, oc, gates_t, slopes, g_nsa_out, batch, seq)


def kernel(x, mem, g_mix, w_in, b_nsa_gate, g_q, g_kc, g_ks, g_kw, pe_k, pe_v, w_ck1, w_ck2,
           w_cv1, w_cv2, g_nsa_out, w_gk2, b_gk, g_gla_out, w_out, g_cross, g_mem, w_cq, w_ck,
           w_cv, g_cq, g_ck, w_co, g_ffn, w_gu, w_down):
    batch, seq, _ = x.shape
    x2d = x.reshape(batch * seq, D_MODEL)
    l = 0
    proj = _in_proj(x2d, g_mix[l], _pack_w_in(w_in[l]))
    o_gla = _gla(proj, w_gk2[l], b_gk[l], g_gla_out[l], batch, seq)
    o_nsa = _nsa(proj, b_nsa_gate[l], g_q[l], g_kc[l], g_ks[l], g_kw[l], pe_k[l], pe_v[l],
                 w_ck1[l], w_ck2[l], w_cv1[l], w_cv2[l], g_nsa_out[l], batch, seq)
    ck, cv = _mem_kv(mem, g_mem[l], w_ck[l], w_cv[l], g_ck[l])
    x2d = _out_cross(x2d, o_nsa, o_gla, w_out[l], g_cross[l], w_cq[l], g_cq[l], ck, cv,
                     w_co[l], seq)
    x2d = _ffn(x2d, g_ffn[l], w_gu[l], w_down[l])
    return x2d.reshape(batch, seq, D_MODEL)
```

```python
import functools

import numpy as np
import jax
import jax.numpy as jnp
from jax import lax
from jax.experimental import pallas as pl
from jax.experimental.pallas import tpu as pltpu

F32 = jnp.float32
BF16 = jnp.bfloat16

D_MODEL = 2048
MEM_LEN = 256
EPS = 1e-6
NSA_HEADS = 8
NSA_KV_HEADS = 2
NSA_GROUP = NSA_HEADS // NSA_KV_HEADS
NSA_DK = 128
NSA_DV = 128
CMP_LEN = 32
CMP_STRIDE = 16
CMP_HIDDEN = 256
SEL_BLOCK = 64
SEL_TOP = 16
WINDOW = 512
GLA_HEADS = 4
GLA_DK = 128
GLA_DV = 256
GLA_GATE_RANK = 16
GLA_GATE_NORM = 16.0
GLA_CHUNK = 64
MEM_HEADS = 4
MEM_DH = 128
D_FF = -(-8 * D_MODEL // (3 * 256)) * 256
D_NSA_OUT = NSA_HEADS * NSA_DV
D_GLA_OUT = GLA_HEADS * GLA_DV

LANES = 128
MXU_DIM = 256
VMEM_BYTES_V7X = 64 * 1024 * 1024

COL_Q = 0
COL_KV = COL_Q + NSA_HEADS * NSA_DK
COL_QL = COL_KV + 6 * NSA_KV_HEADS * NSA_DK
COL_KL = COL_QL + GLA_HEADS * GLA_DK
COL_VL = COL_KL + GLA_HEADS * GLA_DK
COL_RL = COL_VL + GLA_HEADS * GLA_DV
COL_MISC = COL_RL + GLA_HEADS * GLA_DV
MISC_GATE = 0
MISC_AL = 32
D_IN_PACKED = 6144
IN_TILE_N = 2048


def _cparams(semantics, vmem_mb):
    return pltpu.CompilerParams(dimension_semantics=semantics,
                                vmem_limit_bytes=vmem_mb * 1024 * 1024)


def _rms(u, g):
    return u * lax.rsqrt(jnp.mean(u * u, axis=-1, keepdims=True) + EPS) * g


def _inproj_kernel(x_ref, g_ref, w_ref, o_ref, h_ref):
    @pl.when(pl.program_id(1) == 0)
    def _():
        h_ref[...] = _rms(x_ref[...], g_ref[...]).astype(BF16)

    o_ref[...] = jnp.dot(h_ref[...], w_ref[...], preferred_element_type=F32)


def _in_proj(x2d, g_mix, w_packed, tm=512):
    n = x2d.shape[0]
    return pl.pallas_call(
        _inproj_kernel,
        name="in_proj",
        grid=(n // tm, D_IN_PACKED // IN_TILE_N),
        in_specs=[
            pl.BlockSpec((tm, D_MODEL), lambda i, j: (i, 0)),
            pl.BlockSpec((1, D_MODEL), lambda i, j: (0, 0)),
            pl.BlockSpec((D_MODEL, IN_TILE_N), lambda i, j: (0, j)),
        ],
        out_specs=pl.BlockSpec((tm, IN_TILE_N), lambda i, j: (i, j)),
        out_shape=jax.ShapeDtypeStruct((n, D_IN_PACKED), F32),
        scratch_shapes=[pltpu.VMEM((tm, D_MODEL), BF16)],
        compiler_params=_cparams(("parallel", "arbitrary"), 48),
    )(x2d, g_mix.reshape(1, D_MODEL), w_packed)


GATE_ROWS = 16


def _gate_layout(u):
    pos = np.arange(MISC_AL)
    k, r = pos // GATE_ROWS, pos % GATE_ROWS
    br, g = r // NSA_GROUP, r % NSA_GROUP
    used = r < 3 * NSA_GROUP
    src = np.where(used, (k * NSA_GROUP + g) * 3 + br, 0)
    return jnp.where(jnp.asarray(used), jnp.take(u, jnp.asarray(src), axis=-1), 0.0)


def _pack_w_in(w_in):
    sizes = (NSA_HEADS * NSA_DK,) + (NSA_KV_HEADS * NSA_DK,) * 6 + (3 * NSA_HEADS,) + (
        GLA_HEADS * GLA_DK, GLA_HEADS * GLA_DK, GLA_HEADS * GLA_DV, GLA_GATE_RANK, GLA_HEADS * GLA_DV)
    offs = np.concatenate([[0], np.cumsum(sizes)])
    seg = lambda i: w_in[:, offs[i]:offs[i + 1]]
    zeros = lambda n: jnp.zeros((D_MODEL, n), w_in.dtype)
    misc = jnp.concatenate([_gate_layout(seg(7)), seg(11),
                            zeros(LANES - MISC_AL - GLA_GATE_RANK)], axis=1)
    cols = [seg(0)] + [seg(i) for i in range(1, 7)] + [seg(8), seg(9), seg(10), seg(12), misc,
                                                      zeros(D_IN_PACKED - COL_MISC - LANES)]
    return jnp.concatenate(cols, axis=1).astype(BF16)


def _memkv_kernel(mem_ref, gm_ref, wk_ref, wv_ref, gk_ref, ck_ref, cv_ref):
    hm = _rms(mem_ref[...], gm_ref[...]).astype(BF16)
    k = jnp.dot(hm, wk_ref[...], preferred_element_type=F32)
    v = jnp.dot(hm, wv_ref[...], preferred_element_type=F32)
    for h in range(MEM_HEADS):
        sl = slice(h * MEM_DH, (h + 1) * MEM_DH)
        ck_ref[h] = _rms(k[:, sl], gk_ref[...]).astype(BF16)
        cv_ref[h] = v[:, sl].astype(BF16)


def _mem_kv(mem, g_mem, w_ck, w_cv, g_ck):
    b, m, _ = mem.shape
    dm = MEM_HEADS * MEM_DH
    out = jax.ShapeDtypeStruct((b, MEM_HEADS, m, MEM_DH), BF16)
    return pl.pallas_call(
        _memkv_kernel,
        name="mem_kv",
        grid=(b,),
        in_specs=[
            pl.BlockSpec((None, m, D_MODEL), lambda i: (i, 0, 0)),
            pl.BlockSpec((1, D_MODEL), lambda i: (0, 0)),
            pl.BlockSpec((D_MODEL, dm), lambda i: (0, 0)),
            pl.BlockSpec((D_MODEL, dm), lambda i: (0, 0)),
            pl.BlockSpec((1, MEM_DH), lambda i: (0, 0)),
        ],
        out_specs=[pl.BlockSpec((None, MEM_HEADS, m, MEM_DH), lambda i: (i, 0, 0, 0))] * 2,
        out_shape=[out, out],
        compiler_params=_cparams(("parallel",), 32),
    )(mem, g_mem.reshape(1, D_MODEL), w_ck.astype(BF16), w_cv.astype(BF16), g_ck.reshape(1, MEM_DH))


def _outx_kernel(x_ref, nsa_ref, gla_ref, wo1_ref, wo2_ref, gc_ref, wcq_ref, gcq_ref,
                 ck_ref, cv_ref, wco_ref, o_ref):
    x1 = (x_ref[...]
          + jnp.dot(nsa_ref[...], wo1_ref[...], preferred_element_type=F32)
          + jnp.dot(gla_ref[...], wo2_ref[...], preferred_element_type=F32))
    hq = _rms(x1, gc_ref[...]).astype(BF16)
    cq = jnp.dot(hq, wcq_ref[...], preferred_element_type=F32)
    outs = []
    for h in range(MEM_HEADS):
        c = _rms(cq[:, h * MEM_DH:(h + 1) * MEM_DH], gcq_ref[...]) * (MEM_DH ** -0.5)
        s = lax.dot_general(c.astype(BF16), ck_ref[h], (((1,), (1,)), ((), ())),
                            preferred_element_type=F32)
        p = jnp.exp(s - jnp.max(s, axis=-1, keepdims=True))
        p = p / jnp.sum(p, axis=-1, keepdims=True)
        outs.append(jnp.dot(p.astype(BF16), cv_ref[h], preferred_element_type=F32))
    oc = jnp.concatenate(outs, axis=-1).astype(BF16)
    o_ref[...] = x1 + jnp.dot(oc, wco_ref[...], preferred_element_type=F32)


def _out_cross(x2d, o_nsa, o_gla, w_out, g_cross, w_cq, g_cq, ck, cv, w_co, seq, tm=512):
    n = x2d.shape[0]
    tiles_per_batch = seq // tm
    dm = MEM_HEADS * MEM_DH
    m = ck.shape[2]
    full = lambda shape: pl.BlockSpec(shape, lambda i: (0,) * len(shape))
    kv_spec = pl.BlockSpec((None, MEM_HEADS, m, MEM_DH), lambda i: (i // tiles_per_batch, 0, 0, 0))
    w_out = w_out.astype(BF16)
    return pl.pallas_call(
        _outx_kernel,
        name="out_cross",
        grid=(n // tm,),
        in_specs=[
            pl.BlockSpec((tm, D_MODEL), lambda i: (i, 0)),
            pl.BlockSpec((tm, D_NSA_OUT), lambda i: (i, 0)),
            pl.BlockSpec((tm, D_GLA_OUT), lambda i: (i, 0)),
            full((D_NSA_OUT, D_MODEL)),
            full((D_GLA_OUT, D_MODEL)),
            full((1, D_MODEL)),
            full((D_MODEL, dm)),
            full((1, MEM_DH)),
            kv_spec,
            kv_spec,
            full((dm, D_MODEL)),
        ],
        out_specs=pl.BlockSpec((tm, D_MODEL), lambda i: (i, 0)),
        out_shape=jax.ShapeDtypeStruct((n, D_MODEL), F32),
        compiler_params=_cparams(("parallel",), 56),
    )(x2d, o_nsa, o_gla, w_out[:D_NSA_OUT], w_out[D_NSA_OUT:], g_cross.reshape(1, D_MODEL),
      w_cq.astype(BF16), g_cq.reshape(1, MEM_DH), ck, cv, w_co.astype(BF16))


def _ffn_kernel(x_ref, g_ref, wg_ref, wu_ref, wd_ref, o_ref, h_ref):
    @pl.when(pl.program_id(1) == 0)
    def _():
        x = x_ref[...]
        h_ref[...] = _rms(x, g_ref[...]).astype(BF16)
        o_ref[...] = x

    h = h_ref[...]
    g = jnp.dot(h, wg_ref[...], preferred_element_type=F32)
    u = jnp.dot(h, wu_ref[...], preferred_element_type=F32)
    a = (g * jax.nn.sigmoid(g) * u).astype(BF16)
    o_ref[...] += jnp.dot(a, wd_ref[...], preferred_element_type=F32)


def _ffn(x2d, g_ffn, w_gu, w_down, tm=512, tf=512):
    n = x2d.shape[0]
    w_gu = w_gu.astype(BF16)
    return pl.pallas_call(
        _ffn_kernel,
        name="ffn",
        grid=(n // tm, D_FF // tf),
        in_specs=[
            pl.BlockSpec((tm, D_MODEL), lambda i, j: (i, 0)),
            pl.BlockSpec((1, D_MODEL), lambda i, j: (0, 0)),
            pl.BlockSpec((D_MODEL, tf), lambda i, j: (0, j)),
            pl.BlockSpec((D_MODEL, tf), lambda i, j: (0, j)),
            pl.BlockSpec((tf, D_MODEL), lambda i, j: (j, 0)),
        ],
        out_specs=pl.BlockSpec((tm, D_MODEL), lambda i, j: (i, 0)),
        out_shape=jax.ShapeDtypeStruct((n, D_MODEL), F32),
        scratch_shapes=[pltpu.VMEM((tm, D_MODEL), BF16)],
        compiler_params=_cparams(("parallel", "arbitrary"), 48),
    )(x2d, g_ffn.reshape(1, D_MODEL), w_gu[:, :D_FF], w_gu[:, D_FF:], w_down.astype(BF16))


def _split3(x):
    hi = x.astype(BF16)
    r = x - hi.astype(F32)
    mid = r.astype(BF16)
    lo = (r - mid.astype(F32)).astype(BF16)
    return hi, mid, lo


def _dot_exact_lhs(a_bf16, x):
    return sum(jnp.dot(a_bf16, p, preferred_element_type=F32) for p in _split3(x))


def _dot_split(a, w):
    a_hi, a_lo, _ = _split3(a)
    w_hi, w_lo, _ = _split3(w)
    d = lambda p, q: jnp.dot(p, q, preferred_element_type=F32)
    return d(a_hi, w_hi) + d(a_hi, w_lo) + d(a_lo, w_hi)


_NT = (((1,), (1,)), ((), ()))
_TN = (((0,), (0,)), ((), ()))
GLA_DIRECT = 8


def _gla_kernel(ql_ref, kl_ref, vl_ref, rl_ref, misc_ref, w2_ref, bg_ref, go_ref, band_ref,
                o_ref, st_ref):
    c_len = GLA_CHUNK
    tc = ql_ref.shape[0]
    n_chunks = tc // c_len

    @pl.when(pl.program_id(2) == 0)
    def _():
        st_ref[...] = jnp.zeros_like(st_ref)

    x = _dot_split(misc_ref[...], w2_ref[...]) + bg_ref[...]
    la = (jnp.minimum(x, 0.0) - jnp.log(1.0 + jnp.exp(-jnp.abs(x)))) * (1.0 / GLA_GATE_NORM)
    pos = lax.broadcasted_iota(jnp.int32, (tc, 1), 0) & (c_len - 1)
    b = la
    sh = 1
    while sh < c_len:
        b = b + jnp.where(pos >= sh, pltpu.roll(b, sh, 0), 0.0)
        sh *= 2
    q = ql_ref[...] * (GLA_DK ** -0.5)
    k = kl_ref[...]
    b3 = b.reshape(n_chunks, c_len, GLA_DK)

    def chunk_row(r, n):
        return jnp.broadcast_to(b3[:, r:r + 1, :], (n_chunks, n, GLA_DK))

    level_q, level_k, level_s = [], [], []
    s = c_len
    while s > GLA_DIRECT:
        half = s // 2
        ref = jnp.concatenate([chunk_row(m0 + half - 1, s) for m0 in range(0, c_len, s)],
                              axis=1).reshape(tc, GLA_DK)
        second = (pos & (s - 1)) >= half
        level_q.append(jnp.where(second, q * jnp.exp(jnp.minimum(b - ref, 0.0)), 0.0).astype(BF16))
        level_k.append(jnp.where(second, 0.0, k * jnp.exp(jnp.minimum(ref - b, 0.0))).astype(BF16))
        level_s.append(s)
        s = half
    sublanes = 8
    group = lambda u: u.reshape(tc // sublanes, sublanes, GLA_DK)
    band = jnp.zeros((tc, c_len), F32)
    for d in range(GLA_DIRECT):
        k_sh = k if d == 0 else pltpu.roll(group(k), d, 1).reshape(tc, GLA_DK)
        b_sh = b if d == 0 else pltpu.roll(group(b), d, 1).reshape(tc, GLA_DK)
        cd = jnp.sum(q * k_sh * jnp.exp(jnp.minimum(b - b_sh, 0.0)), axis=-1, keepdims=True)
        band = band + cd * band_ref[d]
    b_end = chunk_row(c_len - 1, c_len).reshape(tc, GLA_DK)
    q_dec = (q * jnp.exp(b)).astype(BF16)
    k_dec = (k * jnp.exp(b_end - b)).astype(BF16)
    s_dec = jnp.exp(b_end)
    v = vl_ref[...].astype(BF16)
    row = lax.broadcasted_iota(jnp.int32, (c_len, 1), 0)
    col = lax.broadcasted_iota(jnp.int32, (1, c_len), 1)

    st = st_ref[...]
    for c in range(n_chunks):
        sl = slice(c * c_len, (c + 1) * c_len)
        attn = band[sl]
        for qs, ks, s in zip(level_q, level_k, level_s):
            a_s = lax.dot_general(qs[sl], ks[sl], _NT, preferred_element_type=F32)
            attn = attn + (a_s if s == c_len else jnp.where((row // s) == (col // s), a_s, 0.0))
        o = (jnp.dot(attn.astype(BF16), v[sl], preferred_element_type=F32)
             + lax.dot_general(q_dec[sl], st.astype(BF16), _NT, preferred_element_type=F32))
        st = (st * s_dec[c * c_len:c * c_len + 1]
              + lax.dot_general(v[sl], k_dec[sl], _TN, preferred_element_type=F32))
        r = rl_ref[sl, :]
        o_ref[sl, :] = (_rms(o, go_ref[...]) * (r * jax.nn.sigmoid(r))).astype(o_ref.dtype)
    st_ref[...] = st


def _gla(proj, w_gk2, b_gk, g_gla_out, batch, seq, tc=512):
    n = proj.shape[0]
    nt = seq // tc
    w2 = jnp.zeros((LANES, GLA_HEADS * GLA_DK), F32).at[MISC_AL:MISC_AL + GLA_GATE_RANK].set(w_gk2)
    i_pos = np.arange(tc)[:, None] % GLA_CHUNK
    d_off = np.arange(GLA_DIRECT)[:, None, None]
    band = ((np.arange(GLA_CHUNK)[None, :] == i_pos - d_off) & (i_pos % GLA_DIRECT >= d_off))
    band = jnp.asarray(band, F32)
    rows = lambda b, h, i: b * nt + i
    return pl.pallas_call(
        _gla_kernel,
        name="gla",
        grid=(batch, GLA_HEADS, nt),
        in_specs=[
            pl.BlockSpec((tc, GLA_DK), lambda b, h, i: (rows(b, h, i), COL_QL // GLA_DK + h)),
            pl.BlockSpec((tc, GLA_DK), lambda b, h, i: (rows(b, h, i), COL_KL // GLA_DK + h)),
            pl.BlockSpec((tc, GLA_DV), lambda b, h, i: (rows(b, h, i), COL_VL // GLA_DV + h)),
            pl.BlockSpec((tc, GLA_DV), lambda b, h, i: (rows(b, h, i), COL_RL // GLA_DV + h)),
            pl.BlockSpec((tc, LANES), lambda b, h, i: (rows(b, h, i), COL_MISC // LANES)),
            pl.BlockSpec((LANES, GLA_DK), lambda b, h, i: (0, h)),
            pl.BlockSpec((1, GLA_DK), lambda b, h, i: (0, h)),
            pl.BlockSpec((1, GLA_DV), lambda b, h, i: (0, 0)),
            pl.BlockSpec((GLA_DIRECT, tc, GLA_CHUNK), lambda b, h, i: (0, 0, 0)),
        ],
        out_specs=pl.BlockSpec((tc, GLA_DV), lambda b, h, i: (rows(b, h, i), h)),
        out_shape=jax.ShapeDtypeStruct((n, D_GLA_OUT), BF16),
        scratch_shapes=[pltpu.VMEM((GLA_DV, GLA_DK), F32)],
        compiler_params=_cparams(("parallel", "parallel", "arbitrary"), 32),
    )(proj, proj, proj, proj, proj, w2, b_gk.reshape(1, -1), g_gla_out.reshape(1, GLA_DV), band)


NSA_TQ = 256
NSA_NQ = NSA_TQ * NSA_GROUP
NEG = -1e30
SLC_CHUNK = 512
WIN_KEYS = WINDOW + NSA_TQ
CMP_CHUNK = 128
FEAT = 16


def _nsa_prep_kernel(q_ref, cmp_ref, slc_ref, win_ref, misc_ref, gq_ref, gks_ref, gkw_ref, bg_ref,
                     qt_ref, kcm_ref, vcm_ref, ks_ref, vst_ref, kw_ref, vwt_ref, gt_ref):
    q = q_ref[...]
    for k in range(NSA_KV_HEADS):
        for g in range(NSA_GROUP):
            h = k * NSA_GROUP + g
            qn = _rms(q[:, h * NSA_DK:(h + 1) * NSA_DK], gq_ref[...]) * (NSA_DK ** -0.5)
            qt_ref[k, :, g * NSA_TQ:(g + 1) * NSA_TQ] = qn.T.astype(BF16)
        ksl = slice(k * NSA_DK, (k + 1) * NSA_DK)
        vsl = slice((NSA_KV_HEADS + k) * NSA_DK, (NSA_KV_HEADS + k + 1) * NSA_DK)
        kcm_ref[k] = cmp_ref[:, ksl]
        vcm_ref[k] = cmp_ref[:, vsl]
        ks_ref[k] = _rms(slc_ref[:, ksl], gks_ref[...]).astype(BF16)
        vst_ref[k] = slc_ref[:, vsl].T.astype(BF16)
        kw_ref[k] = _rms(win_ref[:, ksl], gkw_ref[...]).astype(BF16)
        vwt_ref[k] = win_ref[:, vsl].T.astype(BF16)
    gates = jax.nn.sigmoid(misc_ref[...] + bg_ref[...])
    gt_ref[...] = gates.T[:NSA_KV_HEADS * GATE_ROWS, :]


def _nsa_prep(proj, g_q, g_ks, g_kw, b_gate, batch, seq):
    nqt = seq // NSA_TQ
    hk = NSA_KV_HEADS
    pair = 2 * hk * NSA_DK
    rows = lambda b, i: b * nqt + i
    vec = lambda n: pl.BlockSpec((1, n), lambda b, i: (0, 0))
    bias = jnp.zeros((1, LANES), F32).at[0, :MISC_AL].set(_gate_layout(b_gate))
    tok = lambda dt: jax.ShapeDtypeStruct((batch, hk, seq, NSA_DK), dt)
    tr = jax.ShapeDtypeStruct((batch, hk, nqt, NSA_DV, NSA_TQ), BF16)
    tok_spec = pl.BlockSpec((None, hk, NSA_TQ, NSA_DK), lambda b, i: (b, 0, i, 0))
    tr_spec = pl.BlockSpec((None, hk, None, NSA_DV, NSA_TQ), lambda b, i: (b, 0, i, 0, 0))
    return pl.pallas_call(
        _nsa_prep_kernel,
        name="nsa_prep",
        grid=(batch, nqt),
        in_specs=[
            pl.BlockSpec((NSA_TQ, NSA_HEADS * NSA_DK), lambda b, i: (rows(b, i), 0)),
            pl.BlockSpec((NSA_TQ, pair), lambda b, i: (rows(b, i), COL_KV // pair)),
            pl.BlockSpec((NSA_TQ, pair), lambda b, i: (rows(b, i), COL_KV // pair + 1)),
            pl.BlockSpec((NSA_TQ, pair), lambda b, i: (rows(b, i), COL_KV // pair + 2)),
            pl.BlockSpec((NSA_TQ, LANES), lambda b, i: (rows(b, i), COL_MISC // LANES)),
            vec(NSA_DK), vec(NSA_DK), vec(NSA_DK), vec(LANES),
        ],
        out_specs=[
            pl.BlockSpec((None, hk, None, NSA_DK, NSA_NQ), lambda b, i: (b, 0, i, 0, 0)),
            tok_spec, tok_spec, tok_spec, tr_spec, tok_spec, tr_spec,
            pl.BlockSpec((None, hk * GATE_ROWS, NSA_TQ), lambda b, i: (b, 0, i)),
        ],
        out_shape=[
            jax.ShapeDtypeStruct((batch, hk, nqt, NSA_DK, NSA_NQ), BF16),
            tok(F32), tok(F32), tok(BF16), tr, tok(BF16), tr,
            jax.ShapeDtypeStruct((batch, hk * GATE_ROWS, seq), F32),
        ],
        compiler_params=_cparams(("parallel", "parallel"), 32),
    )(proj, proj, proj, proj, proj, g_q.reshape(1, -1), g_ks.reshape(1, -1), g_kw.reshape(1, -1), bias)


def _compress_kernel(uk_ref, uv_ref, pek_ref, pev_ref, wk1_ref, wk2_ref, wv1_ref, wv2t_ref, gk_ref,
                     kc_ref, vct_ref):
    def hidden(u_ref, pe_ref, w1_ref):
        n = u_ref.shape[0] // CMP_STRIDE
        toks = [u_ref[pl.ds(l, n, stride=CMP_STRIDE), :] for l in range(CMP_STRIDE)]

        def half(h):
            rows = [(toks[l] + pe_ref[h * CMP_STRIDE + l:h * CMP_STRIDE + l + 1, :]).astype(BF16)
                    for l in range(CMP_STRIDE)]
            return jnp.dot(jnp.concatenate(rows, axis=1), w1_ref[h], preferred_element_type=F32)

        hid = half(0) + pltpu.roll(half(1), n - 1, 0)
        return (hid * jax.nn.sigmoid(hid)).astype(BF16)

    kc = jnp.dot(hidden(uk_ref, pek_ref, wk1_ref), wk2_ref[...], preferred_element_type=F32)
    kc_ref[...] = _rms(kc, gk_ref[...]).astype(BF16)
    vct = lax.dot_general(wv2t_ref[...], hidden(uv_ref, pev_ref, wv1_ref), _NT,
                          preferred_element_type=F32).astype(BF16)
    for c in range(vct_ref.shape[0]):
        vct_ref[c] = vct[:, c * CMP_CHUNK:(c + 1) * CMP_CHUNK]


def _compress(k_tok, v_tok, pe_k, pe_v, w_ck1, w_ck2, w_cv1, w_cv2, g_kc):
    batch, hk, seq, d = k_tok.shape
    nc = seq // CMP_STRIDE
    flat = CMP_STRIDE * d
    halves = CMP_LEN // CMP_STRIDE
    u_spec = pl.BlockSpec((None, None, seq, d), lambda b, k: (b, k, 0, 0))
    full = lambda shape: pl.BlockSpec(shape, lambda b, k: (0,) * len(shape))
    return pl.pallas_call(
        _compress_kernel,
        name="nsa_compress",
        grid=(batch, hk),
        in_specs=[u_spec, u_spec, full((CMP_LEN, d)), full((CMP_LEN, d)),
                  full((halves, flat, CMP_HIDDEN)), full((CMP_HIDDEN, d)),
                  full((halves, flat, CMP_HIDDEN)), full((d, CMP_HIDDEN)), full((1, d))],
        out_specs=[pl.BlockSpec((None, None, nc, d), lambda b, k: (b, k, 0, 0)),
                   pl.BlockSpec((None, None, nc // CMP_CHUNK, d, CMP_CHUNK),
                                lambda b, k: (b, k, 0, 0, 0))],
        out_shape=[jax.ShapeDtypeStruct((batch, hk, nc, d), BF16),
                   jax.ShapeDtypeStruct((batch, hk, nc // CMP_CHUNK, d, CMP_CHUNK), BF16)],
        compiler_params=_cparams(("parallel", "parallel"), 48),
    )(k_tok, v_tok, pe_k, pe_v,
      w_ck1.astype(BF16).reshape(halves, flat, CMP_HIDDEN), w_ck2.astype(BF16),
      w_cv1.astype(BF16).reshape(halves, flat, CMP_HIDDEN), w_cv2.T.astype(BF16), g_kc.reshape(1, d))


def _query_rows(qt):
    lane = lax.broadcasted_iota(jnp.int32, (1, NSA_NQ), 1)
    t = qt * NSA_TQ + (lane & (NSA_TQ - 1))
    return t


def _topk_mask(imp, cur, n_top):
    ns, nt = imp.shape
    j = lax.broadcasted_iota(jnp.int32, (ns, nt), 0)
    forced = (j == 0) | (j == cur) | (j == cur - 1)
    visible = j <= cur
    work = jnp.where(visible & jnp.logical_not(forced), imp, -1.0)

    def extract(_, work):
        m = jnp.max(work, axis=0, keepdims=True)
        cand = (work == m) & (m > -0.5)
        idx = jnp.min(jnp.where(cand, j, ns), axis=0, keepdims=True)
        return jnp.where(j == idx, -2.0, work)

    work = lax.fori_loop(0, n_top - 3, extract, work)
    return (forced & visible) | (work == -2.0)


def _feature_rows(rows):
    n = rows[0].shape[1]
    r = lax.broadcasted_iota(jnp.int32, (FEAT // 2, n), 0)
    out = jnp.zeros((FEAT // 2, n), F32)
    for i, row in enumerate(rows):
        out = jnp.where(r == i, row, out)
    return out


def _aug_rhs(q, mask8, bias8):
    return jnp.concatenate([q, jnp.concatenate([mask8, bias8], axis=0).astype(BF16)], axis=0)


def _key_features(n, kind):
    a = np.arange(n)
    f = np.zeros((n, FEAT), np.float32)
    if kind == "cmp":
        f[:, 8], f[:, 9], f[:, 10], f[:, 11] = a % CMP_CHUNK, a // CMP_CHUNK, 1.0, 1.0
    else:
        f[:, 8], f[:, 9], f[:, 10] = a % SEL_BLOCK, a // SEL_BLOCK, 1.0
    if kind == "slc":
        f[a, a // SEL_BLOCK] = 1.0
    if kind == "win":
        f[a, a // NSA_TQ] = 1.0
    if kind == "own":
        own_blk = a[:NSA_TQ] // SEL_BLOCK
        f[:NSA_TQ, 10] = 0.0
        f[NSA_TQ:, 9] = own_blk
        f[a[NSA_TQ:], own_blk] = 1.0
        f[a[:NSA_TQ], NSA_TQ // SEL_BLOCK + own_blk] = 1.0
    return jnp.asarray(f, BF16)


def _nsa_sel_kernel(qt_ref, kc_ref, kf_ref, vct_ref, mmat_ref, slope_ref, negsel_ref, flag_ref,
                    oc_ref, near_ref, *, n_top):
    qt = pl.program_id(2)
    t0 = qt * NSA_TQ
    t = _query_rows(qt)
    slope = slope_ref[...]
    q = qt_ref[...]
    zeros8 = jnp.zeros((FEAT // 2, NSA_NQ), F32)
    per_tile = NSA_TQ // CMP_STRIDE
    n_chunks = (per_tile * qt + per_tile + CMP_CHUNK - 1) // CMP_CHUNK
    span = CMP_CHUNK * CMP_STRIDE
    rhs = _aug_rhs(q, zeros8, _feature_rows([CMP_STRIDE * slope, span * slope,
                                             (0.5 * (CMP_LEN - 1)) * slope, (-t0).astype(F32) * slope]))

    def branch(n):
        rows = n * CMP_CHUNK
        edge = rows - min(n, 2) * CMP_CHUNK

        def run():
            lhs = jnp.concatenate([kc_ref[:rows, :], kf_ref[:rows, :]], axis=1)
            s = jnp.dot(lhs, rhs, preferred_element_type=F32)
            ci = edge + lax.broadcasted_iota(jnp.int32, (rows - edge, 1), 0)
            tail = jnp.where(t >= ci * CMP_STRIDE + (CMP_LEN - 1), s[edge:], NEG)
            s = tail if edge == 0 else jnp.concatenate([s[:edge], tail], axis=0)
            m = jnp.maximum(jnp.max(s, axis=0, keepdims=True), 0.1 * NEG)
            p = jnp.exp(s - m)
            l = jnp.sum(p, axis=0, keepdims=True)
            inv = 1.0 / jnp.where(l > 0.0, l, 1.0)
            vt = jnp.concatenate([vct_ref[c] for c in range(n)], axis=1)
            oc = jnp.dot(vt, p.astype(BF16), preferred_element_type=F32) * inv
            p = p * inv
            psum = sum(p[:, g * NSA_TQ:(g + 1) * NSA_TQ] for g in range(NSA_GROUP))
            return oc, _dot_exact_lhs(mmat_ref[:, :rows], psum)

        return run

    oc, imp = lax.switch(n_chunks - 1, [branch(n + 1) for n in range(kc_ref.shape[0] // CMP_CHUNK)])
    oc_ref[...] = oc
    cur = lax.shift_right_logical(t[:, :NSA_TQ], 6)
    sel = _topk_mask(imp, cur, n_top)
    j = lax.broadcasted_iota(jnp.int32, sel.shape, 0)
    per_own = NSA_TQ // SEL_BLOCK
    past = sel & (j < per_own * qt)
    negsel = jnp.where(past, 0.0, NEG)
    negsel_ref[...] = negsel
    own = [jnp.max(jnp.where(sel & (j == per_own * qt + r), 1.0, 0.0), axis=0, keepdims=True)
           for r in range(per_own)]
    near_ref[...] = _feature_rows([negsel[r:r + 1, :] for r in range(per_own)]
                                  + [jnp.where(o > 0.5, 0.0, NEG) for o in own])
    listed = (past & (j >= NSA_TQ // SEL_BLOCK)).astype(F32)
    flag = jnp.max(listed.T, axis=0, keepdims=True)
    ns = flag.shape[1]
    sh = 1
    while sh < SLC_CHUNK // SEL_BLOCK:
        flag = jnp.maximum(flag, pltpu.roll(flag, ns - sh, 1))
        sh *= 2
    flag_ref[...] = flag.astype(jnp.int32)


def _nsa_select(qt_arr, kc, vct, batch, seq):
    hk = NSA_KV_HEADS
    nqt = seq // NSA_TQ
    nc = seq // CMP_STRIDE
    ns = seq // SEL_BLOCK
    n_cmp = nc - (CMP_LEN // CMP_STRIDE - 1)
    n_top = min(SEL_TOP, ns)
    ratio, lead = SEL_BLOCK // CMP_STRIDE, CMP_LEN // CMP_STRIDE - 1
    mm = np.zeros((ns, nc), np.float32)
    for r in range(ratio + lead):
        st = CMP_STRIDE * (r - lead)
        ov = (min(st + CMP_LEN, SEL_BLOCK) - max(st, 0)) / CMP_STRIDE
        for jb in range(ns):
            i = jb * ratio + r - lead
            if 0 <= i < n_cmp:
                mm[jb, i] += ov
    slopes = jnp.exp2(-8.0 * jnp.arange(1, NSA_HEADS + 1, dtype=F32) / NSA_HEADS)
    slopes = jnp.repeat(slopes.reshape(hk, 1, NSA_GROUP), NSA_TQ, axis=2)
    negsel, flags, oc, near = pl.pallas_call(
        functools.partial(_nsa_sel_kernel, n_top=n_top),
        name="nsa_select",
        grid=(batch, hk, nqt),
        in_specs=[
            pl.BlockSpec((None, None, None, NSA_DK, NSA_NQ), lambda b, k, i: (b, k, i, 0, 0)),
            pl.BlockSpec((None, None, nc, NSA_DK), lambda b, k, i: (b, k, 0, 0)),
            pl.BlockSpec((nc, FEAT), lambda b, k, i: (0, 0)),
            pl.BlockSpec((None, None, nc // CMP_CHUNK, NSA_DV, CMP_CHUNK), lambda b, k, i: (b, k, 0, 0, 0)),
            pl.BlockSpec((ns, nc), lambda b, k, i: (0, 0)),
            pl.BlockSpec((None, 1, NSA_NQ), lambda b, k, i: (k, 0, 0)),
        ],
        out_specs=[
            pl.BlockSpec((None, None, None, ns, NSA_TQ), lambda b, k, i: (b, k, i, 0, 0)),
            pl.BlockSpec((None, None, None, 1, ns), lambda b, k, i: (b, k, i, 0, 0)),
            pl.BlockSpec((None, None, None, NSA_DV, NSA_NQ), lambda b, k, i: (b, k, i, 0, 0)),
            pl.BlockSpec((None, None, None, FEAT // 2, NSA_TQ), lambda b, k, i: (b, k, i, 0, 0)),
        ],
        out_shape=[
            jax.ShapeDtypeStruct((batch, hk, nqt, ns, NSA_TQ), F32),
            jax.ShapeDtypeStruct((batch, hk, nqt, 1, ns), jnp.int32),
            jax.ShapeDtypeStruct((batch, hk, nqt, NSA_DV, NSA_NQ), F32),
            jax.ShapeDtypeStruct((batch, hk, nqt, FEAT // 2, NSA_TQ), F32),
        ],
        compiler_params=_cparams(("parallel", "parallel", "parallel"), 48),
    )(qt_arr, kc, _key_features(nc, "cmp"), vct, jnp.asarray(mm, BF16), slopes)
    return negsel, flags, oc, near, slopes


def _nsa_attn_kernel(flag_ref, qt_ref, ks_ref, vst_ref, kw_ref, vwt_ref, negsel_ref, near_ref, oc_ref, gt_ref,
                     slope_ref, go_ref, kfs_ref, kfw_ref, kfo_ref, o_ref, m_ref, l_ref, acc_ref,
                     sa_ref, sb_ref, lst_ref):
    qt = pl.program_id(2)
    t0 = qt * NSA_TQ
    t = _query_rows(qt)
    slope = slope_ref[...]
    q = qt_ref[...]
    blocks = SLC_CHUNK // SEL_BLOCK
    a = lax.broadcasted_iota(jnp.int32, (NSA_TQ, 1), 0)
    u = t - t0
    causal = jnp.where(a <= u, 0.0, NEG)
    too_old = jnp.where(a > u, 0.0, NEG)
    dot = functools.partial(jnp.dot, preferred_element_type=F32)
    per_group = lambda rows8: jnp.concatenate([rows8] * NSA_GROUP, axis=1)
    r8 = lax.broadcasted_iota(jnp.int32, (FEAT // 2, NSA_TQ), 0)

    lhs = jnp.concatenate(
        [jnp.concatenate([ks_ref[pl.ds(pl.multiple_of(t0, NSA_TQ), NSA_TQ), :], ks_ref[:NSA_TQ, :]], axis=0),
         kfo_ref[...]], axis=1)
    s = dot(lhs, _aug_rhs(q, per_group(near_ref[...]),
                          _feature_rows([slope, SEL_BLOCK * slope, (-t0).astype(F32) * slope])))
    s = jnp.concatenate([s[:NSA_TQ] + causal, s[NSA_TQ:]], axis=0)
    m = jnp.max(s, axis=0, keepdims=True)
    p = jnp.exp(s - m)
    m_ref[...] = m
    l_ref[...] = jnp.sum(p, axis=0, keepdims=True)
    acc_ref[...] = dot(jnp.concatenate([vst_ref[qt], vst_ref[0]], axis=1), p.astype(BF16))

    wsub = WIN_KEYS // NSA_TQ
    before_start = jnp.where(r8 < WINDOW // NSA_TQ - qt, NEG, 0.0)
    lhs = jnp.concatenate([kw_ref[pl.ds(pl.multiple_of(t0, NSA_TQ), WIN_KEYS), :], kfw_ref[...]], axis=1)
    sw = dot(lhs, _aug_rhs(q, per_group(before_start),
                           _feature_rows([slope, SEL_BLOCK * slope, (-float(WINDOW)) * slope])))
    sw = jnp.concatenate([sw[:NSA_TQ] + too_old, sw[NSA_TQ:WINDOW], sw[WINDOW:] + causal], axis=0)
    pw = jnp.exp(sw - jnp.max(sw, axis=0, keepdims=True))
    vt = jnp.concatenate([vwt_ref[qt + i] for i in range(wsub)], axis=1)
    o_w = dot(vt, pw.astype(BF16)) / jnp.sum(pw, axis=0, keepdims=True)

    sub = SLC_CHUNK // NSA_TQ
    lst_ref[0] = 0

    def listing(c, n):
        lst_ref[n] = c
        return n + (flag_ref[0, c * blocks] > 0).astype(jnp.int32)

    n = lax.fori_loop(0, (t0 + SLC_CHUNK - 1) // SLC_CHUNK, listing, 0)
    lst_ref[n] = lst_ref[jnp.maximum(n - 1, 0)]

    def scores(c, dst_ref):
        k0 = pl.multiple_of(c * SLC_CHUNK, SLC_CHUNK)
        neg8 = negsel_ref[pl.ds(pl.multiple_of(c * blocks, blocks), blocks), :]
        neg8 = jnp.where((r8 < NSA_TQ // SEL_BLOCK) & (c == 0), NEG, neg8)
        rhs = _aug_rhs(q, per_group(neg8),
                       _feature_rows([slope, SEL_BLOCK * slope, (k0 - t0).astype(F32) * slope]))
        lhs = jnp.concatenate([ks_ref[pl.ds(k0, SLC_CHUNK), :], kfs_ref[...]], axis=1)
        dst_ref[...] = dot(lhs, rhs)

    def update(c, src_ref):
        s = src_ref[...]
        m_old = m_ref[...]
        m_new = jnp.maximum(m_old, jnp.max(s, axis=0, keepdims=True))
        p = jnp.exp(s - m_new)
        alpha = jnp.exp(m_old - m_new)
        l_ref[...] = alpha * l_ref[...] + jnp.sum(p, axis=0, keepdims=True)
        vt = jnp.concatenate([vst_ref[c * sub + i] for i in range(sub)], axis=1)
        acc_ref[...] = alpha * acc_ref[...] + dot(vt, p.astype(BF16))
        m_ref[...] = m_new

    @pl.when(n > 0)
    def _():
        scores(lst_ref[0], sa_ref)

    def pair(tp, carry):
        i = 2 * tp
        scores(lst_ref[i + 1], sb_ref)
        update(lst_ref[i], sa_ref)

        @pl.when(i + 1 < n)
        def _():
            scores(lst_ref[i + 2], sa_ref)
            update(lst_ref[i + 1], sb_ref)

        return carry

    lax.fori_loop(0, (n + 1) // 2, pair, 0)

    def gate(br):
        return jnp.concatenate([gt_ref[br * NSA_GROUP + g:br * NSA_GROUP + g + 1, :]
                                for g in range(NSA_GROUP)], axis=1)

    o = gate(0) * oc_ref[...] + gate(1) * (acc_ref[...] / l_ref[...]) + gate(2) * o_w
    o = o * lax.rsqrt(jnp.mean(o * o, axis=0, keepdims=True) + EPS) * go_ref[...]
    for g in range(NSA_GROUP):
        sl = slice(g * NSA_TQ, (g + 1) * NSA_TQ)
        o_ref[:, g * NSA_DV:(g + 1) * NSA_DV] = o[:, sl].T.astype(o_ref.dtype)


def _nsa_attend(flags, qt_arr, ks, vst, kw, vwt, negsel, near, oc, gates_t, slopes, g_nsa_out, batch, seq):
    hk = NSA_KV_HEADS
    nqt = seq // NSA_TQ
    ns = seq // SEL_BLOCK
    per_tile = lambda shape: pl.BlockSpec((None, None, None) + shape, lambda b, k, i: (b, k, i, 0, 0))
    tok = lambda n: pl.BlockSpec((None, None, n, NSA_DK), lambda b, k, i: (b, k, 0, 0))
    tr = lambda n: pl.BlockSpec((None, None, n, NSA_DV, NSA_TQ), lambda b, k, i: (b, k, 0, 0, 0))
    wpad = WINDOW // NSA_TQ
    kw = jnp.pad(kw, ((0, 0), (0, 0), (WINDOW, 0), (0, 0)))
    vwt = jnp.pad(vwt, ((0, 0), (0, 0), (wpad, 0), (0, 0), (0, 0)))
    return pl.pallas_call(
        _nsa_attn_kernel,
        name="nsa_attend",
        grid=(batch, hk, nqt),
        in_specs=[
            pl.BlockSpec((None, None, None, 1, ns), lambda b, k, i: (b, k, i, 0, 0),
                         memory_space=pltpu.SMEM),
            per_tile((NSA_DK, NSA_NQ)),
            tok(seq), tr(nqt), tok(seq + WINDOW), tr(nqt + wpad),
            per_tile((ns, NSA_TQ)),
            per_tile((FEAT // 2, NSA_TQ)),
            per_tile((NSA_DV, NSA_NQ)),
            pl.BlockSpec((None, GATE_ROWS, NSA_TQ), lambda b, k, i: (b, k, i)),
            pl.BlockSpec((None, 1, NSA_NQ), lambda b, k, i: (k, 0, 0)),
            pl.BlockSpec((NSA_DV, 1), lambda b, k, i: (0, 0)),
            pl.BlockSpec((SLC_CHUNK, FEAT), lambda b, k, i: (0, 0)),
            pl.BlockSpec((WIN_KEYS, FEAT), lambda b, k, i: (0, 0)),
            pl.BlockSpec((2 * NSA_TQ, FEAT), lambda b, k, i: (0, 0)),
        ],
        out_specs=pl.BlockSpec((NSA_TQ, NSA_GROUP * NSA_DV), lambda b, k, i: (b * nqt + i, k)),
        out_shape=jax.ShapeDtypeStruct((batch * seq, D_NSA_OUT), BF16),
        scratch_shapes=[pltpu.VMEM((1, NSA_NQ), F32), pltpu.VMEM((1, NSA_NQ), F32),
                        pltpu.VMEM((NSA_DV, NSA_NQ), F32),
                        pltpu.VMEM((SLC_CHUNK, NSA_NQ), F32), pltpu.VMEM((SLC_CHUNK, NSA_NQ), F32),
                        pltpu.SMEM((seq // SLC_CHUNK + 1,), jnp.int32)],
        compiler_params=_cparams(("parallel", "parallel", "arbitrary"), 48),
    )(flags, qt_arr, ks, vst, kw, vwt, negsel, near, oc, gates_t, slopes, g_nsa_out.reshape(NSA_DV, 1),
      _key_features(SLC_CHUNK, "slc"), _key_features(WIN_KEYS, "win"), _key_features(2 * NSA_TQ, "own"))


def _nsa(proj, b_gate, g_q, g_kc, g_ks, g_kw, pe_k, pe_v, w_ck1, w_ck2, w_cv1, w_cv2, g_nsa_out,
         batch, seq):
    assert seq % (CMP_CHUNK * CMP_STRIDE) == 0 and seq >= WIN_KEYS
    qt_arr, k_tok, v_tok, ks, vst, kw, vwt, gates_t = _nsa_prep(proj, g_q, g_ks, g_kw, b_gate, batch, seq)
    kc, vct = _compress(k_tok, v_tok, pe_k, pe_v, w_ck1, w_ck2, w_cv1, w_cv2, g_kc)
    negsel, flags, oc, near, slopes = _nsa_select(qt_arr, kc, vct, batch, seq)
    return _nsa_attend(flags, qt_arr, ks, vst, kw, vwt, negsel, near, oc, gates_t, slopes, g_nsa_out,
                       batch, seq)


def kernel(x, mem, g_mix, w_in, b_nsa_gate, g_q, g_kc, g_ks, g_kw, pe_k, pe_v, w_ck1, w_ck2,
           w_cv1, w_cv2, g_nsa_out, w_gk2, b_gk, g_gla_out, w_out, g_cross, g_mem, w_cq, w_ck,
           w_cv, g_cq, g_ck, w_co, g_ffn, w_gu, w_down):
    batch, seq, _ = x.shape
    x2d = x.reshape(batch * seq, D_MODEL)
    l = 0
    proj = _in_proj(x2d, g_mix[l], _pack_w_in(w_in[l]))
    o_gla = _gla(proj, w_gk2[l], b_gk[l], g_gla_out[l], batch, seq)
    o_nsa = _nsa(proj, b_nsa_gate[l], g_q[l], g_kc[l], g_ks[l], g_kw[l], pe_k[l], pe_v[l],
                 w_ck1[l], w_ck2[l], w_cv1[l], w_cv2[l], g_nsa_out[l], batch, seq)
    ck, cv = _mem_kv(mem, g_mem[l], w_ck[l], w_cv[l], g_ck[l])
    x2d = _out_cross(x2d, o_nsa, o_gla, w_out[l], g_cross[l], w_cq[l], g_cq[l], ck, cv,
                     w_co[l], seq)
    x2d = _ffn(x2d, g_ffn[l], w_gu[l], w_down[l])
    return x2d.reshape(batch, seq, D_MODEL)
```

```python
import functools

import numpy as np
import jax
import jax.numpy as jnp
from jax import lax
from jax.experimental import pallas as pl
from jax.experimental.pallas import tpu as pltpu

F32 = jnp.float32
BF16 = jnp.bfloat16

D_MODEL = 2048
MEM_LEN = 256
EPS = 1e-6
NSA_HEADS = 8
NSA_KV_HEADS = 2
NSA_GROUP = NSA_HEADS // NSA_KV_HEADS
NSA_DK = 128
NSA_DV = 128
CMP_LEN = 32
CMP_STRIDE = 16
CMP_HIDDEN = 256
SEL_BLOCK = 64
SEL_TOP = 16
WINDOW = 512
GLA_HEADS = 4
GLA_DK = 128
GLA_DV = 256
GLA_GATE_RANK = 16
GLA_GATE_NORM = 16.0
GLA_CHUNK = 64
MEM_HEADS = 4
MEM_DH = 128
D_FF = -(-8 * D_MODEL // (3 * 256)) * 256
D_NSA_OUT = NSA_HEADS * NSA_DV
D_GLA_OUT = GLA_HEADS * GLA_DV

LANES = 128
MXU_DIM = 256
VMEM_BYTES_V7X = 64 * 1024 * 1024

COL_Q = 0
COL_KV = COL_Q + NSA_HEADS * NSA_DK
COL_QL = COL_KV + 6 * NSA_KV_HEADS * NSA_DK
COL_KL = COL_QL + GLA_HEADS * GLA_DK
COL_VL = COL_KL + GLA_HEADS * GLA_DK
COL_RL = COL_VL + GLA_HEADS * GLA_DV
COL_MISC = COL_RL + GLA_HEADS * GLA_DV
MISC_GATE = 0
MISC_AL = 32
D_IN_PACKED = COL_MISC + LANES


def _cparams(semantics, vmem_mb):
    return pltpu.CompilerParams(dimension_semantics=semantics,
                                vmem_limit_bytes=vmem_mb * 1024 * 1024)


def _rms(u, g):
    return u * lax.rsqrt(jnp.mean(u * u, axis=-1, keepdims=True) + EPS) * g


def _inproj_kernel(x_ref, g_ref, w_ref, o_ref):
    h = _rms(x_ref[...], g_ref[...]).astype(BF16)
    o_ref[...] = jnp.dot(h, w_ref[...], preferred_element_type=F32)


def _in_proj(x2d, g_mix, w_packed, tm=256):
    n = x2d.shape[0]
    return pl.pallas_call(
        _inproj_kernel,
        name="in_proj",
        grid=(n // tm,),
        in_specs=[
            pl.BlockSpec((tm, D_MODEL), lambda i: (i, 0)),
            pl.BlockSpec((1, D_MODEL), lambda i: (0, 0)),
            pl.BlockSpec((D_MODEL, D_IN_PACKED), lambda i: (0, 0), pipeline_mode=pl.Buffered(1)),
        ],
        out_specs=pl.BlockSpec((tm, D_IN_PACKED), lambda i: (i, 0)),
        out_shape=jax.ShapeDtypeStruct((n, D_IN_PACKED), F32),
        compiler_params=_cparams(("parallel",), 48),
    )(x2d, g_mix.reshape(1, D_MODEL), w_packed)


GATE_ROWS = 16


def _gate_layout(u):
    pos = np.arange(MISC_AL)
    k, r = pos // GATE_ROWS, pos % GATE_ROWS
    br, g = r // NSA_GROUP, r % NSA_GROUP
    used = r < 3 * NSA_GROUP
    src = np.where(used, (k * NSA_GROUP + g) * 3 + br, 0)
    return jnp.where(jnp.asarray(used), jnp.take(u, jnp.asarray(src), axis=-1), 0.0)


def _pack_w_in(w_in):
    w_in = w_in.astype(BF16)
    sizes = (NSA_HEADS * NSA_DK,) + (NSA_KV_HEADS * NSA_DK,) * 6 + (3 * NSA_HEADS,) + (
        GLA_HEADS * GLA_DK, GLA_HEADS * GLA_DK, GLA_HEADS * GLA_DV, GLA_GATE_RANK, GLA_HEADS * GLA_DV)
    offs = np.concatenate([[0], np.cumsum(sizes)])
    seg = lambda i: w_in[:, offs[i]:offs[i + 1]]
    zeros = lambda n: jnp.zeros((D_MODEL, n), w_in.dtype)
    misc = jnp.concatenate([_gate_layout(seg(7)), seg(11),
                            zeros(LANES - MISC_AL - GLA_GATE_RANK)], axis=1)
    cols = [seg(0)] + [seg(i) for i in range(1, 7)] + [seg(8), seg(9), seg(10), seg(12), misc]
    return jnp.concatenate(cols, axis=1)


def _memkv_kernel(mem_ref, gm_ref, wk_ref, wv_ref, gk_ref, ck_ref, cv_ref):
    hm = _rms(mem_ref[...], gm_ref[...]).astype(BF16)
    k = jnp.dot(hm, wk_ref[...], preferred_element_type=F32)
    v = jnp.dot(hm, wv_ref[...], preferred_element_type=F32)
    for h in range(MEM_HEADS):
        sl = slice(h * MEM_DH, (h + 1) * MEM_DH)
        ck_ref[h] = _rms(k[:, sl], gk_ref[...]).astype(BF16)
        cv_ref[h] = v[:, sl].astype(BF16)


def _mem_kv(mem, g_mem, w_ck, w_cv, g_ck):
    b, m, _ = mem.shape
    dm = MEM_HEADS * MEM_DH
    out = jax.ShapeDtypeStruct((b, MEM_HEADS, m, MEM_DH), BF16)
    return pl.pallas_call(
        _memkv_kernel,
        name="mem_kv",
        grid=(b,),
        in_specs=[
            pl.BlockSpec((None, m, D_MODEL), lambda i: (i, 0, 0)),
            pl.BlockSpec((1, D_MODEL), lambda i: (0, 0)),
            pl.BlockSpec((D_MODEL, dm), lambda i: (0, 0)),
            pl.BlockSpec((D_MODEL, dm), lambda i: (0, 0)),
            pl.BlockSpec((1, MEM_DH), lambda i: (0, 0)),
        ],
        out_specs=[pl.BlockSpec((None, MEM_HEADS, m, MEM_DH), lambda i: (i, 0, 0, 0))] * 2,
        out_shape=[out, out],
        compiler_params=_cparams(("parallel",), 32),
    )(mem, g_mem.reshape(1, D_MODEL), w_ck.astype(BF16), w_cv.astype(BF16), g_ck.reshape(1, MEM_DH))


def _outx_kernel(x_ref, nsa_ref, gla_ref, wo1_ref, wo2_ref, gc_ref, wcq_ref, gcq_ref,
                 ck_ref, cv_ref, wco_ref, o_ref):
    x1 = (x_ref[...]
          + jnp.dot(nsa_ref[...], wo1_ref[...], preferred_element_type=F32)
          + jnp.dot(gla_ref[...], wo2_ref[...], preferred_element_type=F32))
    hq = _rms(x1, gc_ref[...]).astype(BF16)
    cq = jnp.dot(hq, wcq_ref[...], preferred_element_type=F32)
    outs = []
    for h in range(MEM_HEADS):
        c = _rms(cq[:, h * MEM_DH:(h + 1) * MEM_DH], gcq_ref[...]) * (MEM_DH ** -0.5)
        s = lax.dot_general(c.astype(BF16), ck_ref[h], (((1,), (1,)), ((), ())),
                            preferred_element_type=F32)
        p = jnp.exp(s - jnp.max(s, axis=-1, keepdims=True))
        p = p / jnp.sum(p, axis=-1, keepdims=True)
        outs.append(jnp.dot(p.astype(BF16), cv_ref[h], preferred_element_type=F32))
    oc = jnp.concatenate(outs, axis=-1).astype(BF16)
    o_ref[...] = x1 + jnp.dot(oc, wco_ref[...], preferred_element_type=F32)


def _out_cross(x2d, o_nsa, o_gla, w_out, g_cross, w_cq, g_cq, ck, cv, w_co, seq, tm=512):
    n = x2d.shape[0]
    tiles_per_batch = seq // tm
    dm = MEM_HEADS * MEM_DH
    m = ck.shape[2]
    full = lambda shape: pl.BlockSpec(shape, lambda i: (0,) * len(shape))
    kv_spec = pl.BlockSpec((None, MEM_HEADS, m, MEM_DH), lambda i: (i // tiles_per_batch, 0, 0, 0))
    w_out = w_out.astype(BF16)
    return pl.pallas_call(
        _outx_kernel,
        name="out_cross",
        grid=(n // tm,),
        in_specs=[
            pl.BlockSpec((tm, D_MODEL), lambda i: (i, 0)),
            pl.BlockSpec((tm, D_NSA_OUT), lambda i: (i, 0)),
            pl.BlockSpec((tm, D_GLA_OUT), lambda i: (i, 0)),
            full((D_NSA_OUT, D_MODEL)),
            full((D_GLA_OUT, D_MODEL)),
            full((1, D_MODEL)),
            full((D_MODEL, dm)),
            full((1, MEM_DH)),
            kv_spec,
            kv_spec,
            full((dm, D_MODEL)),
        ],
        out_specs=pl.BlockSpec((tm, D_MODEL), lambda i: (i, 0)),
        out_shape=jax.ShapeDtypeStruct((n, D_MODEL), F32),
        compiler_params=_cparams(("parallel",), 56),
    )(x2d, o_nsa, o_gla, w_out[:D_NSA_OUT], w_out[D_NSA_OUT:], g_cross.reshape(1, D_MODEL),
      w_cq.astype(BF16), g_cq.reshape(1, MEM_DH), ck, cv, w_co.astype(BF16))


def _ffn_kernel(x_ref, g_ref, wg_ref, wu_ref, wd_ref, o_ref, h_ref):
    @pl.when(pl.program_id(1) == 0)
    def _():
        x = x_ref[...]
        h_ref[...] = _rms(x, g_ref[...]).astype(BF16)
        o_ref[...] = x

    h = h_ref[...]
    g = jnp.dot(h, wg_ref[...], preferred_element_type=F32)
    u = jnp.dot(h, wu_ref[...], preferred_element_type=F32)
    a = (g * jax.nn.sigmoid(g) * u).astype(BF16)
    o_ref[...] += jnp.dot(a, wd_ref[...], preferred_element_type=F32)


def _ffn(x2d, g_ffn, w_gu, w_down, tm=512, tf=512):
    n = x2d.shape[0]
    w_gu = w_gu.astype(BF16)
    return pl.pallas_call(
        _ffn_kernel,
        name="ffn",
        grid=(n // tm, D_FF // tf),
        in_specs=[
            pl.BlockSpec((tm, D_MODEL), lambda i, j: (i, 0)),
            pl.BlockSpec((1, D_MODEL), lambda i, j: (0, 0)),
            pl.BlockSpec((D_MODEL, tf), lambda i, j: (0, j)),
            pl.BlockSpec((D_MODEL, tf), lambda i, j: (0, D_FF // tf + j)),
            pl.BlockSpec((tf, D_MODEL), lambda i, j: (j, 0)),
        ],
        out_specs=pl.BlockSpec((tm, D_MODEL), lambda i, j: (i, 0)),
        out_shape=jax.ShapeDtypeStruct((n, D_MODEL), F32),
        scratch_shapes=[pltpu.VMEM((tm, D_MODEL), BF16)],
        compiler_params=_cparams(("parallel", "arbitrary"), 48),
    )(x2d, g_ffn.reshape(1, D_MODEL), w_gu, w_gu, w_down.astype(BF16))


def _split3(x):
    hi = x.astype(BF16)
    r = x - hi.astype(F32)
    mid = r.astype(BF16)
    lo = (r - mid.astype(F32)).astype(BF16)
    return hi, mid, lo


def _dot_exact_lhs(a_bf16, x):
    return sum(jnp.dot(a_bf16, p, preferred_element_type=F32) for p in _split3(x))


def _dot_split(a, w):
    a_hi, a_lo, _ = _split3(a)
    w_hi, w_lo, _ = _split3(w)
    d = lambda p, q: jnp.dot(p, q, preferred_element_type=F32)
    return d(a_hi, w_hi) + d(a_hi, w_lo) + d(a_lo, w_hi)


_NT = (((1,), (1,)), ((), ()))
_TN = (((0,), (0,)), ((), ()))
GLA_DIRECT = 8


def _gla_kernel(ql_ref, kl_ref, vl_ref, rl_ref, misc_ref, w2_ref, bg_ref, go_ref, band_ref,
                o_ref, st_ref):
    c_len = GLA_CHUNK
    tc = ql_ref.shape[0]
    n_chunks = tc // c_len

    @pl.when(pl.program_id(2) == 0)
    def _():
        st_ref[...] = jnp.zeros_like(st_ref)

    x = _dot_split(misc_ref[...], w2_ref[...]) + bg_ref[...]
    la = (jnp.minimum(x, 0.0) - jnp.log(1.0 + jnp.exp(-jnp.abs(x)))) * (1.0 / GLA_GATE_NORM)
    pos = lax.broadcasted_iota(jnp.int32, (tc, 1), 0) & (c_len - 1)
    b = la
    sh = 1
    while sh < c_len:
        b = b + jnp.where(pos >= sh, pltpu.roll(b, sh, 0), 0.0)
        sh *= 2
    q = ql_ref[...] * (GLA_DK ** -0.5)
    k = kl_ref[...]
    b3 = b.reshape(n_chunks, c_len, GLA_DK)

    def chunk_row(r, n):
        return jnp.broadcast_to(b3[:, r:r + 1, :], (n_chunks, n, GLA_DK))

    level_q, level_k, level_s = [], [], []
    s = c_len
    while s > GLA_DIRECT:
        half = s // 2
        ref = jnp.concatenate([chunk_row(m0 + half - 1, s) for m0 in range(0, c_len, s)],
                              axis=1).reshape(tc, GLA_DK)
        second = (pos & (s - 1)) >= half
        level_q.append(jnp.where(second, q * jnp.exp(jnp.minimum(b - ref, 0.0)), 0.0).astype(BF16))
        level_k.append(jnp.where(second, 0.0, k * jnp.exp(jnp.minimum(ref - b, 0.0))).astype(BF16))
        level_s.append(s)
        s = half
    sublanes = 8
    group = lambda u: u.reshape(tc // sublanes, sublanes, GLA_DK)
    band = jnp.zeros((tc, c_len), F32)
    for d in range(GLA_DIRECT):
        k_sh = k if d == 0 else pltpu.roll(group(k), d, 1).reshape(tc, GLA_DK)
        b_sh = b if d == 0 else pltpu.roll(group(b), d, 1).reshape(tc, GLA_DK)
        cd = jnp.sum(q * k_sh * jnp.exp(jnp.minimum(b - b_sh, 0.0)), axis=-1, keepdims=True)
        band = band + cd * band_ref[d]
    b_end = chunk_row(c_len - 1, c_len).reshape(tc, GLA_DK)
    q_dec = (q * jnp.exp(b)).astype(BF16)
    k_dec = (k * jnp.exp(b_end - b)).astype(BF16)
    s_dec = jnp.exp(b_end)
    v = vl_ref[...].astype(BF16)
    row = lax.broadcasted_iota(jnp.int32, (c_len, 1), 0)
    col = lax.broadcasted_iota(jnp.int32, (1, c_len), 1)

    st = st_ref[...]
    for c in range(n_chunks):
        sl = slice(c * c_len, (c + 1) * c_len)
        attn = band[sl]
        for qs, ks, s in zip(level_q, level_k, level_s):
            a_s = lax.dot_general(qs[sl], ks[sl], _NT, preferred_element_type=F32)
            attn = attn + (a_s if s == c_len else jnp.where((row // s) == (col // s), a_s, 0.0))
        o = (jnp.dot(attn.astype(BF16), v[sl], preferred_element_type=F32)
             + lax.dot_general(q_dec[sl], st.astype(BF16), _NT, preferred_element_type=F32))
        st = (st * s_dec[c * c_len:c * c_len + 1]
              + lax.dot_general(v[sl], k_dec[sl], _TN, preferred_element_type=F32))
        r = rl_ref[sl, :]
        o_ref[sl, :] = (_rms(o, go_ref[...]) * (r * jax.nn.sigmoid(r))).astype(o_ref.dtype)
    st_ref[...] = st


def _gla(proj, w_gk2, b_gk, g_gla_out, batch, seq, tc=512):
    n = proj.shape[0]
    nt = seq // tc
    w2 = jnp.zeros((LANES, GLA_HEADS * GLA_DK), F32).at[MISC_AL:MISC_AL + GLA_GATE_RANK].set(w_gk2)
    i_pos = np.arange(tc)[:, None] % GLA_CHUNK
    d_off = np.arange(GLA_DIRECT)[:, None, None]
    band = ((np.arange(GLA_CHUNK)[None, :] == i_pos - d_off) & (i_pos % GLA_DIRECT >= d_off))
    band = jnp.asarray(band, F32)
    rows = lambda b, h, i: b * nt + i
    return pl.pallas_call(
        _gla_kernel,
        name="gla",
        grid=(batch, GLA_HEADS, nt),
        in_specs=[
            pl.BlockSpec((tc, GLA_DK), lambda b, h, i: (rows(b, h, i), COL_QL // GLA_DK + h)),
            pl.BlockSpec((tc, GLA_DK), lambda b, h, i: (rows(b, h, i), COL_KL // GLA_DK + h)),
            pl.BlockSpec((tc, GLA_DV), lambda b, h, i: (rows(b, h, i), COL_VL // GLA_DV + h)),
            pl.BlockSpec((tc, GLA_DV), lambda b, h, i: (rows(b, h, i), COL_RL // GLA_DV + h)),
            pl.BlockSpec((tc, LANES), lambda b, h, i: (rows(b, h, i), COL_MISC // LANES)),
            pl.BlockSpec((LANES, GLA_DK), lambda b, h, i: (0, h)),
            pl.BlockSpec((1, GLA_DK), lambda b, h, i: (0, h)),
            pl.BlockSpec((1, GLA_DV), lambda b, h, i: (0, 0)),
            pl.BlockSpec((GLA_DIRECT, tc, GLA_CHUNK), lambda b, h, i: (0, 0, 0)),
        ],
        out_specs=pl.BlockSpec((tc, GLA_DV), lambda b, h, i: (rows(b, h, i), h)),
        out_shape=jax.ShapeDtypeStruct((n, D_GLA_OUT), BF16),
        scratch_shapes=[pltpu.VMEM((GLA_DV, GLA_DK), F32)],
        compiler_params=_cparams(("parallel", "parallel", "arbitrary"), 32),
    )(proj, proj, proj, proj, proj, w2, b_gk.reshape(1, -1), g_gla_out.reshape(1, GLA_DV), band)


NSA_TQ = 256
NSA_NQ = NSA_TQ * NSA_GROUP
NEG = -1e30
SLC_CHUNK = 512
WIN_KEYS = WINDOW + NSA_TQ
CMP_CHUNK = 128
FEAT = 16


def _nsa_prep_kernel(q_ref, cmp_ref, slc_ref, win_ref, misc_ref, gq_ref, gks_ref, gkw_ref, bg_ref,
                     qt_ref, kcm_ref, vcm_ref, ks_ref, vst_ref, kw_ref, vwt_ref, gt_ref):
    q = q_ref[...]
    for k in range(NSA_KV_HEADS):
        for g in range(NSA_GROUP):
            h = k * NSA_GROUP + g
            qn = _rms(q[:, h * NSA_DK:(h + 1) * NSA_DK], gq_ref[...]) * (NSA_DK ** -0.5)
            qt_ref[k, :, g * NSA_TQ:(g + 1) * NSA_TQ] = qn.T.astype(BF16)
        ksl = slice(k * NSA_DK, (k + 1) * NSA_DK)
        vsl = slice((NSA_KV_HEADS + k) * NSA_DK, (NSA_KV_HEADS + k + 1) * NSA_DK)
        kcm_ref[k] = cmp_ref[:, ksl]
        vcm_ref[k] = cmp_ref[:, vsl]
        ks_ref[k] = _rms(slc_ref[:, ksl], gks_ref[...]).astype(BF16)
        vst_ref[k] = slc_ref[:, vsl].T.astype(BF16)
        kw_ref[k] = _rms(win_ref[:, ksl], gkw_ref[...]).astype(BF16)
        vwt_ref[k] = win_ref[:, vsl].T.astype(BF16)
    gates = jax.nn.sigmoid(misc_ref[...] + bg_ref[...])
    gt_ref[...] = gates.T[:NSA_KV_HEADS * GATE_ROWS, :]


def _nsa_prep(proj, g_q, g_ks, g_kw, b_gate, batch, seq):
    nqt = seq // NSA_TQ
    hk = NSA_KV_HEADS
    pair = 2 * hk * NSA_DK
    rows = lambda b, i: b * nqt + i
    vec = lambda n: pl.BlockSpec((1, n), lambda b, i: (0, 0))
    bias = jnp.zeros((1, LANES), F32).at[0, :MISC_AL].set(_gate_layout(b_gate))
    tok = lambda dt: jax.ShapeDtypeStruct((batch, hk, seq, NSA_DK), dt)
    tr = jax.ShapeDtypeStruct((batch, hk, nqt, NSA_DV, NSA_TQ), BF16)
    tok_spec = pl.BlockSpec((None, hk, NSA_TQ, NSA_DK), lambda b, i: (b, 0, i, 0))
    tr_spec = pl.BlockSpec((None, hk, None, NSA_DV, NSA_TQ), lambda b, i: (b, 0, i, 0, 0))
    return pl.pallas_call(
        _nsa_prep_kernel,
        name="nsa_prep",
        grid=(batch, nqt),
        in_specs=[
            pl.BlockSpec((NSA_TQ, NSA_HEADS * NSA_DK), lambda b, i: (rows(b, i), 0)),
            pl.BlockSpec((NSA_TQ, pair), lambda b, i: (rows(b, i), COL_KV // pair)),
            pl.BlockSpec((NSA_TQ, pair), lambda b, i: (rows(b, i), COL_KV // pair + 1)),
            pl.BlockSpec((NSA_TQ, pair), lambda b, i: (rows(b, i), COL_KV // pair + 2)),
            pl.BlockSpec((NSA_TQ, LANES), lambda b, i: (rows(b, i), COL_MISC // LANES)),
            vec(NSA_DK), vec(NSA_DK), vec(NSA_DK), vec(LANES),
        ],
        out_specs=[
            pl.BlockSpec((None, hk, None, NSA_DK, NSA_NQ), lambda b, i: (b, 0, i, 0, 0)),
            tok_spec, tok_spec, tok_spec, tr_spec, tok_spec, tr_spec,
            pl.BlockSpec((None, hk * GATE_ROWS, NSA_TQ), lambda b, i: (b, 0, i)),
        ],
        out_shape=[
            jax.ShapeDtypeStruct((batch, hk, nqt, NSA_DK, NSA_NQ), BF16),
            tok(F32), tok(F32), tok(BF16), tr, tok(BF16), tr,
            jax.ShapeDtypeStruct((batch, hk * GATE_ROWS, seq), F32),
        ],
        compiler_params=_cparams(("parallel", "parallel"), 32),
    )(proj, proj, proj, proj, proj, g_q.reshape(1, -1), g_ks.reshape(1, -1), g_kw.reshape(1, -1), bias)


def _compress_kernel(uk_ref, uv_ref, pek_ref, pev_ref, wk1_ref, wk2_ref, wv1_ref, wv2t_ref, gk_ref,
                     kc_ref, vct_ref):
    def hidden(u_ref, pe_ref, w1_ref):
        n = u_ref.shape[0] // CMP_STRIDE
        toks = [u_ref[pl.ds(l, n, stride=CMP_STRIDE), :] for l in range(CMP_STRIDE)]

        def half(h):
            rows = [(toks[l] + pe_ref[h * CMP_STRIDE + l:h * CMP_STRIDE + l + 1, :]).astype(BF16)
                    for l in range(CMP_STRIDE)]
            return jnp.dot(jnp.concatenate(rows, axis=1), w1_ref[h], preferred_element_type=F32)

        hid = half(0) + pltpu.roll(half(1), n - 1, 0)
        return (hid * jax.nn.sigmoid(hid)).astype(BF16)

    kc = jnp.dot(hidden(uk_ref, pek_ref, wk1_ref), wk2_ref[...], preferred_element_type=F32)
    kc_ref[...] = _rms(kc, gk_ref[...]).astype(BF16)
    vct = lax.dot_general(wv2t_ref[...], hidden(uv_ref, pev_ref, wv1_ref), _NT,
                          preferred_element_type=F32).astype(BF16)
    for c in range(vct_ref.shape[0]):
        vct_ref[c] = vct[:, c * CMP_CHUNK:(c + 1) * CMP_CHUNK]


def _compress(k_tok, v_tok, pe_k, pe_v, w_ck1, w_ck2, w_cv1, w_cv2, g_kc):
    batch, hk, seq, d = k_tok.shape
    nc = seq // CMP_STRIDE
    flat = CMP_STRIDE * d
    halves = CMP_LEN // CMP_STRIDE
    u_spec = pl.BlockSpec((None, None, seq, d), lambda b, k: (b, k, 0, 0))
    full = lambda shape: pl.BlockSpec(shape, lambda b, k: (0,) * len(shape))
    return pl.pallas_call(
        _compress_kernel,
        name="nsa_compress",
        grid=(batch, hk),
        in_specs=[u_spec, u_spec, full((CMP_LEN, d)), full((CMP_LEN, d)),
                  full((halves, flat, CMP_HIDDEN)), full((CMP_HIDDEN, d)),
                  full((halves, flat, CMP_HIDDEN)), full((d, CMP_HIDDEN)), full((1, d))],
        out_specs=[pl.BlockSpec((None, None, nc, d), lambda b, k: (b, k, 0, 0)),
                   pl.BlockSpec((None, None, nc // CMP_CHUNK, d, CMP_CHUNK),
                                lambda b, k: (b, k, 0, 0, 0))],
        out_shape=[jax.ShapeDtypeStruct((batch, hk, nc, d), BF16),
                   jax.ShapeDtypeStruct((batch, hk, nc // CMP_CHUNK, d, CMP_CHUNK), BF16)],
        compiler_params=_cparams(("parallel", "parallel"), 48),
    )(k_tok, v_tok, pe_k, pe_v,
      w_ck1.astype(BF16).reshape(halves, flat, CMP_HIDDEN), w_ck2.astype(BF16),
      w_cv1.astype(BF16).reshape(halves, flat, CMP_HIDDEN), w_cv2.T.astype(BF16), g_kc.reshape(1, d))


def _query_rows(qt):
    lane = lax.broadcasted_iota(jnp.int32, (1, NSA_NQ), 1)
    t = qt * NSA_TQ + (lane & (NSA_TQ - 1))
    return t


def _topk_mask(imp, cur, n_top):
    ns, nt = imp.shape
    j = lax.broadcasted_iota(jnp.int32, (ns, nt), 0)
    forced = (j == 0) | (j == cur) | (j == cur - 1)
    visible = j <= cur
    work = jnp.where(visible & jnp.logical_not(forced), imp, -1.0)

    def extract(_, work):
        m = jnp.max(work, axis=0, keepdims=True)
        cand = (work == m) & (m > -0.5)
        idx = jnp.min(jnp.where(cand, j, ns), axis=0, keepdims=True)
        return jnp.where(j == idx, -2.0, work)

    work = lax.fori_loop(0, n_top - 3, extract, work)
    return (forced & visible) | (work == -2.0)


def _feature_rows(rows):
    n = rows[0].shape[1]
    r = lax.broadcasted_iota(jnp.int32, (FEAT // 2, n), 0)
    out = jnp.zeros((FEAT // 2, n), F32)
    for i, row in enumerate(rows):
        out = jnp.where(r == i, row, out)
    return out


def _aug_rhs(q, mask8, bias8):
    return jnp.concatenate([q, jnp.concatenate([mask8, bias8], axis=0).astype(BF16)], axis=0)


def _key_features(n, kind):
    a = np.arange(n)
    f = np.zeros((n, FEAT), np.float32)
    if kind == "cmp":
        f[:, 8], f[:, 9], f[:, 10], f[:, 11] = a % CMP_CHUNK, a // CMP_CHUNK, 1.0, 1.0
    else:
        f[:, 8], f[:, 9], f[:, 10] = a % SEL_BLOCK, a // SEL_BLOCK, 1.0
    if kind == "slc":
        f[a, a // SEL_BLOCK] = 1.0
    if kind == "win":
        f[a, a // NSA_TQ] = 1.0
    if kind == "own":
        own_blk = a[:NSA_TQ] // SEL_BLOCK
        f[:NSA_TQ, 10] = 0.0
        f[NSA_TQ:, 9] = own_blk
        f[a[NSA_TQ:], own_blk] = 1.0
        f[a[:NSA_TQ], NSA_TQ // SEL_BLOCK + own_blk] = 1.0
    return jnp.asarray(f, BF16)


def _nsa_sel_kernel(qt_ref, kc_ref, kf_ref, vct_ref, mmat_ref, slope_ref, negsel_ref, flag_ref,
                    oc_ref, near_ref, *, n_top):
    qt = pl.program_id(2)
    t0 = qt * NSA_TQ
    t = _query_rows(qt)
    slope = slope_ref[...]
    q = qt_ref[...]
    zeros8 = jnp.zeros((FEAT // 2, NSA_NQ), F32)
    per_tile = NSA_TQ // CMP_STRIDE
    n_chunks = (per_tile * qt + per_tile + CMP_CHUNK - 1) // CMP_CHUNK
    span = CMP_CHUNK * CMP_STRIDE
    rhs = _aug_rhs(q, zeros8, _feature_rows([CMP_STRIDE * slope, span * slope,
                                             (0.5 * (CMP_LEN - 1)) * slope, (-t0).astype(F32) * slope]))

    def branch(n):
        rows = n * CMP_CHUNK
        edge = rows - min(n, 2) * CMP_CHUNK

        def run():
            lhs = jnp.concatenate([kc_ref[:rows, :], kf_ref[:rows, :]], axis=1)
            s = jnp.dot(lhs, rhs, preferred_element_type=F32)
            ci = edge + lax.broadcasted_iota(jnp.int32, (rows - edge, 1), 0)
            tail = jnp.where(t >= ci * CMP_STRIDE + (CMP_LEN - 1), s[edge:], NEG)
            s = tail if edge == 0 else jnp.concatenate([s[:edge], tail], axis=0)
            m = jnp.maximum(jnp.max(s, axis=0, keepdims=True), 0.1 * NEG)
            p = jnp.exp(s - m)
            l = jnp.sum(p, axis=0, keepdims=True)
            inv = 1.0 / jnp.where(l > 0.0, l, 1.0)
            vt = jnp.concatenate([vct_ref[c] for c in range(n)], axis=1)
            oc = jnp.dot(vt, p.astype(BF16), preferred_element_type=F32) * inv
            p = p * inv
            psum = sum(p[:, g * NSA_TQ:(g + 1) * NSA_TQ] for g in range(NSA_GROUP))
            return oc, _dot_exact_lhs(mmat_ref[:, :rows], psum)

        return run

    oc, imp = lax.switch(n_chunks - 1, [branch(n + 1) for n in range(kc_ref.shape[0] // CMP_CHUNK)])
    oc_ref[...] = oc
    cur = lax.shift_right_logical(t[:, :NSA_TQ], 6)
    sel = _topk_mask(imp, cur, n_top)
    j = lax.broadcasted_iota(jnp.int32, sel.shape, 0)
    per_own = NSA_TQ // SEL_BLOCK
    past = sel & (j < per_own * qt)
    negsel = jnp.where(past, 0.0, NEG)
    negsel_ref[...] = negsel
    own = [jnp.max(jnp.where(sel & (j == per_own * qt + r), 1.0, 0.0), axis=0, keepdims=True)
           for r in range(per_own)]
    near_ref[...] = _feature_rows([negsel[r:r + 1, :] for r in range(per_own)]
                                  + [jnp.where(o > 0.5, 0.0, NEG) for o in own])
    listed = (past & (j >= NSA_TQ // SEL_BLOCK)).astype(F32)
    per_chunk = SLC_CHUNK // SEL_BLOCK
    taken = jnp.max(listed.reshape(listed.shape[0] // per_chunk, per_chunk, NSA_TQ), axis=1)
    flag_ref[...] = jnp.max(taken, axis=1, keepdims=True).astype(jnp.int32)


def _nsa_select(qt_arr, kc, vct, batch, seq):
    hk = NSA_KV_HEADS
    nqt = seq // NSA_TQ
    nc = seq // CMP_STRIDE
    ns = seq // SEL_BLOCK
    n_cmp = nc - (CMP_LEN // CMP_STRIDE - 1)
    n_top = min(SEL_TOP, ns)
    ratio, lead = SEL_BLOCK // CMP_STRIDE, CMP_LEN // CMP_STRIDE - 1
    mm = np.zeros((ns, nc), np.float32)
    for r in range(ratio + lead):
        st = CMP_STRIDE * (r - lead)
        ov = (min(st + CMP_LEN, SEL_BLOCK) - max(st, 0)) / CMP_STRIDE
        for jb in range(ns):
            i = jb * ratio + r - lead
            if 0 <= i < n_cmp:
                mm[jb, i] += ov
    slopes = jnp.exp2(-8.0 * jnp.arange(1, NSA_HEADS + 1, dtype=F32) / NSA_HEADS)
    slopes = jnp.repeat(slopes.reshape(hk, 1, NSA_GROUP), NSA_TQ, axis=2)
    negsel, flags, oc, near = pl.pallas_call(
        functools.partial(_nsa_sel_kernel, n_top=n_top),
        name="nsa_select",
        grid=(batch, hk, nqt),
        in_specs=[
            pl.BlockSpec((None, None, None, NSA_DK, NSA_NQ), lambda b, k, i: (b, k, i, 0, 0)),
            pl.BlockSpec((None, None, nc, NSA_DK), lambda b, k, i: (b, k, 0, 0)),
            pl.BlockSpec((nc, FEAT), lambda b, k, i: (0, 0)),
            pl.BlockSpec((None, None, nc // CMP_CHUNK, NSA_DV, CMP_CHUNK), lambda b, k, i: (b, k, 0, 0, 0)),
            pl.BlockSpec((ns, nc), lambda b, k, i: (0, 0)),
            pl.BlockSpec((None, 1, NSA_NQ), lambda b, k, i: (k, 0, 0)),
        ],
        out_specs=[
            pl.BlockSpec((None, None, None, ns, NSA_TQ), lambda b, k, i: (b, k, i, 0, 0)),
            pl.BlockSpec((None, None, None, seq // SLC_CHUNK, 1), lambda b, k, i: (b, k, i, 0, 0)),
            pl.BlockSpec((None, None, None, NSA_DV, NSA_NQ), lambda b, k, i: (b, k, i, 0, 0)),
            pl.BlockSpec((None, None, None, FEAT // 2, NSA_TQ), lambda b, k, i: (b, k, i, 0, 0)),
        ],
        out_shape=[
            jax.ShapeDtypeStruct((batch, hk, nqt, ns, NSA_TQ), F32),
            jax.ShapeDtypeStruct((batch, hk, nqt, seq // SLC_CHUNK, 1), jnp.int32),
            jax.ShapeDtypeStruct((batch, hk, nqt, NSA_DV, NSA_NQ), F32),
            jax.ShapeDtypeStruct((batch, hk, nqt, FEAT // 2, NSA_TQ), F32),
        ],
        compiler_params=_cparams(("parallel", "parallel", "parallel"), 48),
    )(qt_arr, kc, _key_features(nc, "cmp"), vct, jnp.asarray(mm, BF16), slopes)
    return negsel, flags, oc, near, slopes


def _nsa_attn_kernel(flag_ref, qt_ref, ks_ref, vst_ref, kw_ref, vwt_ref, negsel_ref, near_ref, oc_ref, gt_ref,
                     slope_ref, go_ref, kfs_ref, kfw_ref, kfo_ref, o_ref, m_ref, l_ref, acc_ref,
                     sa_ref, sb_ref, lst_ref):
    qt = pl.program_id(2)
    t0 = qt * NSA_TQ
    t = _query_rows(qt)
    slope = slope_ref[...]
    q = qt_ref[...]
    blocks = SLC_CHUNK // SEL_BLOCK
    a = lax.broadcasted_iota(jnp.int32, (NSA_TQ, 1), 0)
    u = t - t0
    causal = jnp.where(a <= u, 0.0, NEG)
    too_old = jnp.where(a > u, 0.0, NEG)
    dot = functools.partial(jnp.dot, preferred_element_type=F32)
    per_group = lambda rows8: jnp.concatenate([rows8] * NSA_GROUP, axis=1)
    r8 = lax.broadcasted_iota(jnp.int32, (FEAT // 2, NSA_TQ), 0)

    lst_ref[0] = 0

    def listing(c, n):
        lst_ref[n] = c
        return n + (flag_ref[c, 0] > 0).astype(jnp.int32)

    n = lax.fori_loop(0, (t0 + SLC_CHUNK - 1) // SLC_CHUNK, listing, 0)
    lst_ref[n] = lst_ref[jnp.maximum(n - 1, 0)]

    def scores(c, dst_ref):
        k0 = pl.multiple_of(c * SLC_CHUNK, SLC_CHUNK)
        neg8 = negsel_ref[pl.ds(pl.multiple_of(c * blocks, blocks), blocks), :]
        neg8 = jnp.where((r8 < NSA_TQ // SEL_BLOCK) & (c == 0), NEG, neg8)
        rhs = _aug_rhs(q, per_group(neg8),
                       _feature_rows([slope, SEL_BLOCK * slope, (k0 - t0).astype(F32) * slope]))
        lhs = jnp.concatenate([ks_ref[pl.ds(k0, SLC_CHUNK), :], kfs_ref[...]], axis=1)
        dst_ref[...] = dot(lhs, rhs)

    scores(lst_ref[0], sa_ref)

    lhs = jnp.concatenate(
        [jnp.concatenate([ks_ref[pl.ds(pl.multiple_of(t0, NSA_TQ), NSA_TQ), :], ks_ref[:NSA_TQ, :]], axis=0),
         kfo_ref[...]], axis=1)
    s = dot(lhs, _aug_rhs(q, per_group(near_ref[...]),
                          _feature_rows([slope, SEL_BLOCK * slope, (-t0).astype(F32) * slope])))
    s = jnp.concatenate([s[:NSA_TQ] + causal, s[NSA_TQ:]], axis=0)
    m = jnp.max(s, axis=0, keepdims=True)
    p = jnp.exp(s - m)
    m_ref[...] = m
    l_ref[...] = jnp.sum(p, axis=0, keepdims=True)
    acc_ref[...] = dot(jnp.concatenate([vst_ref[qt], vst_ref[0]], axis=1), p.astype(BF16))

    wsub = WIN_KEYS // NSA_TQ
    before_start = jnp.where(r8 < WINDOW // NSA_TQ - qt, NEG, 0.0)
    lhs = jnp.concatenate([kw_ref[pl.ds(pl.multiple_of(t0, NSA_TQ), WIN_KEYS), :], kfw_ref[...]], axis=1)
    sw = dot(lhs, _aug_rhs(q, per_group(before_start),
                           _feature_rows([slope, SEL_BLOCK * slope, (-float(WINDOW)) * slope])))
    sw = jnp.concatenate([sw[:NSA_TQ] + too_old, sw[NSA_TQ:WINDOW], sw[WINDOW:] + causal], axis=0)
    pw = jnp.exp(sw - jnp.max(sw, axis=0, keepdims=True))
    vt = jnp.concatenate([vwt_ref[qt + i] for i in range(wsub)], axis=1)
    o_w = dot(vt, pw.astype(BF16)) / jnp.sum(pw, axis=0, keepdims=True)

    sub = SLC_CHUNK // NSA_TQ

    def update(c, src_ref):
        s = src_ref[...]
        m_old = m_ref[...]
        m_new = jnp.maximum(m_old, jnp.max(s, axis=0, keepdims=True))
        p = jnp.exp(s - m_new)
        alpha = jnp.exp(m_old - m_new)
        l_ref[...] = alpha * l_ref[...] + jnp.sum(p, axis=0, keepdims=True)
        vt = jnp.concatenate([vst_ref[c * sub + i] for i in range(sub)], axis=1)
        acc_ref[...] = alpha * acc_ref[...] + dot(vt, p.astype(BF16))
        m_ref[...] = m_new

    def pair(tp, carry):
        i = 2 * tp
        scores(lst_ref[i + 1], sb_ref)
        update(lst_ref[i], sa_ref)

        @pl.when(i + 1 < n)
        def _():
            scores(lst_ref[i + 2], sa_ref)
            update(lst_ref[i + 1], sb_ref)

        return carry

    lax.fori_loop(0, (n + 1) // 2, pair, 0)

    def gate(br):
        return jnp.concatenate([gt_ref[br * NSA_GROUP + g:br * NSA_GROUP + g + 1, :]
                                for g in range(NSA_GROUP)], axis=1)

    o = gate(0) * oc_ref[...] + gate(1) * (acc_ref[...] / l_ref[...]) + gate(2) * o_w
    o = o * lax.rsqrt(jnp.mean(o * o, axis=0, keepdims=True) + EPS) * go_ref[...]
    for g in range(NSA_GROUP):
        sl = slice(g * NSA_TQ, (g + 1) * NSA_TQ)
        o_ref[:, g * NSA_DV:(g + 1) * NSA_DV] = o[:, sl].T.astype(o_ref.dtype)


def _nsa_attend(flags, qt_arr, ks, vst, kw, vwt, negsel, near, oc, gates_t, slopes, g_nsa_out, batch, seq):
    hk = NSA_KV_HEADS
    nqt = seq // NSA_TQ
    ns = seq // SEL_BLOCK
    per_tile = lambda shape: pl.BlockSpec((None, None, None) + shape, lambda b, k, i: (b, k, i, 0, 0))
    tok = lambda n: pl.BlockSpec((None, None, n, NSA_DK), lambda b, k, i: (b, k, 0, 0))
    tr = lambda n: pl.BlockSpec((None, None, n, NSA_DV, NSA_TQ), lambda b, k, i: (b, k, 0, 0, 0))
    wpad = WINDOW // NSA_TQ
    kw = jnp.pad(kw, ((0, 0), (0, 0), (WINDOW, 0), (0, 0)))
    vwt = jnp.pad(vwt, ((0, 0), (0, 0), (wpad, 0), (0, 0), (0, 0)))
    return pl.pallas_call(
        _nsa_attn_kernel,
        name="nsa_attend",
        grid=(batch, hk, nqt),
        in_specs=[
            pl.BlockSpec((None, None, None, seq // SLC_CHUNK, 1), lambda b, k, i: (b, k, i, 0, 0),
                         memory_space=pltpu.SMEM),
            per_tile((NSA_DK, NSA_NQ)),
            tok(seq), tr(nqt), tok(seq + WINDOW), tr(nqt + wpad),
            per_tile((ns, NSA_TQ)),
            per_tile((FEAT // 2, NSA_TQ)),
            per_tile((NSA_DV, NSA_NQ)),
            pl.BlockSpec((None, GATE_ROWS, NSA_TQ), lambda b, k, i: (b, k, i)),
            pl.BlockSpec((None, 1, NSA_NQ), lambda b, k, i: (k, 0, 0)),
            pl.BlockSpec((NSA_DV, 1), lambda b, k, i: (0, 0)),
            pl.BlockSpec((SLC_CHUNK, FEAT), lambda b, k, i: (0, 0)),
            pl.BlockSpec((WIN_KEYS, FEAT), lambda b, k, i: (0, 0)),
            pl.BlockSpec((2 * NSA_TQ, FEAT), lambda b, k, i: (0, 0)),
        ],
        out_specs=pl.BlockSpec((NSA_TQ, NSA_GROUP * NSA_DV), lambda b, k, i: (b * nqt + i, k)),
        out_shape=jax.ShapeDtypeStruct((batch * seq, D_NSA_OUT), BF16),
        scratch_shapes=[pltpu.VMEM((1, NSA_NQ), F32), pltpu.VMEM((1, NSA_NQ), F32),
                        pltpu.VMEM((NSA_DV, NSA_NQ), F32),
                        pltpu.VMEM((SLC_CHUNK, NSA_NQ), F32), pltpu.VMEM((SLC_CHUNK, NSA_NQ), F32),
                        pltpu.SMEM((seq // SLC_CHUNK + 1,), jnp.int32)],
        compiler_params=_cparams(("parallel", "parallel", "arbitrary"), 48),
    )(flags, qt_arr, ks, vst, kw, vwt, negsel, near, oc, gates_t, slopes, g_nsa_out.reshape(NSA_DV, 1),
      _key_features(SLC_CHUNK, "slc"), _key_features(WIN_KEYS, "win"), _key_features(2 * NSA_TQ, "own"))


def _nsa(proj, b_gate, g_q, g_kc, g_ks, g_kw, pe_k, pe_v, w_ck1, w_ck2, w_cv1, w_cv2, g_nsa_out,
         batch, seq):
    assert seq % (CMP_CHUNK * CMP_STRIDE) == 0 and seq >= WIN_KEYS
    qt_arr, k_tok, v_tok, ks, vst, kw, vwt, gates_t = _nsa_prep(proj, g_q, g_ks, g_kw, b_gate, batch, seq)
    kc, vct = _compress(k_tok, v_tok, pe_k, pe_v, w_ck1, w_ck2, w_cv1, w_cv2, g_kc)
    negsel, flags, oc, near, slopes = _nsa_select(qt_arr, kc, vct, batch, seq)
    return _nsa_attend(flags, qt_arr, ks, vst, kw, vwt, negsel, near, oc, gates_t, slopes, g_nsa_out,
                       batch, seq)


def kernel(x, mem, g_mix, w_in, b_nsa_gate, g_q, g_kc, g_ks, g_kw, pe_k, pe_v, w_ck1, w_ck2,
           w_cv1, w_cv2, g_nsa_out, w_gk2, b_gk, g_gla_out, w_out, g_cross, g_mem, w_cq, w_ck,
           w_cv, g_cq, g_ck, w_co, g_ffn, w_gu, w_down):
    batch, seq, _ = x.shape
    x2d = x.reshape(batch * seq, D_MODEL)
    l = 0
    proj = _in_proj(x2d, g_mix[l], _pack_w_in(w_in[l]))
    o_gla = _gla(proj, w_gk2[l], b_gk[l], g_gla_out[l], batch, seq)
    o_nsa = _nsa(proj, b_nsa_gate[l], g_q[l], g_kc[l], g_ks[l], g_kw[l], pe_k[l], pe_v[l],
                 w_ck1[l], w_ck2[l], w_cv1[l], w_cv2[l], g_nsa_out[l], batch, seq)
    ck, cv = _mem_kv(mem, g_mem[l], w_ck[l], w_cv[l], g_ck[l])
    x2d = _out_cross(x2d, o_nsa, o_gla, w_out[l], g_cross[l], w_cq[l], g_cq[l], ck, cv,
                     w_co[l], seq)
    x2d = _ffn(x2d, g_ffn[l], w_gu[l], w_down[l])
    return x2d.reshape(batch, seq, D_MODEL)
```

```python
import functools

import numpy as np
import jax
import jax.numpy as jnp
from jax import lax
from jax.experimental import pallas as pl
from jax.experimental.pallas import tpu as pltpu

F32 = jnp.float32
BF16 = jnp.bfloat16

D_MODEL = 2048
EPS = 1e-6
NSA_HEADS = 8
NSA_KV_HEADS = 2
NSA_GROUP = NSA_HEADS // NSA_KV_HEADS
NSA_DK = 128
NSA_DV = 128
CMP_LEN = 32
CMP_STRIDE = 16
CMP_HIDDEN = 256
SEL_BLOCK = 64
SEL_TOP = 16
WINDOW = 512
GLA_HEADS = 4
GLA_DK = 128
GLA_DV = 256
GLA_GATE_RANK = 16
GLA_GATE_NORM = 16.0
GLA_CHUNK = 64
MEM_HEADS = 4
MEM_DH = 128
D_FF = -(-8 * D_MODEL // (3 * 256)) * 256
D_NSA_OUT = NSA_HEADS * NSA_DV
D_GLA_OUT = GLA_HEADS * GLA_DV

LANES = 128

COL_Q = 0
COL_KV = COL_Q + NSA_HEADS * NSA_DK
COL_QL = COL_KV + 6 * NSA_KV_HEADS * NSA_DK
COL_KL = COL_QL + GLA_HEADS * GLA_DK
COL_VL = COL_KL + GLA_HEADS * GLA_DK
COL_RL = COL_VL + GLA_HEADS * GLA_DV
COL_MISC = COL_RL + GLA_HEADS * GLA_DV
MISC_AL = 32
D_IN_PACKED = COL_MISC + LANES


def _cparams(semantics, vmem_mb):
    return pltpu.CompilerParams(dimension_semantics=semantics,
                                vmem_limit_bytes=vmem_mb * 1024 * 1024)


def _rms(u, g):
    return u * lax.rsqrt(jnp.mean(u * u, axis=-1, keepdims=True) + EPS) * g


def _inproj_kernel(x_ref, g_ref, w_ref, o_ref):
    h = _rms(x_ref[...], g_ref[...]).astype(BF16)
    o_ref[...] = jnp.dot(h, w_ref[...], preferred_element_type=F32)


def _in_proj(x2d, g_mix, w_packed, tm=256):
    n = x2d.shape[0]
    return pl.pallas_call(
        _inproj_kernel,
        name="in_proj",
        grid=(n // tm,),
        in_specs=[
            pl.BlockSpec((tm, D_MODEL), lambda i: (i, 0)),
            pl.BlockSpec((1, D_MODEL), lambda i: (0, 0)),
            pl.BlockSpec((D_MODEL, D_IN_PACKED), lambda i: (0, 0), pipeline_mode=pl.Buffered(1)),
        ],
        out_specs=pl.BlockSpec((tm, D_IN_PACKED), lambda i: (i, 0)),
        out_shape=jax.ShapeDtypeStruct((n, D_IN_PACKED), F32),
        compiler_params=_cparams(("parallel",), 48),
    )(x2d, g_mix.reshape(1, D_MODEL), w_packed)


GATE_ROWS = 16


def _gate_layout(u):
    pos = np.arange(MISC_AL)
    k, r = pos // GATE_ROWS, pos % GATE_ROWS
    br, g = r // NSA_GROUP, r % NSA_GROUP
    used = r < 3 * NSA_GROUP
    src = np.where(used, (k * NSA_GROUP + g) * 3 + br, 0)
    return jnp.where(jnp.asarray(used), jnp.take(u, jnp.asarray(src), axis=-1), 0.0)


def _pack_w_in(w_in):
    w_in = w_in.astype(BF16)
    sizes = (NSA_HEADS * NSA_DK,) + (NSA_KV_HEADS * NSA_DK,) * 6 + (3 * NSA_HEADS,) + (
        GLA_HEADS * GLA_DK, GLA_HEADS * GLA_DK, GLA_HEADS * GLA_DV, GLA_GATE_RANK, GLA_HEADS * GLA_DV)
    offs = np.concatenate([[0], np.cumsum(sizes)])
    seg = lambda i: w_in[:, offs[i]:offs[i + 1]]
    zeros = lambda n: jnp.zeros((D_MODEL, n), w_in.dtype)
    misc = jnp.concatenate([_gate_layout(seg(7)), seg(11),
                            zeros(LANES - MISC_AL - GLA_GATE_RANK)], axis=1)
    cols = [seg(0)] + [seg(i) for i in range(1, 7)] + [seg(8), seg(9), seg(10), seg(12), misc]
    return jnp.concatenate(cols, axis=1)


def _memkv_kernel(mem_ref, gm_ref, wk_ref, wv_ref, gk_ref, ck_ref, cv_ref):
    hm = _rms(mem_ref[...], gm_ref[...]).astype(BF16)
    k = jnp.dot(hm, wk_ref[...], preferred_element_type=F32)
    v = jnp.dot(hm, wv_ref[...], preferred_element_type=F32)
    for h in range(MEM_HEADS):
        sl = slice(h * MEM_DH, (h + 1) * MEM_DH)
        ck_ref[h] = _rms(k[:, sl], gk_ref[...]).astype(BF16)
        cv_ref[h] = v[:, sl].astype(BF16)


def _mem_kv(mem, g_mem, w_ck, w_cv, g_ck):
    b, m, _ = mem.shape
    dm = MEM_HEADS * MEM_DH
    out = jax.ShapeDtypeStruct((b, MEM_HEADS, m, MEM_DH), BF16)
    return pl.pallas_call(
        _memkv_kernel,
        name="mem_kv",
        grid=(b,),
        in_specs=[
            pl.BlockSpec((None, m, D_MODEL), lambda i: (i, 0, 0)),
            pl.BlockSpec((1, D_MODEL), lambda i: (0, 0)),
            pl.BlockSpec((D_MODEL, dm), lambda i: (0, 0)),
            pl.BlockSpec((D_MODEL, dm), lambda i: (0, 0)),
            pl.BlockSpec((1, MEM_DH), lambda i: (0, 0)),
        ],
        out_specs=[pl.BlockSpec((None, MEM_HEADS, m, MEM_DH), lambda i: (i, 0, 0, 0))] * 2,
        out_shape=[out, out],
        compiler_params=_cparams(("parallel",), 32),
    )(mem, g_mem.reshape(1, D_MODEL), w_ck.astype(BF16), w_cv.astype(BF16), g_ck.reshape(1, MEM_DH))


def _outx_kernel(x_ref, nsa_ref, gla_ref, wo1_ref, wo2_ref, gc_ref, wcq_ref, gcq_ref,
                 ck_ref, cv_ref, wco_ref, gf_ref, o_ref, hf_ref):
    x1 = (x_ref[...]
          + jnp.dot(nsa_ref[...], wo1_ref[...], preferred_element_type=F32)
          + jnp.dot(gla_ref[...], wo2_ref[...], preferred_element_type=F32))
    hq = _rms(x1, gc_ref[...]).astype(BF16)
    cq = jnp.dot(hq, wcq_ref[...], preferred_element_type=F32)
    outs = []
    for h in range(MEM_HEADS):
        c = _rms(cq[:, h * MEM_DH:(h + 1) * MEM_DH], gcq_ref[...]) * (MEM_DH ** -0.5)
        s = lax.dot_general(c.astype(BF16), ck_ref[h], (((1,), (1,)), ((), ())),
                            preferred_element_type=F32)
        p = jnp.exp(s - jnp.max(s, axis=-1, keepdims=True))
        p = p / jnp.sum(p, axis=-1, keepdims=True)
        outs.append(jnp.dot(p.astype(BF16), cv_ref[h], preferred_element_type=F32))
    oc = jnp.concatenate(outs, axis=-1).astype(BF16)
    x2 = x1 + jnp.dot(oc, wco_ref[...], preferred_element_type=F32)
    o_ref[...] = x2
    hf_ref[...] = _rms(x2, gf_ref[...]).astype(BF16)


def _out_cross(x2d, o_nsa, o_gla, w_out, g_cross, w_cq, g_cq, ck, cv, w_co, g_ffn, seq, tm=512):
    n = x2d.shape[0]
    tiles_per_batch = seq // tm
    dm = MEM_HEADS * MEM_DH
    m = ck.shape[2]
    full = lambda shape: pl.BlockSpec(shape, lambda i: (0,) * len(shape))
    kv_spec = pl.BlockSpec((None, MEM_HEADS, m, MEM_DH), lambda i: (i // tiles_per_batch, 0, 0, 0))
    w_out = w_out.astype(BF16)
    return pl.pallas_call(
        _outx_kernel,
        name="out_cross",
        grid=(n // tm,),
        in_specs=[
            pl.BlockSpec((tm, D_MODEL), lambda i: (i, 0)),
            pl.BlockSpec((tm, D_NSA_OUT), lambda i: (i, 0)),
            pl.BlockSpec((tm, D_GLA_OUT), lambda i: (i, 0)),
            pl.BlockSpec((D_NSA_OUT, D_MODEL), lambda i: (0, 0)),
            pl.BlockSpec((D_GLA_OUT, D_MODEL), lambda i: (D_NSA_OUT // D_GLA_OUT, 0)),
            full((1, D_MODEL)),
            full((D_MODEL, dm)),
            full((1, MEM_DH)),
            kv_spec,
            kv_spec,
            full((dm, D_MODEL)),
            full((1, D_MODEL)),
        ],
        out_specs=[pl.BlockSpec((tm, D_MODEL), lambda i: (i, 0))] * 2,
        out_shape=[jax.ShapeDtypeStruct((n, D_MODEL), F32), jax.ShapeDtypeStruct((n, D_MODEL), BF16)],
        compiler_params=_cparams(("parallel",), 56),
    )(x2d, o_nsa, o_gla, w_out, w_out, g_cross.reshape(1, D_MODEL), w_cq.astype(BF16),
      g_cq.reshape(1, MEM_DH), ck, cv, w_co.astype(BF16), g_ffn.reshape(1, D_MODEL))


def _ffn_kernel(x_ref, h_ref, wg_ref, wu_ref, wd_ref, o_ref):
    @pl.when(pl.program_id(1) == 0)
    def _():
        o_ref[...] = x_ref[...]

    h = h_ref[...]
    g = jnp.dot(h, wg_ref[...], preferred_element_type=F32)
    u = jnp.dot(h, wu_ref[...], preferred_element_type=F32)
    a = (g * jax.nn.sigmoid(g) * u).astype(BF16)
    o_ref[...] += jnp.dot(a, wd_ref[...], preferred_element_type=F32)


def _ffn(x2d, h2d, w_gu, w_down, tm=512, tf=512):
    n = x2d.shape[0]
    w_gu = w_gu.astype(BF16)
    return pl.pallas_call(
        _ffn_kernel,
        name="ffn",
        grid=(n // tm, D_FF // tf),
        in_specs=[
            pl.BlockSpec((tm, D_MODEL), lambda i, j: (i, 0)),
            pl.BlockSpec((tm, D_MODEL), lambda i, j: (i, 0)),
            pl.BlockSpec((D_MODEL, tf), lambda i, j: (0, j)),
            pl.BlockSpec((D_MODEL, tf), lambda i, j: (0, D_FF // tf + j)),
            pl.BlockSpec((tf, D_MODEL), lambda i, j: (j, 0)),
        ],
        out_specs=pl.BlockSpec((tm, D_MODEL), lambda i, j: (i, 0)),
        out_shape=jax.ShapeDtypeStruct((n, D_MODEL), F32),
        compiler_params=_cparams(("parallel", "arbitrary"), 48),
    )(x2d, h2d, w_gu, w_gu, w_down.astype(BF16))


def _split3(x):
    hi = x.astype(BF16)
    r = x - hi.astype(F32)
    mid = r.astype(BF16)
    lo = (r - mid.astype(F32)).astype(BF16)
    return hi, mid, lo


def _dot_exact_lhs(a_bf16, x):
    return sum(jnp.dot(a_bf16, p, preferred_element_type=F32) for p in _split3(x))


def _dot_split(a, w):
    a_hi, a_lo, _ = _split3(a)
    w_hi, w_lo, _ = _split3(w)
    d = lambda p, q: jnp.dot(p, q, preferred_element_type=F32)
    return d(a_hi, w_hi) + d(a_hi, w_lo) + d(a_lo, w_hi)


_NT = (((1,), (1,)), ((), ()))
_TN = (((0,), (0,)), ((), ()))
GLA_DIRECT = 8


def _gla_kernel(ql_ref, kl_ref, vl_ref, rl_ref, misc_ref, w2_ref, bg_ref, go_ref, band_ref,
                o_ref, st_ref):
    c_len = GLA_CHUNK
    tc = ql_ref.shape[0]
    n_chunks = tc // c_len

    @pl.when(pl.program_id(2) == 0)
    def _():
        st_ref[...] = jnp.zeros_like(st_ref)

    x = _dot_split(misc_ref[...], w2_ref[...]) + bg_ref[...]
    la = (jnp.minimum(x, 0.0) - jnp.log(1.0 + jnp.exp(-jnp.abs(x)))) * (1.0 / GLA_GATE_NORM)
    pos = lax.broadcasted_iota(jnp.int32, (tc, 1), 0) & (c_len - 1)
    b = la
    sh = 1
    while sh < c_len:
        b = b + jnp.where(pos >= sh, pltpu.roll(b, sh, 0), 0.0)
        sh *= 2
    q = ql_ref[...] * (GLA_DK ** -0.5)
    k = kl_ref[...]
    b3 = b.reshape(n_chunks, c_len, GLA_DK)

    def chunk_row(r, n):
        return jnp.broadcast_to(b3[:, r:r + 1, :], (n_chunks, n, GLA_DK))

    level_q, level_k, level_s = [], [], []
    s = c_len
    while s > GLA_DIRECT:
        half = s // 2
        ref = jnp.concatenate([chunk_row(m0 + half - 1, s) for m0 in range(0, c_len, s)],
                              axis=1).reshape(tc, GLA_DK)
        second = (pos & (s - 1)) >= half
        level_q.append(jnp.where(second, q * jnp.exp(jnp.minimum(b - ref, 0.0)), 0.0).astype(BF16))
        level_k.append(jnp.where(second, 0.0, k * jnp.exp(jnp.minimum(ref - b, 0.0))).astype(BF16))
        level_s.append(s)
        s = half
    sublanes = 8
    group = lambda u: u.reshape(tc // sublanes, sublanes, GLA_DK)
    band = jnp.zeros((tc, c_len), F32)
    for d in range(GLA_DIRECT):
        k_sh = k if d == 0 else pltpu.roll(group(k), d, 1).reshape(tc, GLA_DK)
        b_sh = b if d == 0 else pltpu.roll(group(b), d, 1).reshape(tc, GLA_DK)
        cd = jnp.sum(q * k_sh * jnp.exp(jnp.minimum(b - b_sh, 0.0)), axis=-1, keepdims=True)
        band = band + cd * band_ref[d]
    b_end = chunk_row(c_len - 1, c_len).reshape(tc, GLA_DK)
    q_dec = (q * jnp.exp(b)).astype(BF16)
    k_dec = (k * jnp.exp(b_end - b)).astype(BF16)
    s_dec = jnp.exp(b_end)
    v = vl_ref[...].astype(BF16)
    row = lax.broadcasted_iota(jnp.int32, (c_len, 1), 0)
    col = lax.broadcasted_iota(jnp.int32, (1, c_len), 1)

    st = st_ref[...]
    for c in range(n_chunks):
        sl = slice(c * c_len, (c + 1) * c_len)
        attn = band[sl]
        for qs, ks, s in zip(level_q, level_k, level_s):
            a_s = lax.dot_general(qs[sl], ks[sl], _NT, preferred_element_type=F32)
            attn = attn + (a_s if s == c_len else jnp.where((row // s) == (col // s), a_s, 0.0))
        o = (jnp.dot(attn.astype(BF16), v[sl], preferred_element_type=F32)
             + lax.dot_general(q_dec[sl], st.astype(BF16), _NT, preferred_element_type=F32))
        st = (st * s_dec[c * c_len:c * c_len + 1]
              + lax.dot_general(v[sl], k_dec[sl], _TN, preferred_element_type=F32))
        r = rl_ref[sl, :]
        o_ref[sl, :] = (_rms(o, go_ref[...]) * (r * jax.nn.sigmoid(r))).astype(o_ref.dtype)
    st_ref[...] = st


def _gla(proj, w_gk2, b_gk, g_gla_out, batch, seq, tc=512):
    n = proj.shape[0]
    nt = seq // tc
    w2 = jnp.zeros((LANES, GLA_HEADS * GLA_DK), F32).at[MISC_AL:MISC_AL + GLA_GATE_RANK].set(w_gk2)
    i_pos = np.arange(tc)[:, None] % GLA_CHUNK
    d_off = np.arange(GLA_DIRECT)[:, None, None]
    band = ((np.arange(GLA_CHUNK)[None, :] == i_pos - d_off) & (i_pos % GLA_DIRECT >= d_off))
    band = jnp.asarray(band, F32)
    rows = lambda b, h, i: b * nt + i
    return pl.pallas_call(
        _gla_kernel,
        name="gla",
        grid=(batch, GLA_HEADS, nt),
        in_specs=[
            pl.BlockSpec((tc, GLA_DK), lambda b, h, i: (rows(b, h, i), COL_QL // GLA_DK + h)),
            pl.BlockSpec((tc, GLA_DK), lambda b, h, i: (rows(b, h, i), COL_KL // GLA_DK + h)),
            pl.BlockSpec((tc, GLA_DV), lambda b, h, i: (rows(b, h, i), COL_VL // GLA_DV + h)),
            pl.BlockSpec((tc, GLA_DV), lambda b, h, i: (rows(b, h, i), COL_RL // GLA_DV + h)),
            pl.BlockSpec((tc, LANES), lambda b, h, i: (rows(b, h, i), COL_MISC // LANES)),
            pl.BlockSpec((LANES, GLA_DK), lambda b, h, i: (0, h)),
            pl.BlockSpec((1, GLA_DK), lambda b, h, i: (0, h)),
            pl.BlockSpec((1, GLA_DV), lambda b, h, i: (0, 0)),
            pl.BlockSpec((GLA_DIRECT, tc, GLA_CHUNK), lambda b, h, i: (0, 0, 0)),
        ],
        out_specs=pl.BlockSpec((tc, GLA_DV), lambda b, h, i: (rows(b, h, i), h)),
        out_shape=jax.ShapeDtypeStruct((n, D_GLA_OUT), BF16),
        scratch_shapes=[pltpu.VMEM((GLA_DV, GLA_DK), F32)],
        compiler_params=_cparams(("parallel", "parallel", "arbitrary"), 32),
    )(proj, proj, proj, proj, proj, w2, b_gk.reshape(1, -1), g_gla_out.reshape(1, GLA_DV), band)


NSA_TQ = 256
NSA_NQ = NSA_TQ * NSA_GROUP
NEG = -1e30
SLC_CHUNK = 512
WIN_KEYS = WINDOW + NSA_TQ
CMP_CHUNK = 128
FEAT = 16


def _nsa_prep_kernel(q_ref, cmp_ref, slc_ref, win_ref, misc_ref, gq_ref, gks_ref, gkw_ref, bg_ref,
                     qt_ref, kcm_ref, vcm_ref, ks_ref, vst_ref, kw_ref, vwt_ref, gt_ref):
    q = q_ref[...]
    for k in range(NSA_KV_HEADS):
        for g in range(NSA_GROUP):
            h = k * NSA_GROUP + g
            qn = _rms(q[:, h * NSA_DK:(h + 1) * NSA_DK], gq_ref[...]) * (NSA_DK ** -0.5 * LOG2E)
            qt_ref[k, :, g * NSA_TQ:(g + 1) * NSA_TQ] = qn.T.astype(BF16)
        ksl = slice(k * NSA_DK, (k + 1) * NSA_DK)
        vsl = slice((NSA_KV_HEADS + k) * NSA_DK, (NSA_KV_HEADS + k + 1) * NSA_DK)
        kcm_ref[k] = cmp_ref[:, ksl]
        vcm_ref[k] = cmp_ref[:, vsl]
        ks_ref[k] = _rms(slc_ref[:, ksl], gks_ref[...]).astype(BF16)
        vst_ref[k] = slc_ref[:, vsl].T.astype(BF16)
        kw_ref[k] = _rms(win_ref[:, ksl], gkw_ref[...]).astype(BF16)
        vwt_ref[k] = win_ref[:, vsl].T.astype(BF16)
    gates = jax.nn.sigmoid(misc_ref[...] + bg_ref[...])
    gt_ref[...] = gates.T[:NSA_KV_HEADS * GATE_ROWS, :]


def _nsa_prep(proj, g_q, g_ks, g_kw, b_gate, batch, seq):
    nqt = seq // NSA_TQ
    hk = NSA_KV_HEADS
    pair = 2 * hk * NSA_DK
    rows = lambda b, i: b * nqt + i
    vec = lambda n: pl.BlockSpec((1, n), lambda b, i: (0, 0))
    bias = jnp.zeros((1, LANES), F32).at[0, :MISC_AL].set(_gate_layout(b_gate))
    tok = lambda dt: jax.ShapeDtypeStruct((batch, hk, seq, NSA_DK), dt)
    tr = jax.ShapeDtypeStruct((batch, hk, nqt, NSA_DV, NSA_TQ), BF16)
    tok_spec = pl.BlockSpec((None, hk, NSA_TQ, NSA_DK), lambda b, i: (b, 0, i, 0))
    tr_spec = pl.BlockSpec((None, hk, None, NSA_DV, NSA_TQ), lambda b, i: (b, 0, i, 0, 0))
    return pl.pallas_call(
        _nsa_prep_kernel,
        name="nsa_prep",
        grid=(batch, nqt),
        in_specs=[
            pl.BlockSpec((NSA_TQ, NSA_HEADS * NSA_DK), lambda b, i: (rows(b, i), 0)),
            pl.BlockSpec((NSA_TQ, pair), lambda b, i: (rows(b, i), COL_KV // pair)),
            pl.BlockSpec((NSA_TQ, pair), lambda b, i: (rows(b, i), COL_KV // pair + 1)),
            pl.BlockSpec((NSA_TQ, pair), lambda b, i: (rows(b, i), COL_KV // pair + 2)),
            pl.BlockSpec((NSA_TQ, LANES), lambda b, i: (rows(b, i), COL_MISC // LANES)),
            vec(NSA_DK), vec(NSA_DK), vec(NSA_DK), vec(LANES),
        ],
        out_specs=[
            pl.BlockSpec((None, hk, None, NSA_DK, NSA_NQ), lambda b, i: (b, 0, i, 0, 0)),
            tok_spec, tok_spec, tok_spec, tr_spec, tok_spec, tr_spec,
            pl.BlockSpec((None, hk * GATE_ROWS, NSA_TQ), lambda b, i: (b, 0, i)),
        ],
        out_shape=[
            jax.ShapeDtypeStruct((batch, hk, nqt, NSA_DK, NSA_NQ), BF16),
            tok(F32), tok(F32), tok(BF16), tr, tok(BF16), tr,
            jax.ShapeDtypeStruct((batch, hk * GATE_ROWS, seq), F32),
        ],
        compiler_params=_cparams(("parallel", "parallel"), 32),
    )(proj, proj, proj, proj, proj, g_q.reshape(1, -1), g_ks.reshape(1, -1), g_kw.reshape(1, -1), bias)


def _compress_kernel(uk_ref, uv_ref, pek_ref, pev_ref, wk1_ref, wk2_ref, wv1_ref, wv2t_ref, gk_ref,
                     kc_ref, vct_ref):
    def hidden(u_ref, pe_ref, w1_ref):
        n = u_ref.shape[0] // CMP_STRIDE
        toks = [u_ref[pl.ds(l, n, stride=CMP_STRIDE), :] for l in range(CMP_STRIDE)]

        def half(h):
            rows = [(toks[l] + pe_ref[h * CMP_STRIDE + l:h * CMP_STRIDE + l + 1, :]).astype(BF16)
                    for l in range(CMP_STRIDE)]
            return jnp.dot(jnp.concatenate(rows, axis=1), w1_ref[h], preferred_element_type=F32)

        hid = half(0) + pltpu.roll(half(1), n - 1, 0)
        return (hid * jax.nn.sigmoid(hid)).astype(BF16)

    kc = jnp.dot(hidden(uk_ref, pek_ref, wk1_ref), wk2_ref[...], preferred_element_type=F32)
    kc_ref[...] = _rms(kc, gk_ref[...]).astype(BF16)
    vct = lax.dot_general(wv2t_ref[...], hidden(uv_ref, pev_ref, wv1_ref), _NT,
                          preferred_element_type=F32).astype(BF16)
    for c in range(vct_ref.shape[0]):
        vct_ref[c] = vct[:, c * CMP_CHUNK:(c + 1) * CMP_CHUNK]


def _compress(k_tok, v_tok, pe_k, pe_v, w_ck1, w_ck2, w_cv1, w_cv2, g_kc):
    batch, hk, seq, d = k_tok.shape
    nc = seq // CMP_STRIDE
    flat = CMP_STRIDE * d
    halves = CMP_LEN // CMP_STRIDE
    u_spec = pl.BlockSpec((None, None, seq, d), lambda b, k: (b, k, 0, 0))
    full = lambda shape: pl.BlockSpec(shape, lambda b, k: (0,) * len(shape))
    return pl.pallas_call(
        _compress_kernel,
        name="nsa_compress",
        grid=(batch, hk),
        in_specs=[u_spec, u_spec, full((CMP_LEN, d)), full((CMP_LEN, d)),
                  full((halves, flat, CMP_HIDDEN)), full((CMP_HIDDEN, d)),
                  full((halves, flat, CMP_HIDDEN)), full((d, CMP_HIDDEN)), full((1, d))],
        out_specs=[pl.BlockSpec((None, None, nc, d), lambda b, k: (b, k, 0, 0)),
                   pl.BlockSpec((None, None, nc // CMP_CHUNK, d, CMP_CHUNK),
                                lambda b, k: (b, k, 0, 0, 0))],
        out_shape=[jax.ShapeDtypeStruct((batch, hk, nc, d), BF16),
                   jax.ShapeDtypeStruct((batch, hk, nc // CMP_CHUNK, d, CMP_CHUNK), BF16)],
        compiler_params=_cparams(("parallel", "parallel"), 48),
    )(k_tok, v_tok, pe_k, pe_v,
      w_ck1.astype(BF16).reshape(halves, flat, CMP_HIDDEN), w_ck2.astype(BF16),
      w_cv1.astype(BF16).reshape(halves, flat, CMP_HIDDEN), w_cv2.T.astype(BF16), g_kc.reshape(1, d))


def _query_rows(qt):
    lane = lax.broadcasted_iota(jnp.int32, (1, NSA_NQ), 1)
    t = qt * NSA_TQ + (lane & (NSA_TQ - 1))
    return t


def _topk_mask(imp, cur, n_top):
    ns, nt = imp.shape
    j = lax.broadcasted_iota(jnp.int32, (ns, nt), 0)
    forced = (j == 0) | (j == cur) | (j == cur - 1)
    visible = j <= cur
    work = jnp.where(visible & jnp.logical_not(forced), imp, -1.0)

    def extract(_, work):
        m = jnp.max(work, axis=0, keepdims=True)
        cand = (work == m) & (m > -0.5)
        idx = jnp.min(jnp.where(cand, j, ns), axis=0, keepdims=True)
        return jnp.where(j == idx, -2.0, work)

    work = lax.fori_loop(0, n_top - 3, extract, work)
    return (forced & visible) | (work == -2.0)


def _feature_rows(rows):
    n = rows[0].shape[1]
    r = lax.broadcasted_iota(jnp.int32, (FEAT // 2, n), 0)
    out = jnp.zeros((FEAT // 2, n), F32)
    for i, row in enumerate(rows):
        out = jnp.where(r == i, row, out)
    return out


def _aug_rhs(q, mask8, bias8):
    return jnp.concatenate([q, jnp.concatenate([mask8, bias8], axis=0).astype(BF16)], axis=0)


def _key_features(n, kind):
    a = np.arange(n)
    f = np.zeros((n, FEAT), np.float32)
    unit = CMP_CHUNK if kind == "cmp" else SEL_BLOCK
    if kind == "own":
        a = a % NSA_TQ
    f[:, 8:11] = (a % unit)[:, None]
    f[:, 11:14] = (a // unit)[:, None]
    rows = np.arange(n)
    if kind == "slc":
        f[rows, a // SEL_BLOCK] = 1.0
    if kind == "win":
        f[rows, a // NSA_TQ] = 1.0
    if kind == "own":
        f[rows, np.where(rows < NSA_TQ, NSA_TQ // SEL_BLOCK, 0) + a // SEL_BLOCK] = 1.0
    return jnp.asarray(f, BF16)


LOG2E = float(np.log2(np.e))


def _alibi_rows(slope, lo_unit, hi_unit):
    pieces = [p.astype(F32) for p in _split3(slope * LOG2E)]
    return _feature_rows([lo_unit * p for p in pieces] + [hi_unit * p for p in pieces])


def _nsa_sel_kernel(qt_ref, kc_ref, kf_ref, vct_ref, mmat_ref, slope_ref, negsel_ref, flag_ref,
                    oc_ref, near_ref, *, n_top):
    qt = pl.program_id(2)
    t = _query_rows(qt)
    slope = slope_ref[...]
    q = qt_ref[...]
    zeros8 = jnp.zeros((FEAT // 2, NSA_NQ), F32)
    per_tile = NSA_TQ // CMP_STRIDE
    n_chunks = (per_tile * qt + per_tile + CMP_CHUNK - 1) // CMP_CHUNK
    rhs = _aug_rhs(q, zeros8, _alibi_rows(slope, CMP_STRIDE, CMP_CHUNK * CMP_STRIDE))

    def branch(n):
        rows = n * CMP_CHUNK
        edge = rows - min(n, 2) * CMP_CHUNK

        def run():
            lhs = jnp.concatenate([kc_ref[:rows, :], kf_ref[:rows, :]], axis=1)
            s = jnp.dot(lhs, rhs, preferred_element_type=F32)
            ci = edge + lax.broadcasted_iota(jnp.int32, (rows - edge, 1), 0)
            tail = jnp.where(t >= ci * CMP_STRIDE + (CMP_LEN - 1), s[edge:], NEG)
            s = tail if edge == 0 else jnp.concatenate([s[:edge], tail], axis=0)
            m = jnp.maximum(jnp.max(s, axis=0, keepdims=True), 0.1 * NEG)
            p = jnp.exp2(s - m)
            l = jnp.sum(p, axis=0, keepdims=True)
            inv = 1.0 / jnp.where(l > 0.0, l, 1.0)
            vt = jnp.concatenate([vct_ref[c] for c in range(n)], axis=1)
            oc = jnp.dot(vt, p.astype(BF16), preferred_element_type=F32) * inv
            p = p * inv
            psum = sum(p[:, g * NSA_TQ:(g + 1) * NSA_TQ] for g in range(NSA_GROUP))
            return oc, _dot_exact_lhs(mmat_ref[:, :rows], psum)

        return run

    oc, imp = lax.switch(n_chunks - 1, [branch(n + 1) for n in range(kc_ref.shape[0] // CMP_CHUNK)])
    oc_ref[...] = oc
    cur = lax.shift_right_logical(t[:, :NSA_TQ], 6)
    sel = _topk_mask(imp, cur, n_top)
    j = lax.broadcasted_iota(jnp.int32, sel.shape, 0)
    per_own = NSA_TQ // SEL_BLOCK
    past = sel & (j < per_own * qt)
    negsel = jnp.where(past, 0.0, NEG)
    negsel_ref[...] = negsel
    own = [jnp.max(jnp.where(sel & (j == per_own * qt + r), 1.0, 0.0), axis=0, keepdims=True)
           for r in range(per_own)]
    near_ref[...] = _feature_rows([negsel[r:r + 1, :] for r in range(per_own)]
                                  + [jnp.where(o > 0.5, 0.0, NEG) for o in own])
    listed = (past & (j >= NSA_TQ // SEL_BLOCK)).astype(F32)
    per_chunk = SLC_CHUNK // SEL_BLOCK
    taken = jnp.max(listed.reshape(listed.shape[0] // per_chunk, per_chunk, NSA_TQ), axis=1)
    flag_ref[...] = jnp.max(taken, axis=1, keepdims=True).astype(jnp.int32)


def _nsa_select(qt_arr, kc, vct, batch, seq):
    hk = NSA_KV_HEADS
    nqt = seq // NSA_TQ
    nc = seq // CMP_STRIDE
    ns = seq // SEL_BLOCK
    n_cmp = nc - (CMP_LEN // CMP_STRIDE - 1)
    n_top = min(SEL_TOP, ns)
    ratio, lead = SEL_BLOCK // CMP_STRIDE, CMP_LEN // CMP_STRIDE - 1
    mm = np.zeros((ns, nc), np.float32)
    for r in range(ratio + lead):
        st = CMP_STRIDE * (r - lead)
        ov = (min(st + CMP_LEN, SEL_BLOCK) - max(st, 0)) / CMP_STRIDE
        for jb in range(ns):
            i = jb * ratio + r - lead
            if 0 <= i < n_cmp:
                mm[jb, i] += ov
    slopes = jnp.exp2(-8.0 * jnp.arange(1, NSA_HEADS + 1, dtype=F32) / NSA_HEADS)
    slopes = jnp.repeat(slopes.reshape(hk, 1, NSA_GROUP), NSA_TQ, axis=2)
    negsel, flags, oc, near = pl.pallas_call(
        functools.partial(_nsa_sel_kernel, n_top=n_top),
        name="nsa_select",
        grid=(batch, hk, nqt),
        in_specs=[
            pl.BlockSpec((None, None, None, NSA_DK, NSA_NQ), lambda b, k, i: (b, k, i, 0, 0)),
            pl.BlockSpec((None, None, nc, NSA_DK), lambda b, k, i: (b, k, 0, 0)),
            pl.BlockSpec((nc, FEAT), lambda b, k, i: (0, 0)),
            pl.BlockSpec((None, None, nc // CMP_CHUNK, NSA_DV, CMP_CHUNK), lambda b, k, i: (b, k, 0, 0, 0)),
            pl.BlockSpec((ns, nc), lambda b, k, i: (0, 0)),
            pl.BlockSpec((None, 1, NSA_NQ), lambda b, k, i: (k, 0, 0)),
        ],
        out_specs=[
            pl.BlockSpec((None, None, None, ns, NSA_TQ), lambda b, k, i: (b, k, i, 0, 0)),
            pl.BlockSpec((None, None, None, seq // SLC_CHUNK, 1), lambda b, k, i: (b, k, i, 0, 0)),
            pl.BlockSpec((None, None, None, NSA_DV, NSA_NQ), lambda b, k, i: (b, k, i, 0, 0)),
            pl.BlockSpec((None, None, None, FEAT // 2, NSA_TQ), lambda b, k, i: (b, k, i, 0, 0)),
        ],
        out_shape=[
            jax.ShapeDtypeStruct((batch, hk, nqt, ns, NSA_TQ), F32),
            jax.ShapeDtypeStruct((batch, hk, nqt, seq // SLC_CHUNK, 1), jnp.int32),
            jax.ShapeDtypeStruct((batch, hk, nqt, NSA_DV, NSA_NQ), F32),
            jax.ShapeDtypeStruct((batch, hk, nqt, FEAT // 2, NSA_TQ), F32),
        ],
        compiler_params=_cparams(("parallel", "parallel", "parallel"), 48),
    )(qt_arr, kc, _key_features(nc, "cmp"), vct, jnp.asarray(mm, BF16), slopes)
    return negsel, flags, oc, near, slopes


def _nsa_attn_kernel(flag_ref, qt_ref, ks_ref, vst_ref, kw_ref, vwt_ref, negsel_ref, near_ref, oc_ref, gt_ref,
                     slope_ref, go_ref, kfs_ref, kfw_ref, kfo_ref, o_ref, m_ref, l_ref, acc_ref,
                     sa_ref, sb_ref, lst_ref):
    qt = pl.program_id(2)
    t0 = qt * NSA_TQ
    t = _query_rows(qt)
    slope = slope_ref[...]
    q = qt_ref[...]
    blocks = SLC_CHUNK // SEL_BLOCK
    a = lax.broadcasted_iota(jnp.int32, (NSA_TQ, 1), 0)
    u = t - t0
    causal = jnp.where(a <= u, 0.0, NEG)
    too_old = jnp.where(a > u, 0.0, NEG)
    dot = functools.partial(jnp.dot, preferred_element_type=F32)
    per_group = lambda rows8: jnp.concatenate([rows8] * NSA_GROUP, axis=1)
    r8 = lax.broadcasted_iota(jnp.int32, (FEAT // 2, NSA_TQ), 0)
    slope2 = slope * LOG2E
    alibi8 = _alibi_rows(slope, 1, SEL_BLOCK)

    lst_ref[0] = 0

    def listing(c, n):
        lst_ref[n] = c
        return n + (flag_ref[c, 0] > 0).astype(jnp.int32)

    n = lax.fori_loop(0, (t0 + SLC_CHUNK - 1) // SLC_CHUNK, listing, 0)
    lst_ref[n] = lst_ref[jnp.maximum(n - 1, 0)]

    def scores(c, dst_ref):
        k0 = pl.multiple_of(c * SLC_CHUNK, SLC_CHUNK)
        neg8 = negsel_ref[pl.ds(pl.multiple_of(c * blocks, blocks), blocks), :]
        neg8 = jnp.where((r8 < NSA_TQ // SEL_BLOCK) & (c == 0), NEG, neg8)
        lhs = jnp.concatenate([ks_ref[pl.ds(k0, SLC_CHUNK), :], kfs_ref[...]], axis=1)
        dst_ref[...] = dot(lhs, _aug_rhs(q, per_group(neg8), alibi8))

    scores(lst_ref[0], sa_ref)

    lhs = jnp.concatenate(
        [jnp.concatenate([ks_ref[pl.ds(pl.multiple_of(t0, NSA_TQ), NSA_TQ), :], ks_ref[:NSA_TQ, :]], axis=0),
         kfo_ref[...]], axis=1)
    s = dot(lhs, _aug_rhs(q, per_group(near_ref[...]), alibi8))
    s_own, s_first = s[:NSA_TQ] + causal, s[NSA_TQ:]
    first_off = (-t0).astype(F32) * slope2
    m = jnp.maximum(jnp.max(s_own, axis=0, keepdims=True), jnp.max(s_first, axis=0, keepdims=True) + first_off)
    p = jnp.concatenate([jnp.exp2(s_own - m), jnp.exp2(s_first - (m - first_off))], axis=0)
    m_ref[...] = m
    l_ref[...] = jnp.sum(p, axis=0, keepdims=True)
    acc_ref[...] = dot(jnp.concatenate([vst_ref[qt], vst_ref[0]], axis=1), p.astype(BF16))

    wsub = WIN_KEYS // NSA_TQ
    before_start = jnp.where(r8 < WINDOW // NSA_TQ - qt, NEG, 0.0)
    lhs = jnp.concatenate([kw_ref[pl.ds(pl.multiple_of(t0, NSA_TQ), WIN_KEYS), :], kfw_ref[...]], axis=1)
    sw = dot(lhs, _aug_rhs(q, per_group(before_start), alibi8))
    sw = jnp.concatenate([sw[:NSA_TQ] + too_old, sw[NSA_TQ:WINDOW], sw[WINDOW:] + causal], axis=0)
    pw = jnp.exp2(sw - jnp.max(sw, axis=0, keepdims=True))
    vt = jnp.concatenate([vwt_ref[qt + i] for i in range(wsub)], axis=1)
    o_w = dot(vt, pw.astype(BF16)) / jnp.sum(pw, axis=0, keepdims=True)

    sub = SLC_CHUNK // NSA_TQ

    def update(c, src_ref):
        s = src_ref[...]
        off = (c * SLC_CHUNK - t0).astype(F32) * slope2
        m_old = m_ref[...]
        m_new = jnp.maximum(m_old, jnp.max(s, axis=0, keepdims=True) + off)
        p = jnp.exp2(s - (m_new - off))
        alpha = jnp.exp2(m_old - m_new)
        l_ref[...] = alpha * l_ref[...] + jnp.sum(p, axis=0, keepdims=True)
        vt = jnp.concatenate([vst_ref[c * sub + i] for i in range(sub)], axis=1)
        acc_ref[...] = alpha * acc_ref[...] + dot(vt, p.astype(BF16))
        m_ref[...] = m_new

    def pair(tp, carry):
        i = 2 * tp
        scores(lst_ref[i + 1], sb_ref)
        update(lst_ref[i], sa_ref)

        @pl.when(i + 1 < n)
        def _():
            scores(lst_ref[i + 2], sa_ref)
            update(lst_ref[i + 1], sb_ref)

        return carry

    lax.fori_loop(0, (n + 1) // 2, pair, 0)

    def gate(br):
        return jnp.concatenate([gt_ref[br * NSA_GROUP + g:br * NSA_GROUP + g + 1, :]
                                for g in range(NSA_GROUP)], axis=1)

    o = gate(0) * oc_ref[...] + gate(1) * (acc_ref[...] / l_ref[...]) + gate(2) * o_w
    o = o * lax.rsqrt(jnp.mean(o * o, axis=0, keepdims=True) + EPS) * go_ref[...]
    for g in range(NSA_GROUP):
        sl = slice(g * NSA_TQ, (g + 1) * NSA_TQ)
        o_ref[:, g * NSA_DV:(g + 1) * NSA_DV] = o[:, sl].T.astype(o_ref.dtype)


def _nsa_attend(flags, qt_arr, ks, vst, kw, vwt, negsel, near, oc, gates_t, slopes, g_nsa_out, batch, seq):
    hk = NSA_KV_HEADS
    nqt = seq // NSA_TQ
    ns = seq // SEL_BLOCK
    per_tile = lambda shape: pl.BlockSpec((None, None, None) + shape, lambda b, k, i: (b, k, i, 0, 0))
    tok = lambda n: pl.BlockSpec((None, None, n, NSA_DK), lambda b, k, i: (b, k, 0, 0))
    tr = lambda n: pl.BlockSpec((None, None, n, NSA_DV, NSA_TQ), lambda b, k, i: (b, k, 0, 0, 0))
    wpad = WINDOW // NSA_TQ
    kw = jnp.pad(kw, ((0, 0), (0, 0), (WINDOW, 0), (0, 0)))
    vwt = jnp.pad(vwt, ((0, 0), (0, 0), (wpad, 0), (0, 0), (0, 0)))
    return pl.pallas_call(
        _nsa_attn_kernel,
        name="nsa_attend",
        grid=(batch, hk, nqt),
        in_specs=[
            pl.BlockSpec((None, None, None, seq // SLC_CHUNK, 1), lambda b, k, i: (b, k, i, 0, 0),
                         memory_space=pltpu.SMEM),
            per_tile((NSA_DK, NSA_NQ)),
            tok(seq), tr(nqt), tok(seq + WINDOW), tr(nqt + wpad),
            per_tile((ns, NSA_TQ)),
            per_tile((FEAT // 2, NSA_TQ)),
            per_tile((NSA_DV, NSA_NQ)),
            pl.BlockSpec((None, GATE_ROWS, NSA_TQ), lambda b, k, i: (b, k, i)),
            pl.BlockSpec((None, 1, NSA_NQ), lambda b, k, i: (k, 0, 0)),
            pl.BlockSpec((NSA_DV, 1), lambda b, k, i: (0, 0)),
            pl.BlockSpec((SLC_CHUNK, FEAT), lambda b, k, i: (0, 0)),
            pl.BlockSpec((WIN_KEYS, FEAT), lambda b, k, i: (0, 0)),
            pl.BlockSpec((2 * NSA_TQ, FEAT), lambda b, k, i: (0, 0)),
        ],
        out_specs=pl.BlockSpec((NSA_TQ, NSA_GROUP * NSA_DV), lambda b, k, i: (b * nqt + i, k)),
        out_shape=jax.ShapeDtypeStruct((batch * seq, D_NSA_OUT), BF16),
        scratch_shapes=[pltpu.VMEM((1, NSA_NQ), F32), pltpu.VMEM((1, NSA_NQ), F32),
                        pltpu.VMEM((NSA_DV, NSA_NQ), F32),
                        pltpu.VMEM((SLC_CHUNK, NSA_NQ), F32), pltpu.VMEM((SLC_CHUNK, NSA_NQ), F32),
                        pltpu.SMEM((seq // SLC_CHUNK + 1,), jnp.int32)],
        compiler_params=_cparams(("parallel", "parallel", "arbitrary"), 48),
    )(flags, qt_arr, ks, vst, kw, vwt, negsel, near, oc, gates_t, slopes, g_nsa_out.reshape(NSA_DV, 1),
      _key_features(SLC_CHUNK, "slc"), _key_features(WIN_KEYS, "win"), _key_features(2 * NSA_TQ, "own"))


def _nsa(proj, b_gate, g_q, g_kc, g_ks, g_kw, pe_k, pe_v, w_ck1, w_ck2, w_cv1, w_cv2, g_nsa_out,
         batch, seq):
    assert seq % (CMP_CHUNK * CMP_STRIDE) == 0 and seq >= WIN_KEYS
    qt_arr, k_tok, v_tok, ks, vst, kw, vwt, gates_t = _nsa_prep(proj, g_q, g_ks, g_kw, b_gate, batch, seq)
    kc, vct = _compress(k_tok, v_tok, pe_k, pe_v, w_ck1, w_ck2, w_cv1, w_cv2, g_kc)
    negsel, flags, oc, near, slopes = _nsa_select(qt_arr, kc, vct, batch, seq)
    return _nsa_attend(flags, qt_arr, ks, vst, kw, vwt, negsel, near, oc, gates_t, slopes, g_nsa_out,
                       batch, seq)


def kernel(x, mem, g_mix, w_in, b_nsa_gate, g_q, g_kc, g_ks, g_kw, pe_k, pe_v, w_ck1, w_ck2,
           w_cv1, w_cv2, g_nsa_out, w_gk2, b_gk, g_gla_out, w_out, g_cross, g_mem, w_cq, w_ck,
           w_cv, g_cq, g_ck, w_co, g_ffn, w_gu, w_down):
    batch, seq, _ = x.shape
    x2d = x.reshape(batch * seq, D_MODEL)
    for l in range(w_in.shape[0]):
        proj = _in_proj(x2d, g_mix[l], _pack_w_in(w_in[l]))
        o_gla = _gla(proj, w_gk2[l], b_gk[l], g_gla_out[l], batch, seq)
        o_nsa = _nsa(proj, b_nsa_gate[l], g_q[l], g_kc[l], g_ks[l], g_kw[l], pe_k[l], pe_v[l],
                     w_ck1[l], w_ck2[l], w_cv1[l], w_cv2[l], g_nsa_out[l], batch, seq)
        ck, cv = _mem_kv(mem, g_mem[l], w_ck[l], w_cv[l], g_ck[l])
        x2d, h2d = _out_cross(x2d, o_nsa, o_gla, w_out[l], g_cross[l], w_cq[l], g_cq[l], ck, cv,
                              w_co[l], g_ffn[l], seq)
        x2d = _ffn(x2d, h2d, w_gu[l], w_down[l])
    return x2d.reshape(batch, seq, D_MODEL)
```

```python
import functools

import numpy as np
import jax
import jax.numpy as jnp
from jax import lax
from jax.experimental import pallas as pl
from jax.experimental.pallas import tpu as pltpu

F32 = jnp.float32
BF16 = jnp.bfloat16

D_MODEL = 2048
EPS = 1e-6
NSA_HEADS = 8
NSA_KV_HEADS = 2
NSA_GROUP = NSA_HEADS // NSA_KV_HEADS
NSA_DK = 128
NSA_DV = 128
CMP_LEN = 32
CMP_STRIDE = 16
CMP_HIDDEN = 256
SEL_BLOCK = 64
SEL_TOP = 16
WINDOW = 512
GLA_HEADS = 4
GLA_DK = 128
GLA_DV = 256
GLA_GATE_RANK = 16
GLA_GATE_NORM = 16.0
GLA_CHUNK = 64
MEM_HEADS = 4
MEM_DH = 128
D_FF = -(-8 * D_MODEL // (3 * 256)) * 256
D_NSA_OUT = NSA_HEADS * NSA_DV
D_GLA_OUT = GLA_HEADS * GLA_DV

LANES = 128

COL_Q = 0
COL_KV = COL_Q + NSA_HEADS * NSA_DK
COL_QL = COL_KV + 6 * NSA_KV_HEADS * NSA_DK
COL_KL = COL_QL + GLA_HEADS * GLA_DK
COL_VL = COL_KL + GLA_HEADS * GLA_DK
COL_RL = COL_VL + GLA_HEADS * GLA_DV
COL_MISC = COL_RL + GLA_HEADS * GLA_DV
MISC_AL = 32
D_IN_PACKED = COL_MISC + LANES


def _cparams(semantics, vmem_mb):
    return pltpu.CompilerParams(dimension_semantics=semantics,
                                vmem_limit_bytes=vmem_mb * 1024 * 1024)


def _rms(u, g):
    return u * lax.rsqrt(jnp.mean(u * u, axis=-1, keepdims=True) + EPS) * g


def _inproj_kernel(x_ref, g_ref, w_nsa_ref, w_rest_ref, o_ref):
    h = _rms(x_ref[...], g_ref[...]).astype(BF16)
    o_ref[:, :COL_QL] = jnp.dot(h, w_nsa_ref[...], preferred_element_type=F32)
    o_ref[:, COL_QL:] = jnp.dot(h, w_rest_ref[...], preferred_element_type=F32)


def _in_proj(x2d, g_mix, w_in, w_rest, tm=256):
    n = x2d.shape[0]
    resident = lambda cols: pl.BlockSpec((D_MODEL, cols), lambda i: (0, 0), pipeline_mode=pl.Buffered(1))
    return pl.pallas_call(
        _inproj_kernel,
        name="in_proj",
        grid=(n // tm,),
        in_specs=[
            pl.BlockSpec((tm, D_MODEL), lambda i: (i, 0)),
            pl.BlockSpec((1, D_MODEL), lambda i: (0, 0)),
            resident(COL_QL),
            resident(D_IN_PACKED - COL_QL),
        ],
        out_specs=pl.BlockSpec((tm, D_IN_PACKED), lambda i: (i, 0)),
        out_shape=jax.ShapeDtypeStruct((n, D_IN_PACKED), F32),
        compiler_params=_cparams(("parallel",), 48),
    )(x2d, g_mix.reshape(1, D_MODEL), w_in, w_rest)


GATE_ROWS = 16


def _gate_layout(u):
    pos = np.arange(MISC_AL)
    k, r = pos // GATE_ROWS, pos % GATE_ROWS
    br, g = r // NSA_GROUP, r % NSA_GROUP
    used = r < 3 * NSA_GROUP
    src = np.where(used, (k * NSA_GROUP + g) * 3 + br, 0)
    return jnp.where(jnp.asarray(used), jnp.take(u, jnp.asarray(src), axis=-1), 0.0)


def _pack_w_in(w_in):
    w_in = w_in.astype(BF16)
    sizes = (NSA_HEADS * NSA_DK,) + (NSA_KV_HEADS * NSA_DK,) * 6 + (3 * NSA_HEADS,) + (
        GLA_HEADS * GLA_DK, GLA_HEADS * GLA_DK, GLA_HEADS * GLA_DV, GLA_GATE_RANK, GLA_HEADS * GLA_DV)
    offs = np.concatenate([[0], np.cumsum(sizes)])
    assert offs[7] == COL_QL
    seg = lambda i: w_in[:, offs[i]:offs[i + 1]]
    zeros = lambda n: jnp.zeros((D_MODEL, n), w_in.dtype)
    misc = jnp.concatenate([_gate_layout(seg(7)), seg(11),
                            zeros(LANES - MISC_AL - GLA_GATE_RANK)], axis=1)
    return w_in, jnp.concatenate([seg(8), seg(9), seg(10), seg(12), misc], axis=1)


def _memkv_kernel(mem_ref, gm_ref, wk_ref, wv_ref, gk_ref, ck_ref, cv_ref):
    hm = _rms(mem_ref[...], gm_ref[...]).astype(BF16)
    k = jnp.dot(hm, wk_ref[...], preferred_element_type=F32)
    v = jnp.dot(hm, wv_ref[...], preferred_element_type=F32)
    for h in range(MEM_HEADS):
        sl = slice(h * MEM_DH, (h + 1) * MEM_DH)
        ck_ref[h] = _rms(k[:, sl], gk_ref[...]).astype(BF16)
        cv_ref[h] = v[:, sl].astype(BF16)


def _mem_kv(mem, g_mem, w_ck, w_cv, g_ck):
    b, m, _ = mem.shape
    dm = MEM_HEADS * MEM_DH
    out = jax.ShapeDtypeStruct((b, MEM_HEADS, m, MEM_DH), BF16)
    return pl.pallas_call(
        _memkv_kernel,
        name="mem_kv",
        grid=(b,),
        in_specs=[
            pl.BlockSpec((None, m, D_MODEL), lambda i: (i, 0, 0)),
            pl.BlockSpec((1, D_MODEL), lambda i: (0, 0)),
            pl.BlockSpec((D_MODEL, dm), lambda i: (0, 0)),
            pl.BlockSpec((D_MODEL, dm), lambda i: (0, 0)),
            pl.BlockSpec((1, MEM_DH), lambda i: (0, 0)),
        ],
        out_specs=[pl.BlockSpec((None, MEM_HEADS, m, MEM_DH), lambda i: (i, 0, 0, 0))] * 2,
        out_shape=[out, out],
        compiler_params=_cparams(("parallel",), 32),
    )(mem, g_mem.reshape(1, D_MODEL), w_ck.astype(BF16), w_cv.astype(BF16), g_ck.reshape(1, MEM_DH))


def _outx_kernel(x_ref, nsa_ref, gla_ref, wo1_ref, wo2_ref, gc_ref, wcq_ref, gcq_ref,
                 ck_ref, cv_ref, wco_ref, o_ref):
    x1 = (x_ref[...]
          + jnp.dot(nsa_ref[...], wo1_ref[...], preferred_element_type=F32)
          + jnp.dot(gla_ref[...], wo2_ref[...], preferred_element_type=F32))
    hq = _rms(x1, gc_ref[...]).astype(BF16)
    cq = jnp.dot(hq, wcq_ref[...], preferred_element_type=F32)
    outs = []
    for h in range(MEM_HEADS):
        c = _rms(cq[:, h * MEM_DH:(h + 1) * MEM_DH], gcq_ref[...]) * (MEM_DH ** -0.5)
        s = lax.dot_general(c.astype(BF16), ck_ref[h], (((1,), (1,)), ((), ())),
                            preferred_element_type=F32)
        p = jnp.exp(s - jnp.max(s, axis=-1, keepdims=True))
        p = p / jnp.sum(p, axis=-1, keepdims=True)
        outs.append(jnp.dot(p.astype(BF16), cv_ref[h], preferred_element_type=F32))
    oc = jnp.concatenate(outs, axis=-1).astype(BF16)
    o_ref[...] = x1 + jnp.dot(oc, wco_ref[...], preferred_element_type=F32)


def _out_cross(x2d, o_nsa, o_gla, w_out, g_cross, w_cq, g_cq, ck, cv, w_co, seq, tm=512):
    n = x2d.shape[0]
    tiles_per_batch = seq // tm
    dm = MEM_HEADS * MEM_DH
    m = ck.shape[2]
    full = lambda shape: pl.BlockSpec(shape, lambda i: (0,) * len(shape))
    kv_spec = pl.BlockSpec((None, MEM_HEADS, m, MEM_DH), lambda i: (i // tiles_per_batch, 0, 0, 0))
    w_out = w_out.astype(BF16)
    return pl.pallas_call(
        _outx_kernel,
        name="out_cross",
        grid=(n // tm,),
        in_specs=[
            pl.BlockSpec((tm, D_MODEL), lambda i: (i, 0)),
            pl.BlockSpec((tm, D_NSA_OUT), lambda i: (i, 0)),
            pl.BlockSpec((tm, D_GLA_OUT), lambda i: (i, 0)),
            pl.BlockSpec((D_NSA_OUT, D_MODEL), lambda i: (0, 0)),
            pl.BlockSpec((D_GLA_OUT, D_MODEL), lambda i: (D_NSA_OUT // D_GLA_OUT, 0)),
            full((1, D_MODEL)),
            full((D_MODEL, dm)),
            full((1, MEM_DH)),
            kv_spec,
            kv_spec,
            full((dm, D_MODEL)),
        ],
        out_specs=pl.BlockSpec((tm, D_MODEL), lambda i: (i, 0)),
        out_shape=jax.ShapeDtypeStruct((n, D_MODEL), F32),
        compiler_params=_cparams(("parallel",), 56),
    )(x2d, o_nsa, o_gla, w_out, w_out, g_cross.reshape(1, D_MODEL), w_cq.astype(BF16),
      g_cq.reshape(1, MEM_DH), ck, cv, w_co.astype(BF16))


def _ffn_kernel(x_ref, g_ref, wg_ref, wu_ref, wd_ref, o_ref, h_ref):
    @pl.when(pl.program_id(1) == 0)
    def _():
        x = x_ref[...]
        h_ref[...] = _rms(x, g_ref[...]).astype(BF16)
        o_ref[...] = x

    h = h_ref[...]
    g = jnp.dot(h, wg_ref[...], preferred_element_type=F32)
    u = jnp.dot(h, wu_ref[...], preferred_element_type=F32)
    a = (g * jax.nn.sigmoid(g) * u).astype(BF16)
    o_ref[...] += jnp.dot(a, wd_ref[...], preferred_element_type=F32)


def _ffn(x2d, g_ffn, w_gu, w_down, tm=512, tf=512):
    n = x2d.shape[0]
    w_gu = w_gu.astype(BF16)
    return pl.pallas_call(
        _ffn_kernel,
        name="ffn",
        grid=(n // tm, D_FF // tf),
        in_specs=[
            pl.BlockSpec((tm, D_MODEL), lambda i, j: (i, 0)),
            pl.BlockSpec((1, D_MODEL), lambda i, j: (0, 0)),
            pl.BlockSpec((D_MODEL, tf), lambda i, j: (0, j)),
            pl.BlockSpec((D_MODEL, tf), lambda i, j: (0, D_FF // tf + j)),
            pl.BlockSpec((tf, D_MODEL), lambda i, j: (j, 0)),
        ],
        out_specs=pl.BlockSpec((tm, D_MODEL), lambda i, j: (i, 0)),
        out_shape=jax.ShapeDtypeStruct((n, D_MODEL), F32),
        scratch_shapes=[pltpu.VMEM((tm, D_MODEL), BF16)],
        compiler_params=_cparams(("parallel", "arbitrary"), 48),
    )(x2d, g_ffn.reshape(1, D_MODEL), w_gu, w_gu, w_down.astype(BF16))


def _split3(x):
    hi = x.astype(BF16)
    r = x - hi.astype(F32)
    mid = r.astype(BF16)
    lo = (r - mid.astype(F32)).astype(BF16)
    return hi, mid, lo


def _dot_exact_lhs(a_bf16, x):
    return sum(jnp.dot(a_bf16, p, preferred_element_type=F32) for p in _split3(x))


def _dot_split(a, w):
    a_hi, a_lo, _ = _split3(a)
    w_hi, w_lo, _ = _split3(w)
    d = lambda p, q: jnp.dot(p, q, preferred_element_type=F32)
    return d(a_hi, w_hi) + d(a_hi, w_lo) + d(a_lo, w_hi)


_NT = (((1,), (1,)), ((), ()))
_TN = (((0,), (0,)), ((), ()))
GLA_DIRECT = 8


def _gla_kernel(ql_ref, kl_ref, vl_ref, rl_ref, misc_ref, w2_ref, bg_ref, go_ref, band_ref,
                o_ref, st_ref):
    c_len = GLA_CHUNK
    tc = ql_ref.shape[0]
    n_chunks = tc // c_len

    @pl.when(pl.program_id(2) == 0)
    def _():
        st_ref[...] = jnp.zeros_like(st_ref)

    x = _dot_split(misc_ref[...], w2_ref[...]) + bg_ref[...]
    la = (jnp.minimum(x, 0.0) - jnp.log(1.0 + jnp.exp(-jnp.abs(x)))) * (1.0 / GLA_GATE_NORM)
    pos = lax.broadcasted_iota(jnp.int32, (tc, 1), 0) & (c_len - 1)
    b = la
    sh = 1
    while sh < c_len:
        b = b + jnp.where(pos >= sh, pltpu.roll(b, sh, 0), 0.0)
        sh *= 2
    q = ql_ref[...] * (GLA_DK ** -0.5)
    k = kl_ref[...]
    b3 = b.reshape(n_chunks, c_len, GLA_DK)

    def chunk_row(r, n):
        return jnp.broadcast_to(b3[:, r:r + 1, :], (n_chunks, n, GLA_DK))

    level_q, level_k, level_s = [], [], []
    s = c_len
    while s > GLA_DIRECT:
        half = s // 2
        ref = jnp.concatenate([chunk_row(m0 + half - 1, s) for m0 in range(0, c_len, s)],
                              axis=1).reshape(tc, GLA_DK)
        second = (pos & (s - 1)) >= half
        level_q.append(jnp.where(second, q * jnp.exp(jnp.minimum(b - ref, 0.0)), 0.0).astype(BF16))
        level_k.append(jnp.where(second, 0.0, k * jnp.exp(jnp.minimum(ref - b, 0.0))).astype(BF16))
        level_s.append(s)
        s = half
    sublanes = 8
    group = lambda u: u.reshape(tc // sublanes, sublanes, GLA_DK)
    band = jnp.zeros((tc, c_len), F32)
    for d in range(GLA_DIRECT):
        k_sh = k if d == 0 else pltpu.roll(group(k), d, 1).reshape(tc, GLA_DK)
        b_sh = b if d == 0 else pltpu.roll(group(b), d, 1).reshape(tc, GLA_DK)
        cd = jnp.sum(q * k_sh * jnp.exp(jnp.minimum(b - b_sh, 0.0)), axis=-1, keepdims=True)
        band = band + cd * band_ref[d]
    b_end = chunk_row(c_len - 1, c_len).reshape(tc, GLA_DK)
    q_dec = (q * jnp.exp(b)).astype(BF16)
    k_dec = (k * jnp.exp(b_end - b)).astype(BF16)
    s_dec = jnp.exp(b_end)
    v = vl_ref[...].astype(BF16)
    row = lax.broadcasted_iota(jnp.int32, (c_len, 1), 0)
    col = lax.broadcasted_iota(jnp.int32, (1, c_len), 1)

    st = st_ref[...]
    for c in range(n_chunks):
        sl = slice(c * c_len, (c + 1) * c_len)
        attn = band[sl]
        for qs, ks, s in zip(level_q, level_k, level_s):
            a_s = lax.dot_general(qs[sl], ks[sl], _NT, preferred_element_type=F32)
            attn = attn + (a_s if s == c_len else jnp.where((row // s) == (col // s), a_s, 0.0))
        o = (jnp.dot(attn.astype(BF16), v[sl], preferred_element_type=F32)
             + lax.dot_general(q_dec[sl], st.astype(BF16), _NT, preferred_element_type=F32))
        st = (st * s_dec[c * c_len:c * c_len + 1]
              + lax.dot_general(v[sl], k_dec[sl], _TN, preferred_element_type=F32))
        r = rl_ref[sl, :]
        o_ref[sl, :] = (_rms(o, go_ref[...]) * (r * jax.nn.sigmoid(r))).astype(o_ref.dtype)
    st_ref[...] = st


def _gla(proj, w_gk2, b_gk, g_gla_out, batch, seq, tc=2048):
    n = proj.shape[0]
    nt = seq // tc
    w2 = jnp.zeros((LANES, GLA_HEADS * GLA_DK), F32).at[MISC_AL:MISC_AL + GLA_GATE_RANK].set(w_gk2)
    i_pos = np.arange(tc)[:, None] % GLA_CHUNK
    d_off = np.arange(GLA_DIRECT)[:, None, None]
    band = ((np.arange(GLA_CHUNK)[None, :] == i_pos - d_off) & (i_pos % GLA_DIRECT >= d_off))
    band = jnp.asarray(band, F32)
    rows = lambda b, h, i: b * nt + i
    return pl.pallas_call(
        _gla_kernel,
        name="gla",
        grid=(batch, GLA_HEADS, nt),
        in_specs=[
            pl.BlockSpec((tc, GLA_DK), lambda b, h, i: (rows(b, h, i), COL_QL // GLA_DK + h)),
            pl.BlockSpec((tc, GLA_DK), lambda b, h, i: (rows(b, h, i), COL_KL // GLA_DK + h)),
            pl.BlockSpec((tc, GLA_DV), lambda b, h, i: (rows(b, h, i), COL_VL // GLA_DV + h)),
            pl.BlockSpec((tc, GLA_DV), lambda b, h, i: (rows(b, h, i), COL_RL // GLA_DV + h)),
            pl.BlockSpec((tc, LANES), lambda b, h, i: (rows(b, h, i), COL_MISC // LANES)),
            pl.BlockSpec((LANES, GLA_DK), lambda b, h, i: (0, h)),
            pl.BlockSpec((1, GLA_DK), lambda b, h, i: (0, h)),
            pl.BlockSpec((1, GLA_DV), lambda b, h, i: (0, 0)),
            pl.BlockSpec((GLA_DIRECT, tc, GLA_CHUNK), lambda b, h, i: (0, 0, 0)),
        ],
        out_specs=pl.BlockSpec((tc, GLA_DV), lambda b, h, i: (rows(b, h, i), h)),
        out_shape=jax.ShapeDtypeStruct((n, D_GLA_OUT), BF16),
        scratch_shapes=[pltpu.VMEM((GLA_DV, GLA_DK), F32)],
        compiler_params=_cparams(("parallel", "parallel", "arbitrary"), 32),
    )(proj, proj, proj, proj, proj, w2, b_gk.reshape(1, -1), g_gla_out.reshape(1, GLA_DV), band)


NSA_TQ = 256
NSA_NQ = NSA_TQ * NSA_GROUP
NEG = -1e30
SLC_CHUNK = 512
WIN_KEYS = WINDOW + NSA_TQ
FIRST_KEYS = SEL_BLOCK
CMP_CHUNK = 128
FEAT = 16


def _nsa_prep_kernel(q_ref, cmp_ref, slc_ref, win_ref, misc_ref, gq_ref, gks_ref, gkw_ref, bg_ref,
                     qt_ref, kcm_ref, vcm_ref, ks_ref, vst_ref, kw_ref, vwt_ref, gt_ref):
    q = q_ref[...]
    for k in range(NSA_KV_HEADS):
        for g in range(NSA_GROUP):
            h = k * NSA_GROUP + g
            qn = _rms(q[:, h * NSA_DK:(h + 1) * NSA_DK], gq_ref[...]) * (NSA_DK ** -0.5 * LOG2E)
            qt_ref[k, :, g * NSA_TQ:(g + 1) * NSA_TQ] = qn.T.astype(BF16)
        ksl = slice(k * NSA_DK, (k + 1) * NSA_DK)
        vsl = slice((NSA_KV_HEADS + k) * NSA_DK, (NSA_KV_HEADS + k + 1) * NSA_DK)
        kcm_ref[k] = cmp_ref[:, ksl]
        vcm_ref[k] = cmp_ref[:, vsl]
        ks_ref[k] = _rms(slc_ref[:, ksl], gks_ref[...]).astype(BF16)
        vst_ref[k] = slc_ref[:, vsl].T.astype(BF16)
        kw_ref[k] = _rms(win_ref[:, ksl], gkw_ref[...]).astype(BF16)
        vwt_ref[k] = win_ref[:, vsl].T.astype(BF16)
    gates = jax.nn.sigmoid(misc_ref[...] + bg_ref[...])
    gt_ref[...] = gates.T[:NSA_KV_HEADS * GATE_ROWS, :]


def _nsa_prep(proj, g_q, g_ks, g_kw, b_gate, batch, seq):
    nqt = seq // NSA_TQ
    hk = NSA_KV_HEADS
    pair = 2 * hk * NSA_DK
    rows = lambda b, i: b * nqt + i
    vec = lambda n: pl.BlockSpec((1, n), lambda b, i: (0, 0))
    bias = jnp.zeros((1, LANES), F32).at[0, :MISC_AL].set(_gate_layout(b_gate))
    tok = lambda dt: jax.ShapeDtypeStruct((batch, hk, seq, NSA_DK), dt)
    tr = jax.ShapeDtypeStruct((batch, hk, nqt, NSA_DV, NSA_TQ), BF16)
    tok_spec = pl.BlockSpec((None, hk, NSA_TQ, NSA_DK), lambda b, i: (b, 0, i, 0))
    tr_spec = pl.BlockSpec((None, hk, None, NSA_DV, NSA_TQ), lambda b, i: (b, 0, i, 0, 0))
    return pl.pallas_call(
        _nsa_prep_kernel,
        name="nsa_prep",
        grid=(batch, nqt),
        in_specs=[
            pl.BlockSpec((NSA_TQ, NSA_HEADS * NSA_DK), lambda b, i: (rows(b, i), 0)),
            pl.BlockSpec((NSA_TQ, pair), lambda b, i: (rows(b, i), COL_KV // pair)),
            pl.BlockSpec((NSA_TQ, pair), lambda b, i: (rows(b, i), COL_KV // pair + 1)),
            pl.BlockSpec((NSA_TQ, pair), lambda b, i: (rows(b, i), COL_KV // pair + 2)),
            pl.BlockSpec((NSA_TQ, LANES), lambda b, i: (rows(b, i), COL_MISC // LANES)),
            vec(NSA_DK), vec(NSA_DK), vec(NSA_DK), vec(LANES),
        ],
        out_specs=[
            pl.BlockSpec((None, hk, None, NSA_DK, NSA_NQ), lambda b, i: (b, 0, i, 0, 0)),
            tok_spec, tok_spec, tok_spec, tr_spec, tok_spec, tr_spec,
            pl.BlockSpec((None, hk * GATE_ROWS, NSA_TQ), lambda b, i: (b, 0, i)),
        ],
        out_shape=[
            jax.ShapeDtypeStruct((batch, hk, nqt, NSA_DK, NSA_NQ), BF16),
            tok(F32), tok(F32), tok(BF16), tr, tok(BF16), tr,
            jax.ShapeDtypeStruct((batch, hk * GATE_ROWS, seq), F32),
        ],
        compiler_params=_cparams(("parallel", "parallel"), 32),
    )(proj, proj, proj, proj, proj, g_q.reshape(1, -1), g_ks.reshape(1, -1), g_kw.reshape(1, -1), bias)


def _compress_kernel(uk_ref, uv_ref, pek_ref, pev_ref, wk1_ref, wk2_ref, wv1_ref, wv2t_ref, gk_ref,
                     kc_ref, vct_ref):
    def hidden(u_ref, pe_ref, w1_ref):
        n = u_ref.shape[0] // CMP_STRIDE
        toks = [u_ref[pl.ds(l, n, stride=CMP_STRIDE), :] for l in range(CMP_STRIDE)]

        def half(h):
            rows = [(toks[l] + pe_ref[h * CMP_STRIDE + l:h * CMP_STRIDE + l + 1, :]).astype(BF16)
                    for l in range(CMP_STRIDE)]
            return jnp.dot(jnp.concatenate(rows, axis=1), w1_ref[h], preferred_element_type=F32)

        hid = half(0) + pltpu.roll(half(1), n - 1, 0)
        return (hid * jax.nn.sigmoid(hid)).astype(BF16)

    kc = jnp.dot(hidden(uk_ref, pek_ref, wk1_ref), wk2_ref[...], preferred_element_type=F32)
    kc_ref[...] = _rms(kc, gk_ref[...]).astype(BF16)
    vct = lax.dot_general(wv2t_ref[...], hidden(uv_ref, pev_ref, wv1_ref), _NT,
                          preferred_element_type=F32).astype(BF16)
    for c in range(vct_ref.shape[0]):
        vct_ref[c] = vct[:, c * CMP_CHUNK:(c + 1) * CMP_CHUNK]


def _compress(k_tok, v_tok, pe_k, pe_v, w_ck1, w_ck2, w_cv1, w_cv2, g_kc):
    batch, hk, seq, d = k_tok.shape
    nc = seq // CMP_STRIDE
    flat = CMP_STRIDE * d
    halves = CMP_LEN // CMP_STRIDE
    u_spec = pl.BlockSpec((None, None, seq, d), lambda b, k: (b, k, 0, 0))
    full = lambda shape: pl.BlockSpec(shape, lambda b, k: (0,) * len(shape))
    return pl.pallas_call(
        _compress_kernel,
        name="nsa_compress",
        grid=(batch, hk),
        in_specs=[u_spec, u_spec, full((CMP_LEN, d)), full((CMP_LEN, d)),
                  full((halves, flat, CMP_HIDDEN)), full((CMP_HIDDEN, d)),
                  full((halves, flat, CMP_HIDDEN)), full((d, CMP_HIDDEN)), full((1, d))],
        out_specs=[pl.BlockSpec((None, None, nc, d), lambda b, k: (b, k, 0, 0)),
                   pl.BlockSpec((None, None, nc // CMP_CHUNK, d, CMP_CHUNK),
                                lambda b, k: (b, k, 0, 0, 0))],
        out_shape=[jax.ShapeDtypeStruct((batch, hk, nc, d), BF16),
                   jax.ShapeDtypeStruct((batch, hk, nc // CMP_CHUNK, d, CMP_CHUNK), BF16)],
        compiler_params=_cparams(("parallel", "parallel"), 48),
    )(k_tok, v_tok, pe_k, pe_v,
      w_ck1.astype(BF16).reshape(halves, flat, CMP_HIDDEN), w_ck2.astype(BF16),
      w_cv1.astype(BF16).reshape(halves, flat, CMP_HIDDEN), w_cv2.T.astype(BF16), g_kc.reshape(1, d))


def _query_rows(qt):
    lane = lax.broadcasted_iota(jnp.int32, (1, NSA_NQ), 1)
    t = qt * NSA_TQ + (lane & (NSA_TQ - 1))
    return t


def _topk_mask(imp, cur, n_top):
    ns, nt = imp.shape
    j = lax.broadcasted_iota(jnp.int32, (ns, nt), 0)
    forced = (j == 0) | (j == cur) | (j == cur - 1)
    visible = j <= cur
    work = jnp.where(visible & jnp.logical_not(forced), imp, -1.0)

    def extract(_, work):
        m = jnp.max(work, axis=0, keepdims=True)
        cand = (work == m) & (m > -0.5)
        idx = jnp.min(jnp.where(cand, j, ns), axis=0, keepdims=True)
        return jnp.where(j == idx, -2.0, work)

    work = lax.fori_loop(0, n_top - 3, extract, work)
    return (forced & visible) | (work == -2.0)


def _feature_rows(rows):
    n = rows[0].shape[1]
    r = lax.broadcasted_iota(jnp.int32, (FEAT // 2, n), 0)
    out = jnp.zeros((FEAT // 2, n), F32)
    for i, row in enumerate(rows):
        out = jnp.where(r == i, row, out)
    return out


def _aug_rhs(q, mask8, bias8):
    return jnp.concatenate([q, jnp.concatenate([mask8, bias8], axis=0).astype(BF16)], axis=0)


def _key_features(n, kind):
    a = np.arange(n)
    f = np.zeros((n, FEAT), np.float32)
    unit = CMP_CHUNK if kind == "cmp" else SEL_BLOCK
    if kind == "own":
        a = a % NSA_TQ
    f[:, 8:11] = (a % unit)[:, None]
    f[:, 11:14] = (a // unit)[:, None]
    rows = np.arange(n)
    if kind == "slc":
        f[rows, a // SEL_BLOCK] = 1.0
    if kind == "win":
        f[rows, a // NSA_TQ] = 1.0
    if kind == "own":
        f[rows, np.where(rows < NSA_TQ, FIRST_KEYS // SEL_BLOCK, 0) + a // SEL_BLOCK] = 1.0
    return jnp.asarray(f, BF16)


LOG2E = float(np.log2(np.e))


def _alibi_rows(slope, lo_unit, hi_unit):
    pieces = [p.astype(F32) for p in _split3(slope * LOG2E)]
    return _feature_rows([lo_unit * p for p in pieces] + [hi_unit * p for p in pieces])


def _nsa_sel_kernel(qt_ref, kc_ref, kf_ref, vct_ref, mmat_ref, slope_ref, negsel_ref, flag_ref,
                    oc_ref, near_ref, *, n_top):
    qt = pl.program_id(2)
    t = _query_rows(qt)
    slope = slope_ref[...]
    q = qt_ref[...]
    zeros8 = jnp.zeros((FEAT // 2, NSA_NQ), F32)
    per_tile = NSA_TQ // CMP_STRIDE
    n_chunks = (per_tile * qt + per_tile + CMP_CHUNK - 1) // CMP_CHUNK
    rhs = _aug_rhs(q, zeros8, _alibi_rows(slope, CMP_STRIDE, CMP_CHUNK * CMP_STRIDE))

    def branch(n):
        rows = n * CMP_CHUNK
        edge = rows - min(n, 2) * CMP_CHUNK

        def run():
            lhs = jnp.concatenate([kc_ref[:rows, :], kf_ref[:rows, :]], axis=1)
            s = jnp.dot(lhs, rhs, preferred_element_type=F32)
            ci = edge + lax.broadcasted_iota(jnp.int32, (rows - edge, 1), 0)
            tail = jnp.where(t >= ci * CMP_STRIDE + (CMP_LEN - 1), s[edge:], NEG)
            s = tail if edge == 0 else jnp.concatenate([s[:edge], tail], axis=0)
            m = jnp.maximum(jnp.max(s, axis=0, keepdims=True), 0.1 * NEG)
            p = jnp.exp2(s - m)
            l = jnp.sum(p, axis=0, keepdims=True)
            inv = 1.0 / jnp.where(l > 0.0, l, 1.0)
            vt = jnp.concatenate([vct_ref[c] for c in range(n)], axis=1)
            oc = jnp.dot(vt, p.astype(BF16), preferred_element_type=F32) * inv
            p = p * inv
            psum = sum(p[:, g * NSA_TQ:(g + 1) * NSA_TQ] for g in range(NSA_GROUP))
            return oc, _dot_exact_lhs(mmat_ref[:, :rows], psum)

        return run

    oc, imp = lax.switch(n_chunks - 1, [branch(n + 1) for n in range(kc_ref.shape[0] // CMP_CHUNK)])
    oc_ref[...] = oc
    cur = lax.shift_right_logical(t[:, :NSA_TQ], 6)
    sel = _topk_mask(imp, cur, n_top)
    j = lax.broadcasted_iota(jnp.int32, sel.shape, 0)
    per_own = NSA_TQ // SEL_BLOCK
    past = sel & (j < per_own * qt)
    negsel = jnp.where(past, 0.0, NEG)
    negsel_ref[...] = negsel
    own = [jnp.max(jnp.where(sel & (j == per_own * qt + r), 1.0, 0.0), axis=0, keepdims=True)
           for r in range(per_own)]
    near_ref[...] = _feature_rows([negsel[r:r + 1, :] for r in range(FIRST_KEYS // SEL_BLOCK)]
                                  + [jnp.where(o > 0.5, 0.0, NEG) for o in own])
    listed = (past & (j >= FIRST_KEYS // SEL_BLOCK)).astype(F32)
    per_chunk = SLC_CHUNK // SEL_BLOCK
    taken = jnp.max(listed.reshape(listed.shape[0] // per_chunk, per_chunk, NSA_TQ), axis=1)
    flag_ref[...] = jnp.max(taken, axis=1, keepdims=True).astype(jnp.int32)


def _nsa_select(qt_arr, kc, vct, batch, seq):
    hk = NSA_KV_HEADS
    nqt = seq // NSA_TQ
    nc = seq // CMP_STRIDE
    ns = seq // SEL_BLOCK
    n_cmp = nc - (CMP_LEN // CMP_STRIDE - 1)
    n_top = min(SEL_TOP, ns)
    ratio, lead = SEL_BLOCK // CMP_STRIDE, CMP_LEN // CMP_STRIDE - 1
    mm = np.zeros((ns, nc), np.float32)
    for r in range(ratio + lead):
        st = CMP_STRIDE * (r - lead)
        ov = (min(st + CMP_LEN, SEL_BLOCK) - max(st, 0)) / CMP_STRIDE
        for jb in range(ns):
            i = jb * ratio + r - lead
            if 0 <= i < n_cmp:
                mm[jb, i] += ov
    slopes = jnp.exp2(-8.0 * jnp.arange(1, NSA_HEADS + 1, dtype=F32) / NSA_HEADS)
    slopes = jnp.repeat(slopes.reshape(hk, 1, NSA_GROUP), NSA_TQ, axis=2)
    negsel, flags, oc, near = pl.pallas_call(
        functools.partial(_nsa_sel_kernel, n_top=n_top),
        name="nsa_select",
        grid=(batch, hk, nqt),
        in_specs=[
            pl.BlockSpec((None, None, None, NSA_DK, NSA_NQ), lambda b, k, i: (b, k, i, 0, 0)),
            pl.BlockSpec((None, None, nc, NSA_DK), lambda b, k, i: (b, k, 0, 0)),
            pl.BlockSpec((nc, FEAT), lambda b, k, i: (0, 0)),
            pl.BlockSpec((None, None, nc // CMP_CHUNK, NSA_DV, CMP_CHUNK), lambda b, k, i: (b, k, 0, 0, 0)),
            pl.BlockSpec((ns, nc), lambda b, k, i: (0, 0)),
            pl.BlockSpec((None, 1, NSA_NQ), lambda b, k, i: (k, 0, 0)),
        ],
        out_specs=[
            pl.BlockSpec((None, None, None, ns, NSA_TQ), lambda b, k, i: (b, k, i, 0, 0)),
            pl.BlockSpec((None, None, None, seq // SLC_CHUNK, 1), lambda b, k, i: (b, k, i, 0, 0)),
            pl.BlockSpec((None, None, None, NSA_DV, NSA_NQ), lambda b, k, i: (b, k, i, 0, 0)),
            pl.BlockSpec((None, None, None, FEAT // 2, NSA_TQ), lambda b, k, i: (b, k, i, 0, 0)),
        ],
        out_shape=[
            jax.ShapeDtypeStruct((batch, hk, nqt, ns, NSA_TQ), F32),
            jax.ShapeDtypeStruct((batch, hk, nqt, seq // SLC_CHUNK, 1), jnp.int32),
            jax.ShapeDtypeStruct((batch, hk, nqt, NSA_DV, NSA_NQ), F32),
            jax.ShapeDtypeStruct((batch, hk, nqt, FEAT // 2, NSA_TQ), F32),
        ],
        compiler_params=_cparams(("parallel", "parallel", "parallel"), 48),
    )(qt_arr, kc, _key_features(nc, "cmp"), vct, jnp.asarray(mm, BF16), slopes)
    return negsel, flags, oc, near, slopes


def _nsa_attn_kernel(flag_ref, qt_ref, ks_ref, vst_ref, kw_ref, vwt_ref, negsel_ref, near_ref, oc_ref, gt_ref,
                     slope_ref, go_ref, kfs_ref, kfw_ref, kfo_ref, o_ref, m_ref, l_ref, acc_ref,
                     sa_ref, sb_ref, lst_ref):
    qt = pl.program_id(2)
    t0 = qt * NSA_TQ
    t = _query_rows(qt)
    slope = slope_ref[...]
    q = qt_ref[...]
    blocks = SLC_CHUNK // SEL_BLOCK
    a = lax.broadcasted_iota(jnp.int32, (NSA_TQ, 1), 0)
    u = t - t0
    causal = jnp.where(a <= u, 0.0, NEG)
    too_old = jnp.where(a > u, 0.0, NEG)
    dot = functools.partial(jnp.dot, preferred_element_type=F32)
    per_group = lambda rows8: jnp.concatenate([rows8] * NSA_GROUP, axis=1)
    r8 = lax.broadcasted_iota(jnp.int32, (FEAT // 2, NSA_TQ), 0)
    slope2 = slope * LOG2E
    alibi8 = _alibi_rows(slope, 1, SEL_BLOCK)

    lst_ref[0] = 0

    def listing(c, n):
        lst_ref[n] = c
        return n + (flag_ref[c, 0] > 0).astype(jnp.int32)

    n = lax.fori_loop(0, (t0 + SLC_CHUNK - 1) // SLC_CHUNK, listing, 0)
    lst_ref[n] = lst_ref[jnp.maximum(n - 1, 0)]

    def scores(c, dst_ref):
        k0 = pl.multiple_of(c * SLC_CHUNK, SLC_CHUNK)
        neg8 = negsel_ref[pl.ds(pl.multiple_of(c * blocks, blocks), blocks), :]
        neg8 = jnp.where((r8 < FIRST_KEYS // SEL_BLOCK) & (c == 0), NEG, neg8)
        lhs = jnp.concatenate([ks_ref[pl.ds(k0, SLC_CHUNK), :], kfs_ref[...]], axis=1)
        dst_ref[...] = dot(lhs, _aug_rhs(q, per_group(neg8), alibi8))

    scores(lst_ref[0], sa_ref)

    lhs = jnp.concatenate(
        [jnp.concatenate([ks_ref[pl.ds(pl.multiple_of(t0, NSA_TQ), NSA_TQ), :], ks_ref[:FIRST_KEYS, :]], axis=0),
         kfo_ref[...]], axis=1)
    s = dot(lhs, _aug_rhs(q, per_group(near_ref[...]), alibi8))
    s_own, s_first = s[:NSA_TQ] + causal, s[NSA_TQ:]
    first_off = (-t0).astype(F32) * slope2
    m = jnp.maximum(jnp.max(s_own, axis=0, keepdims=True), jnp.max(s_first, axis=0, keepdims=True) + first_off)
    p = jnp.concatenate([jnp.exp2(s_own - m), jnp.exp2(s_first - (m - first_off))], axis=0)
    m_ref[...] = m
    l_ref[...] = jnp.sum(p, axis=0, keepdims=True)
    acc_ref[...] = dot(jnp.concatenate([vst_ref[qt], vst_ref[0][:, :FIRST_KEYS]], axis=1), p.astype(BF16))

    wsub = WIN_KEYS // NSA_TQ
    before_start = jnp.where(r8 < WINDOW // NSA_TQ - qt, NEG, 0.0)
    lhs = jnp.concatenate([kw_ref[pl.ds(pl.multiple_of(t0, NSA_TQ), WIN_KEYS), :], kfw_ref[...]], axis=1)
    sw = dot(lhs, _aug_rhs(q, per_group(before_start), alibi8))
    sw = jnp.concatenate([sw[:NSA_TQ] + too_old, sw[NSA_TQ:WINDOW], sw[WINDOW:] + causal], axis=0)
    pw = jnp.exp2(sw - jnp.max(sw, axis=0, keepdims=True))
    vt = jnp.concatenate([vwt_ref[qt + i] for i in range(wsub)], axis=1)
    o_w = dot(vt, pw.astype(BF16)) / jnp.sum(pw, axis=0, keepdims=True)

    sub = SLC_CHUNK // NSA_TQ

    def update(c, src_ref):
        s = src_ref[...]
        off = (c * SLC_CHUNK - t0).astype(F32) * slope2
        m_old = m_ref[...]
        m_new = jnp.maximum(m_old, jnp.max(s, axis=0, keepdims=True) + off)
        p = jnp.exp2(s - (m_new - off))
        alpha = jnp.exp2(m_old - m_new)
        l_ref[...] = alpha * l_ref[...] + jnp.sum(p, axis=0, keepdims=True)
        vt = jnp.concatenate([vst_ref[c * sub + i] for i in range(sub)], axis=1)
        acc_ref[...] = alpha * acc_ref[...] + dot(vt, p.astype(BF16))
        m_ref[...] = m_new

    def pair(tp, carry):
        i = 2 * tp
        scores(lst_ref[i + 1], sb_ref)
        update(lst_ref[i], sa_ref)

        @pl.when(i + 1 < n)
        def _():
            scores(lst_ref[i + 2], sa_ref)
            update(lst_ref[i + 1], sb_ref)

        return carry

    lax.fori_loop(0, (n + 1) // 2, pair, 0)

    def gate(br):
        return jnp.concatenate([gt_ref[br * NSA_GROUP + g:br * NSA_GROUP + g + 1, :]
                                for g in range(NSA_GROUP)], axis=1)

    o = gate(0) * oc_ref[...] + gate(1) * (acc_ref[...] / l_ref[...]) + gate(2) * o_w
    o = o * lax.rsqrt(jnp.mean(o * o, axis=0, keepdims=True) + EPS) * go_ref[...]
    for g in range(NSA_GROUP):
        sl = slice(g * NSA_TQ, (g + 1) * NSA_TQ)
        o_ref[:, g * NSA_DV:(g + 1) * NSA_DV] = o[:, sl].T.astype(o_ref.dtype)


def _nsa_attend(flags, qt_arr, ks, vst, kw, vwt, negsel, near, oc, gates_t, slopes, g_nsa_out, batch, seq):
    hk = NSA_KV_HEADS
    nqt = seq // NSA_TQ
    ns = seq // SEL_BLOCK
    per_tile = lambda shape: pl.BlockSpec((None, None, None) + shape, lambda b, k, i: (b, k, i, 0, 0))
    tok = lambda n: pl.BlockSpec((None, None, n, NSA_DK), lambda b, k, i: (b, k, 0, 0))
    tr = lambda n: pl.BlockSpec((None, None, n, NSA_DV, NSA_TQ), lambda b, k, i: (b, k, 0, 0, 0))
    wpad = WINDOW // NSA_TQ
    kw = jnp.pad(kw, ((0, 0), (0, 0), (WINDOW, 0), (0, 0)))
    vwt = jnp.pad(vwt, ((0, 0), (0, 0), (wpad, 0), (0, 0), (0, 0)))
    return pl.pallas_call(
        _nsa_attn_kernel,
        name="nsa_attend",
        grid=(batch, hk, nqt),
        in_specs=[
            pl.BlockSpec((None, None, None, seq // SLC_CHUNK, 1), lambda b, k, i: (b, k, i, 0, 0),
                         memory_space=pltpu.SMEM),
            per_tile((NSA_DK, NSA_NQ)),
            tok(seq), tr(nqt), tok(seq + WINDOW), tr(nqt + wpad),
            per_tile((ns, NSA_TQ)),
            per_tile((FEAT // 2, NSA_TQ)),
            per_tile((NSA_DV, NSA_NQ)),
            pl.BlockSpec((None, GATE_ROWS, NSA_TQ), lambda b, k, i: (b, k, i)),
            pl.BlockSpec((None, 1, NSA_NQ), lambda b, k, i: (k, 0, 0)),
            pl.BlockSpec((NSA_DV, 1), lambda b, k, i: (0, 0)),
            pl.BlockSpec((SLC_CHUNK, FEAT), lambda b, k, i: (0, 0)),
            pl.BlockSpec((WIN_KEYS, FEAT), lambda b, k, i: (0, 0)),
            pl.BlockSpec((NSA_TQ + FIRST_KEYS, FEAT), lambda b, k, i: (0, 0)),
        ],
        out_specs=pl.BlockSpec((NSA_TQ, NSA_GROUP * NSA_DV), lambda b, k, i: (b * nqt + i, k)),
        out_shape=jax.ShapeDtypeStruct((batch * seq, D_NSA_OUT), BF16),
        scratch_shapes=[pltpu.VMEM((1, NSA_NQ), F32), pltpu.VMEM((1, NSA_NQ), F32),
                        pltpu.VMEM((NSA_DV, NSA_NQ), F32),
                        pltpu.VMEM((SLC_CHUNK, NSA_NQ), F32), pltpu.VMEM((SLC_CHUNK, NSA_NQ), F32),
                        pltpu.SMEM((seq // SLC_CHUNK + 1,), jnp.int32)],
        compiler_params=_cparams(("parallel", "parallel", "arbitrary"), 48),
    )(flags, qt_arr, ks, vst, kw, vwt, negsel, near, oc, gates_t, slopes, g_nsa_out.reshape(NSA_DV, 1),
      _key_features(SLC_CHUNK, "slc"), _key_features(WIN_KEYS, "win"),
      _key_features(NSA_TQ + FIRST_KEYS, "own"))


def _nsa(proj, b_gate, g_q, g_kc, g_ks, g_kw, pe_k, pe_v, w_ck1, w_ck2, w_cv1, w_cv2, g_nsa_out,
         batch, seq):
    assert seq % (CMP_CHUNK * CMP_STRIDE) == 0 and seq >= WIN_KEYS
    qt_arr, k_tok, v_tok, ks, vst, kw, vwt, gates_t = _nsa_prep(proj, g_q, g_ks, g_kw, b_gate, batch, seq)
    kc, vct = _compress(k_tok, v_tok, pe_k, pe_v, w_ck1, w_ck2, w_cv1, w_cv2, g_kc)
    negsel, flags, oc, near, slopes = _nsa_select(qt_arr, kc, vct, batch, seq)
    return _nsa_attend(flags, qt_arr, ks, vst, kw, vwt, negsel, near, oc, gates_t, slopes, g_nsa_out,
                       batch, seq)


def kernel(x, mem, g_mix, w_in, b_nsa_gate, g_q, g_kc, g_ks, g_kw, pe_k, pe_v, w_ck1, w_ck2,
           w_cv1, w_cv2, g_nsa_out, w_gk2, b_gk, g_gla_out, w_out, g_cross, g_mem, w_cq, w_ck,
           w_cv, g_cq, g_ck, w_co, g_ffn, w_gu, w_down):
    batch, seq, _ = x.shape
    x2d = x.reshape(batch * seq, D_MODEL)
    for l in range(w_in.shape[0]):
        proj = _in_proj(x2d, g_mix[l], *_pack_w_in(w_in[l]))
        o_gla = _gla(proj, w_gk2[l], b_gk[l], g_gla_out[l], batch, seq)
        o_nsa = _nsa(proj, b_nsa_gate[l], g_q[l], g_kc[l], g_ks[l], g_kw[l], pe_k[l], pe_v[l],
                     w_ck1[l], w_ck2[l], w_cv1[l], w_cv2[l], g_nsa_out[l], batch, seq)
        ck, cv = _mem_kv(mem, g_mem[l], w_ck[l], w_cv[l], g_ck[l])
        x2d = _out_cross(x2d, o_nsa, o_gla, w_out[l], g_cross[l], w_cq[l], g_cq[l], ck, cv,
                         w_co[l], seq)
        x2d = _ffn(x2d, g_ffn[l], w_gu[l], w_down[l])
    return x2d.reshape(batch, seq, D_MODEL)
```

```python
import functools

import numpy as np
import jax
import jax.numpy as jnp
from jax import lax
from jax.experimental import pallas as pl
from jax.experimental.pallas import tpu as pltpu

F32 = jnp.float32
BF16 = jnp.bfloat16

D_MODEL = 2048
EPS = 1e-6
NSA_HEADS = 8
NSA_KV_HEADS = 2
NSA_GROUP = NSA_HEADS // NSA_KV_HEADS
NSA_DK = 128
NSA_DV = 128
CMP_LEN = 32
CMP_STRIDE = 16
CMP_HIDDEN = 256
SEL_BLOCK = 64
SEL_TOP = 16
WINDOW = 512
GLA_HEADS = 4
GLA_DK = 128
GLA_DV = 256
GLA_GATE_RANK = 16
GLA_GATE_NORM = 16.0
GLA_CHUNK = 64
MEM_HEADS = 4
MEM_DH = 128
D_FF = -(-8 * D_MODEL // (3 * 256)) * 256
D_NSA_OUT = NSA_HEADS * NSA_DV
D_GLA_OUT = GLA_HEADS * GLA_DV

LANES = 128

COL_Q = 0
COL_KV = COL_Q + NSA_HEADS * NSA_DK
COL_QL = COL_KV + 6 * NSA_KV_HEADS * NSA_DK
COL_KL = COL_QL + GLA_HEADS * GLA_DK
COL_VL = COL_KL + GLA_HEADS * GLA_DK
COL_RL = COL_VL + GLA_HEADS * GLA_DV
COL_MISC = COL_RL + GLA_HEADS * GLA_DV
MISC_AL = 32
D_IN_PACKED = COL_MISC + LANES


def _cparams(semantics, vmem_mb):
    return pltpu.CompilerParams(dimension_semantics=semantics,
                                vmem_limit_bytes=vmem_mb * 1024 * 1024)


def _rms(u, g):
    return u * lax.rsqrt(jnp.mean(u * u, axis=-1, keepdims=True) + EPS) * g


def _inproj_kernel(x_ref, g_ref, w_nsa_ref, w_rest_ref, o_ref):
    h = _rms(x_ref[...], g_ref[...]).astype(BF16)
    o_ref[:, :COL_QL] = jnp.dot(h, w_nsa_ref[...], preferred_element_type=F32)
    o_ref[:, COL_QL:] = jnp.dot(h, w_rest_ref[...], preferred_element_type=F32)


def _in_proj(x2d, g_mix, w_in, w_rest, tm=256):
    n = x2d.shape[0]
    resident = lambda cols: pl.BlockSpec((D_MODEL, cols), lambda i: (0, 0), pipeline_mode=pl.Buffered(1))
    return pl.pallas_call(
        _inproj_kernel,
        name="in_proj",
        grid=(n // tm,),
        in_specs=[
            pl.BlockSpec((tm, D_MODEL), lambda i: (i, 0)),
            pl.BlockSpec((1, D_MODEL), lambda i: (0, 0)),
            resident(COL_QL),
            resident(D_IN_PACKED - COL_QL),
        ],
        out_specs=pl.BlockSpec((tm, D_IN_PACKED), lambda i: (i, 0)),
        out_shape=jax.ShapeDtypeStruct((n, D_IN_PACKED), F32),
        compiler_params=_cparams(("parallel",), 48),
    )(x2d, g_mix.reshape(1, D_MODEL), w_in, w_rest)


GATE_ROWS = 16


def _gate_layout(u):
    pos = np.arange(MISC_AL)
    k, r = pos // GATE_ROWS, pos % GATE_ROWS
    br, g = r // NSA_GROUP, r % NSA_GROUP
    used = r < 3 * NSA_GROUP
    src = np.where(used, (k * NSA_GROUP + g) * 3 + br, 0)
    return jnp.where(jnp.asarray(used), jnp.take(u, jnp.asarray(src), axis=-1), 0.0)


def _pack_w_in(w_in):
    w_in = w_in.astype(BF16)
    sizes = (NSA_HEADS * NSA_DK,) + (NSA_KV_HEADS * NSA_DK,) * 6 + (3 * NSA_HEADS,) + (
        GLA_HEADS * GLA_DK, GLA_HEADS * GLA_DK, GLA_HEADS * GLA_DV, GLA_GATE_RANK, GLA_HEADS * GLA_DV)
    offs = np.concatenate([[0], np.cumsum(sizes)])
    assert offs[7] == COL_QL
    seg = lambda i: w_in[:, offs[i]:offs[i + 1]]
    zeros = lambda n: jnp.zeros((D_MODEL, n), w_in.dtype)
    misc = jnp.concatenate([_gate_layout(seg(7)), seg(11),
                            zeros(LANES - MISC_AL - GLA_GATE_RANK)], axis=1)
    return w_in, jnp.concatenate([seg(8), seg(9), seg(10), seg(12), misc], axis=1)


def _memkv_kernel(mem_ref, gm_ref, wk_ref, wv_ref, gk_ref, ck_ref, cv_ref):
    hm = _rms(mem_ref[...], gm_ref[...]).astype(BF16)
    k = jnp.dot(hm, wk_ref[...], preferred_element_type=F32)
    v = jnp.dot(hm, wv_ref[...], preferred_element_type=F32)
    for h in range(MEM_HEADS):
        sl = slice(h * MEM_DH, (h + 1) * MEM_DH)
        ck_ref[h] = _rms(k[:, sl], gk_ref[...]).astype(BF16)
        cv_ref[h] = v[:, sl].astype(BF16)


def _mem_kv(mem, g_mem, w_ck, w_cv, g_ck):
    b, m, _ = mem.shape
    dm = MEM_HEADS * MEM_DH
    out = jax.ShapeDtypeStruct((b, MEM_HEADS, m, MEM_DH), BF16)
    return pl.pallas_call(
        _memkv_kernel,
        name="mem_kv",
        grid=(b,),
        in_specs=[
            pl.BlockSpec((None, m, D_MODEL), lambda i: (i, 0, 0)),
            pl.BlockSpec((1, D_MODEL), lambda i: (0, 0)),
            pl.BlockSpec((D_MODEL, dm), lambda i: (0, 0)),
            pl.BlockSpec((D_MODEL, dm), lambda i: (0, 0)),
            pl.BlockSpec((1, MEM_DH), lambda i: (0, 0)),
        ],
        out_specs=[pl.BlockSpec((None, MEM_HEADS, m, MEM_DH), lambda i: (i, 0, 0, 0))] * 2,
        out_shape=[out, out],
        compiler_params=_cparams(("parallel",), 32),
    )(mem, g_mem.reshape(1, D_MODEL), w_ck.astype(BF16), w_cv.astype(BF16), g_ck.reshape(1, MEM_DH))


def _outx_kernel(x_ref, nsa_ref, gla_ref, wo1_ref, wo2_ref, gc_ref, wcq_ref, gcq_ref,
                 ck_ref, cv_ref, wco_ref, o_ref):
    x1 = (x_ref[...]
          + jnp.dot(nsa_ref[...], wo1_ref[...], preferred_element_type=F32)
          + jnp.dot(gla_ref[...], wo2_ref[...], preferred_element_type=F32))
    hq = _rms(x1, gc_ref[...]).astype(BF16)
    cq = jnp.dot(hq, wcq_ref[...], preferred_element_type=F32)
    outs = []
    for h in range(MEM_HEADS):
        c = _rms(cq[:, h * MEM_DH:(h + 1) * MEM_DH], gcq_ref[...]) * (MEM_DH ** -0.5)
        s = lax.dot_general(c.astype(BF16), ck_ref[h], (((1,), (1,)), ((), ())),
                            preferred_element_type=F32)
        p = jnp.exp(s - jnp.max(s, axis=-1, keepdims=True))
        p = p / jnp.sum(p, axis=-1, keepdims=True)
        outs.append(jnp.dot(p.astype(BF16), cv_ref[h], preferred_element_type=F32))
    oc = jnp.concatenate(outs, axis=-1).astype(BF16)
    o_ref[...] = x1 + jnp.dot(oc, wco_ref[...], preferred_element_type=F32)


def _out_cross(x2d, o_nsa, o_gla, w_out, g_cross, w_cq, g_cq, ck, cv, w_co, seq, tm=512):
    n = x2d.shape[0]
    tiles_per_batch = seq // tm
    dm = MEM_HEADS * MEM_DH
    m = ck.shape[2]
    full = lambda shape: pl.BlockSpec(shape, lambda i: (0,) * len(shape))
    kv_spec = pl.BlockSpec((None, MEM_HEADS, m, MEM_DH), lambda i: (i // tiles_per_batch, 0, 0, 0))
    w_out = w_out.astype(BF16)
    return pl.pallas_call(
        _outx_kernel,
        name="out_cross",
        grid=(n // tm,),
        in_specs=[
            pl.BlockSpec((tm, D_MODEL), lambda i: (i, 0)),
            pl.BlockSpec((tm, D_NSA_OUT), lambda i: (i, 0)),
            pl.BlockSpec((tm, D_GLA_OUT), lambda i: (i, 0)),
            pl.BlockSpec((D_NSA_OUT, D_MODEL), lambda i: (0, 0)),
            pl.BlockSpec((D_GLA_OUT, D_MODEL), lambda i: (D_NSA_OUT // D_GLA_OUT, 0)),
            full((1, D_MODEL)),
            full((D_MODEL, dm)),
            full((1, MEM_DH)),
            kv_spec,
            kv_spec,
            full((dm, D_MODEL)),
        ],
        out_specs=pl.BlockSpec((tm, D_MODEL), lambda i: (i, 0)),
        out_shape=jax.ShapeDtypeStruct((n, D_MODEL), F32),
        compiler_params=_cparams(("parallel",), 56),
    )(x2d, o_nsa, o_gla, w_out, w_out, g_cross.reshape(1, D_MODEL), w_cq.astype(BF16),
      g_cq.reshape(1, MEM_DH), ck, cv, w_co.astype(BF16))


def _ffn_kernel(x_ref, g_ref, wg_ref, wu_ref, wd_ref, o_ref, h_ref):
    @pl.when(pl.program_id(1) == 0)
    def _():
        x = x_ref[...]
        h_ref[...] = _rms(x, g_ref[...]).astype(BF16)
        o_ref[...] = x

    h = h_ref[...]
    g = jnp.dot(h, wg_ref[...], preferred_element_type=F32)
    u = jnp.dot(h, wu_ref[...], preferred_element_type=F32)
    a = (g * jax.nn.sigmoid(g) * u).astype(BF16)
    o_ref[...] += jnp.dot(a, wd_ref[...], preferred_element_type=F32)


def _ffn(x2d, g_ffn, w_gu, w_down, tm=512, tf=512):
    n = x2d.shape[0]
    w_gu = w_gu.astype(BF16)
    return pl.pallas_call(
        _ffn_kernel,
        name="ffn",
        grid=(n // tm, D_FF // tf),
        in_specs=[
            pl.BlockSpec((tm, D_MODEL), lambda i, j: (i, 0)),
            pl.BlockSpec((1, D_MODEL), lambda i, j: (0, 0)),
            pl.BlockSpec((D_MODEL, tf), lambda i, j: (0, j)),
            pl.BlockSpec((D_MODEL, tf), lambda i, j: (0, D_FF // tf + j)),
            pl.BlockSpec((tf, D_MODEL), lambda i, j: (j, 0)),
        ],
        out_specs=pl.BlockSpec((tm, D_MODEL), lambda i, j: (i, 0)),
        out_shape=jax.ShapeDtypeStruct((n, D_MODEL), F32),
        scratch_shapes=[pltpu.VMEM((tm, D_MODEL), BF16)],
        compiler_params=_cparams(("parallel", "arbitrary"), 48),
    )(x2d, g_ffn.reshape(1, D_MODEL), w_gu, w_gu, w_down.astype(BF16))


def _split3(x):
    hi = x.astype(BF16)
    r = x - hi.astype(F32)
    mid = r.astype(BF16)
    lo = (r - mid.astype(F32)).astype(BF16)
    return hi, mid, lo


def _dot_exact_lhs(a_bf16, x):
    return sum(jnp.dot(a_bf16, p, preferred_element_type=F32) for p in _split3(x))


def _dot_split(a, w):
    a_hi, a_lo, _ = _split3(a)
    w_hi, w_lo, _ = _split3(w)
    d = lambda p, q: jnp.dot(p, q, preferred_element_type=F32)
    return d(a_hi, w_hi) + d(a_hi, w_lo) + d(a_lo, w_hi)


_NT = (((1,), (1,)), ((), ()))
_TN = (((0,), (0,)), ((), ()))
GLA_DIRECT = 8


def _gla_kernel(ql_ref, kl_ref, vl_ref, rl_ref, misc_ref, w2_ref, bg_ref, go_ref, band_ref,
                o_ref, st_ref):
    c_len = GLA_CHUNK
    tc = ql_ref.shape[0]
    n_chunks = tc // c_len

    @pl.when(pl.program_id(2) == 0)
    def _():
        st_ref[...] = jnp.zeros_like(st_ref)

    x = _dot_split(misc_ref[...], w2_ref[...]) + bg_ref[...]
    la = (jnp.minimum(x, 0.0) - jnp.log(1.0 + jnp.exp(-jnp.abs(x)))) * (1.0 / GLA_GATE_NORM)
    pos = lax.broadcasted_iota(jnp.int32, (tc, 1), 0) & (c_len - 1)
    b = la
    sh = 1
    while sh < c_len:
        b = b + jnp.where(pos >= sh, pltpu.roll(b, sh, 0), 0.0)
        sh *= 2
    q = ql_ref[...] * (GLA_DK ** -0.5)
    k = kl_ref[...]
    b3 = b.reshape(n_chunks, c_len, GLA_DK)

    def chunk_row(r, n):
        return jnp.broadcast_to(b3[:, r:r + 1, :], (n_chunks, n, GLA_DK))

    level_q, level_k, level_s = [], [], []
    s = c_len
    while s > GLA_DIRECT:
        half = s // 2
        ref = jnp.concatenate([chunk_row(m0 + half - 1, s) for m0 in range(0, c_len, s)],
                              axis=1).reshape(tc, GLA_DK)
        second = (pos & (s - 1)) >= half
        level_q.append(jnp.where(second, q * jnp.exp(jnp.minimum(b - ref, 0.0)), 0.0).astype(BF16))
        level_k.append(jnp.where(second, 0.0, k * jnp.exp(jnp.minimum(ref - b, 0.0))).astype(BF16))
        level_s.append(s)
        s = half
    sublanes = 8
    group = lambda u: u.reshape(tc // sublanes, sublanes, GLA_DK)
    band = jnp.zeros((tc, c_len), F32)
    for d in range(GLA_DIRECT):
        k_sh = k if d == 0 else pltpu.roll(group(k), d, 1).reshape(tc, GLA_DK)
        b_sh = b if d == 0 else pltpu.roll(group(b), d, 1).reshape(tc, GLA_DK)
        cd = jnp.sum(q * k_sh * jnp.exp(jnp.minimum(b - b_sh, 0.0)), axis=-1, keepdims=True)
        band = band + cd * band_ref[d]
    b_end = chunk_row(c_len - 1, c_len).reshape(tc, GLA_DK)
    q_dec = (q * jnp.exp(b)).astype(BF16)
    k_dec = (k * jnp.exp(b_end - b)).astype(BF16)
    s_dec = jnp.exp(b_end)
    v = vl_ref[...].astype(BF16)
    row = lax.broadcasted_iota(jnp.int32, (c_len, 1), 0)
    col = lax.broadcasted_iota(jnp.int32, (1, c_len), 1)

    st = st_ref[...]
    for c in range(n_chunks):
        sl = slice(c * c_len, (c + 1) * c_len)
        attn = band[sl]
        for qs, ks, s in zip(level_q, level_k, level_s):
            a_s = lax.dot_general(qs[sl], ks[sl], _NT, preferred_element_type=F32)
            attn = attn + (a_s if s == c_len else jnp.where((row // s) == (col // s), a_s, 0.0))
        o = (jnp.dot(attn.astype(BF16), v[sl], preferred_element_type=F32)
             + lax.dot_general(q_dec[sl], st.astype(BF16), _NT, preferred_element_type=F32))
        st = (st * s_dec[c * c_len:c * c_len + 1]
              + lax.dot_general(v[sl], k_dec[sl], _TN, preferred_element_type=F32))
        r = rl_ref[sl, :]
        o_ref[sl, :] = (_rms(o, go_ref[...]) * (r * jax.nn.sigmoid(r))).astype(o_ref.dtype)
    st_ref[...] = st


def _gla(proj, w_gk2, b_gk, g_gla_out, batch, seq, tc=2048):
    n = proj.shape[0]
    nt = seq // tc
    w2 = jnp.zeros((LANES, GLA_HEADS * GLA_DK), F32).at[MISC_AL:MISC_AL + GLA_GATE_RANK].set(w_gk2)
    i_pos = np.arange(tc)[:, None] % GLA_CHUNK
    d_off = np.arange(GLA_DIRECT)[:, None, None]
    band = ((np.arange(GLA_CHUNK)[None, :] == i_pos - d_off) & (i_pos % GLA_DIRECT >= d_off))
    band = jnp.asarray(band, F32)
    rows = lambda b, h, i: b * nt + i
    return pl.pallas_call(
        _gla_kernel,
        name="gla",
        grid=(batch, GLA_HEADS, nt),
        in_specs=[
            pl.BlockSpec((tc, GLA_DK), lambda b, h, i: (rows(b, h, i), COL_QL // GLA_DK + h)),
            pl.BlockSpec((tc, GLA_DK), lambda b, h, i: (rows(b, h, i), COL_KL // GLA_DK + h)),
            pl.BlockSpec((tc, GLA_DV), lambda b, h, i: (rows(b, h, i), COL_VL // GLA_DV + h)),
            pl.BlockSpec((tc, GLA_DV), lambda b, h, i: (rows(b, h, i), COL_RL // GLA_DV + h)),
            pl.BlockSpec((tc, LANES), lambda b, h, i: (rows(b, h, i), COL_MISC // LANES)),
            pl.BlockSpec((LANES, GLA_DK), lambda b, h, i: (0, h)),
            pl.BlockSpec((1, GLA_DK), lambda b, h, i: (0, h)),
            pl.BlockSpec((1, GLA_DV), lambda b, h, i: (0, 0)),
            pl.BlockSpec((GLA_DIRECT, tc, GLA_CHUNK), lambda b, h, i: (0, 0, 0)),
        ],
        out_specs=pl.BlockSpec((tc, GLA_DV), lambda b, h, i: (rows(b, h, i), h)),
        out_shape=jax.ShapeDtypeStruct((n, D_GLA_OUT), BF16),
        scratch_shapes=[pltpu.VMEM((GLA_DV, GLA_DK), F32)],
        compiler_params=_cparams(("parallel", "parallel", "arbitrary"), 32),
    )(proj, proj, proj, proj, proj, w2, b_gk.reshape(1, -1), g_gla_out.reshape(1, GLA_DV), band)


NSA_TQ = 256
NSA_NQ = NSA_TQ * NSA_GROUP
NEG = -1e30
SLC_CHUNK = 512
WIN_KEYS = WINDOW + NSA_TQ
FIRST_KEYS = SEL_BLOCK
CMP_CHUNK = 128
FEAT = 16


def _nsa_prep_kernel(q_ref, cmp_ref, slc_ref, win_ref, misc_ref, gq_ref, gks_ref, gkw_ref, bg_ref,
                     qt_ref, kcm_ref, vcm_ref, ks_ref, vst_ref, kw_ref, vwt_ref, gt_ref):
    q = q_ref[...]
    for k in range(NSA_KV_HEADS):
        for g in range(NSA_GROUP):
            h = k * NSA_GROUP + g
            qn = _rms(q[:, h * NSA_DK:(h + 1) * NSA_DK], gq_ref[...]) * (NSA_DK ** -0.5 * LOG2E)
            qt_ref[k, :, g * NSA_TQ:(g + 1) * NSA_TQ] = qn.T.astype(BF16)
        ksl = slice(k * NSA_DK, (k + 1) * NSA_DK)
        vsl = slice((NSA_KV_HEADS + k) * NSA_DK, (NSA_KV_HEADS + k + 1) * NSA_DK)
        kcm_ref[k] = cmp_ref[:, ksl]
        vcm_ref[k] = cmp_ref[:, vsl]
        ks_ref[k] = _rms(slc_ref[:, ksl], gks_ref[...]).astype(BF16)
        vst_ref[k] = slc_ref[:, vsl].T.astype(BF16)
        kw_ref[k] = _rms(win_ref[:, ksl], gkw_ref[...]).astype(BF16)
        vwt_ref[k] = win_ref[:, vsl].T.astype(BF16)
    gates = jax.nn.sigmoid(misc_ref[...] + bg_ref[...])
    gt_ref[...] = gates.T[:NSA_KV_HEADS * GATE_ROWS, :]


def _nsa_prep(proj, g_q, g_ks, g_kw, b_gate, batch, seq):
    nqt = seq // NSA_TQ
    hk = NSA_KV_HEADS
    pair = 2 * hk * NSA_DK
    rows = lambda b, i: b * nqt + i
    vec = lambda n: pl.BlockSpec((1, n), lambda b, i: (0, 0))
    bias = jnp.zeros((1, LANES), F32).at[0, :MISC_AL].set(_gate_layout(b_gate))
    tok = lambda dt: jax.ShapeDtypeStruct((batch, hk, seq, NSA_DK), dt)
    tr = jax.ShapeDtypeStruct((batch, hk, nqt, NSA_DV, NSA_TQ), BF16)
    tok_spec = pl.BlockSpec((None, hk, NSA_TQ, NSA_DK), lambda b, i: (b, 0, i, 0))
    tr_spec = pl.BlockSpec((None, hk, None, NSA_DV, NSA_TQ), lambda b, i: (b, 0, i, 0, 0))
    return pl.pallas_call(
        _nsa_prep_kernel,
        name="nsa_prep",
        grid=(batch, nqt),
        in_specs=[
            pl.BlockSpec((NSA_TQ, NSA_HEADS * NSA_DK), lambda b, i: (rows(b, i), 0)),
            pl.BlockSpec((NSA_TQ, pair), lambda b, i: (rows(b, i), COL_KV // pair)),
            pl.BlockSpec((NSA_TQ, pair), lambda b, i: (rows(b, i), COL_KV // pair + 1)),
            pl.BlockSpec((NSA_TQ, pair), lambda b, i: (rows(b, i), COL_KV // pair + 2)),
            pl.BlockSpec((NSA_TQ, LANES), lambda b, i: (rows(b, i), COL_MISC // LANES)),
            vec(NSA_DK), vec(NSA_DK), vec(NSA_DK), vec(LANES),
        ],
        out_specs=[
            pl.BlockSpec((None, hk, None, NSA_DK, NSA_NQ), lambda b, i: (b, 0, i, 0, 0)),
            tok_spec, tok_spec, tok_spec, tr_spec, tok_spec, tr_spec,
            pl.BlockSpec((None, hk * GATE_ROWS, NSA_TQ), lambda b, i: (b, 0, i)),
        ],
        out_shape=[
            jax.ShapeDtypeStruct((batch, hk, nqt, NSA_DK, NSA_NQ), BF16),
            tok(F32), tok(F32), tok(BF16), tr, tok(BF16), tr,
            jax.ShapeDtypeStruct((batch, hk * GATE_ROWS, seq), F32),
        ],
        compiler_params=_cparams(("parallel", "parallel"), 32),
    )(proj, proj, proj, proj, proj, g_q.reshape(1, -1), g_ks.reshape(1, -1), g_kw.reshape(1, -1), bias)


def _compress_kernel(uk_ref, uv_ref, pek_ref, pev_ref, wk1_ref, wk2_ref, wv1_ref, wv2t_ref, gk_ref,
                     kc_ref, vct_ref):
    def hidden(u_ref, pe_ref, w1_ref):
        n = u_ref.shape[0] // CMP_STRIDE
        toks = [u_ref[pl.ds(l, n, stride=CMP_STRIDE), :] for l in range(CMP_STRIDE)]

        def half(h):
            rows = [(toks[l] + pe_ref[h * CMP_STRIDE + l:h * CMP_STRIDE + l + 1, :]).astype(BF16)
                    for l in range(CMP_STRIDE)]
            return jnp.dot(jnp.concatenate(rows, axis=1), w1_ref[h], preferred_element_type=F32)

        hid = half(0) + pltpu.roll(half(1), n - 1, 0)
        return (hid * jax.nn.sigmoid(hid)).astype(BF16)

    kc = jnp.dot(hidden(uk_ref, pek_ref, wk1_ref), wk2_ref[...], preferred_element_type=F32)
    kc_ref[...] = _rms(kc, gk_ref[...]).astype(BF16)
    vct = lax.dot_general(wv2t_ref[...], hidden(uv_ref, pev_ref, wv1_ref), _NT,
                          preferred_element_type=F32).astype(BF16)
    for c in range(vct_ref.shape[0]):
        vct_ref[c] = vct[:, c * CMP_CHUNK:(c + 1) * CMP_CHUNK]


def _compress(k_tok, v_tok, pe_k, pe_v, w_ck1, w_ck2, w_cv1, w_cv2, g_kc):
    batch, hk, seq, d = k_tok.shape
    nc = seq // CMP_STRIDE
    flat = CMP_STRIDE * d
    halves = CMP_LEN // CMP_STRIDE
    u_spec = pl.BlockSpec((None, None, seq, d), lambda b, k: (b, k, 0, 0))
    full = lambda shape: pl.BlockSpec(shape, lambda b, k: (0,) * len(shape))
    return pl.pallas_call(
        _compress_kernel,
        name="nsa_compress",
        grid=(batch, hk),
        in_specs=[u_spec, u_spec, full((CMP_LEN, d)), full((CMP_LEN, d)),
                  full((halves, flat, CMP_HIDDEN)), full((CMP_HIDDEN, d)),
                  full((halves, flat, CMP_HIDDEN)), full((d, CMP_HIDDEN)), full((1, d))],
        out_specs=[pl.BlockSpec((None, None, nc, d), lambda b, k: (b, k, 0, 0)),
                   pl.BlockSpec((None, None, nc // CMP_CHUNK, d, CMP_CHUNK),
                                lambda b, k: (b, k, 0, 0, 0))],
        out_shape=[jax.ShapeDtypeStruct((batch, hk, nc, d), BF16),
                   jax.ShapeDtypeStruct((batch, hk, nc // CMP_CHUNK, d, CMP_CHUNK), BF16)],
        compiler_params=_cparams(("parallel", "parallel"), 48),
    )(k_tok, v_tok, pe_k, pe_v,
      w_ck1.astype(BF16).reshape(halves, flat, CMP_HIDDEN), w_ck2.astype(BF16),
      w_cv1.astype(BF16).reshape(halves, flat, CMP_HIDDEN), w_cv2.T.astype(BF16), g_kc.reshape(1, d))


def _query_rows(qt):
    lane = lax.broadcasted_iota(jnp.int32, (1, NSA_NQ), 1)
    t = qt * NSA_TQ + (lane & (NSA_TQ - 1))
    return t


def _topk_mask(imp, cur, n_top):
    ns, nt = imp.shape
    j = lax.broadcasted_iota(jnp.int32, (ns, nt), 0)
    forced = (j == 0) | (j == cur) | (j == cur - 1)
    visible = j <= cur
    work = jnp.where(visible & jnp.logical_not(forced), imp, -1.0)

    def extract(_, work):
        m = jnp.max(work, axis=0, keepdims=True)
        cand = (work == m) & (m > -0.5)
        idx = jnp.min(jnp.where(cand, j, ns), axis=0, keepdims=True)
        return jnp.where(j == idx, -2.0, work)

    work = lax.fori_loop(0, n_top - 3, extract, work, unroll=True)
    return (forced & visible) | (work == -2.0)


def _feature_rows(rows):
    n = rows[0].shape[1]
    r = lax.broadcasted_iota(jnp.int32, (FEAT // 2, n), 0)
    out = jnp.zeros((FEAT // 2, n), F32)
    for i, row in enumerate(rows):
        out = jnp.where(r == i, row, out)
    return out


def _aug_rhs(q, mask8, bias8):
    return jnp.concatenate([q, jnp.concatenate([mask8, bias8], axis=0).astype(BF16)], axis=0)


def _key_features(n, kind):
    a = np.arange(n)
    f = np.zeros((n, FEAT), np.float32)
    unit = CMP_CHUNK if kind == "cmp" else SEL_BLOCK
    if kind == "own":
        a = a % NSA_TQ
    f[:, 8:11] = (a % unit)[:, None]
    f[:, 11:14] = (a // unit)[:, None]
    rows = np.arange(n)
    if kind == "slc":
        f[rows, a // SEL_BLOCK] = 1.0
    if kind == "win":
        f[rows, a // NSA_TQ] = 1.0
    if kind == "own":
        f[rows, np.where(rows < NSA_TQ, FIRST_KEYS // SEL_BLOCK, 0) + a // SEL_BLOCK] = 1.0
    return jnp.asarray(f, BF16)


LOG2E = float(np.log2(np.e))


def _alibi_rows(slope, lo_unit, hi_unit):
    pieces = [p.astype(F32) for p in _split3(slope * LOG2E)]
    return _feature_rows([lo_unit * p for p in pieces] + [hi_unit * p for p in pieces])


def _nsa_sel_kernel(qt_ref, kc_ref, kf_ref, vct_ref, mmat_ref, slope_ref, negsel_ref, flag_ref,
                    oc_ref, near_ref, *, n_top):
    qt = pl.program_id(2)
    t = _query_rows(qt)
    slope = slope_ref[...]
    q = qt_ref[...]
    zeros8 = jnp.zeros((FEAT // 2, NSA_NQ), F32)
    per_tile = NSA_TQ // CMP_STRIDE
    n_chunks = (per_tile * qt + per_tile + CMP_CHUNK - 1) // CMP_CHUNK
    rhs = _aug_rhs(q, zeros8, _alibi_rows(slope, CMP_STRIDE, CMP_CHUNK * CMP_STRIDE))

    def branch(n):
        rows = n * CMP_CHUNK
        edge = rows - min(n, 2) * CMP_CHUNK

        def run():
            lhs = jnp.concatenate([kc_ref[:rows, :], kf_ref[:rows, :]], axis=1)
            s = jnp.dot(lhs, rhs, preferred_element_type=F32)
            ci = edge + lax.broadcasted_iota(jnp.int32, (rows - edge, 1), 0)
            tail = jnp.where(t >= ci * CMP_STRIDE + (CMP_LEN - 1), s[edge:], NEG)
            s = tail if edge == 0 else jnp.concatenate([s[:edge], tail], axis=0)
            m = jnp.maximum(jnp.max(s, axis=0, keepdims=True), 0.1 * NEG)
            p = jnp.exp2(s - m)
            l = jnp.sum(p, axis=0, keepdims=True)
            inv = 1.0 / jnp.where(l > 0.0, l, 1.0)
            vt = jnp.concatenate([vct_ref[c] for c in range(n)], axis=1)
            oc = jnp.dot(vt, p.astype(BF16), preferred_element_type=F32) * inv
            p = p * inv
            psum = sum(p[:, g * NSA_TQ:(g + 1) * NSA_TQ] for g in range(NSA_GROUP))
            return oc, _dot_exact_lhs(mmat_ref[:, :rows], psum)

        return run

    oc, imp = lax.switch(n_chunks - 1, [branch(n + 1) for n in range(kc_ref.shape[0] // CMP_CHUNK)])
    oc_ref[...] = oc
    cur = lax.shift_right_logical(t[:, :NSA_TQ], 6)
    sel = _topk_mask(imp, cur, n_top)
    j = lax.broadcasted_iota(jnp.int32, sel.shape, 0)
    per_own = NSA_TQ // SEL_BLOCK
    past = sel & (j < per_own * qt)
    negsel = jnp.where(past, 0.0, NEG)
    negsel_ref[...] = negsel
    own = [jnp.max(jnp.where(sel & (j == per_own * qt + r), 1.0, 0.0), axis=0, keepdims=True)
           for r in range(per_own)]
    near_ref[...] = _feature_rows([negsel[r:r + 1, :] for r in range(FIRST_KEYS // SEL_BLOCK)]
                                  + [jnp.where(o > 0.5, 0.0, NEG) for o in own])
    listed = (past & (j >= FIRST_KEYS // SEL_BLOCK)).astype(F32)
    per_chunk = SLC_CHUNK // SEL_BLOCK
    taken = jnp.max(listed.reshape(listed.shape[0] // per_chunk, per_chunk, NSA_TQ), axis=1)
    flag_ref[...] = jnp.max(taken, axis=1, keepdims=True).astype(jnp.int32)


def _nsa_select(qt_arr, kc, vct, batch, seq):
    hk = NSA_KV_HEADS
    nqt = seq // NSA_TQ
    nc = seq // CMP_STRIDE
    ns = seq // SEL_BLOCK
    n_cmp = nc - (CMP_LEN // CMP_STRIDE - 1)
    n_top = min(SEL_TOP, ns)
    ratio, lead = SEL_BLOCK // CMP_STRIDE, CMP_LEN // CMP_STRIDE - 1
    mm = np.zeros((ns, nc), np.float32)
    for r in range(ratio + lead):
        st = CMP_STRIDE * (r - lead)
        ov = (min(st + CMP_LEN, SEL_BLOCK) - max(st, 0)) / CMP_STRIDE
        for jb in range(ns):
            i = jb * ratio + r - lead
            if 0 <= i < n_cmp:
                mm[jb, i] += ov
    slopes = jnp.exp2(-8.0 * jnp.arange(1, NSA_HEADS + 1, dtype=F32) / NSA_HEADS)
    slopes = jnp.repeat(slopes.reshape(hk, 1, NSA_GROUP), NSA_TQ, axis=2)
    negsel, flags, oc, near = pl.pallas_call(
        functools.partial(_nsa_sel_kernel, n_top=n_top),
        name="nsa_select",
        grid=(batch, hk, nqt),
        in_specs=[
            pl.BlockSpec((None, None, None, NSA_DK, NSA_NQ), lambda b, k, i: (b, k, i, 0, 0)),
            pl.BlockSpec((None, None, nc, NSA_DK), lambda b, k, i: (b, k, 0, 0)),
            pl.BlockSpec((nc, FEAT), lambda b, k, i: (0, 0)),
            pl.BlockSpec((None, None, nc // CMP_CHUNK, NSA_DV, CMP_CHUNK), lambda b, k, i: (b, k, 0, 0, 0)),
            pl.BlockSpec((ns, nc), lambda b, k, i: (0, 0)),
            pl.BlockSpec((None, 1, NSA_NQ), lambda b, k, i: (k, 0, 0)),
        ],
        out_specs=[
            pl.BlockSpec((None, None, None, ns, NSA_TQ), lambda b, k, i: (b, k, i, 0, 0)),
            pl.BlockSpec((None, None, None, seq // SLC_CHUNK, 1), lambda b, k, i: (b, k, i, 0, 0)),
            pl.BlockSpec((None, None, None, NSA_DV, NSA_NQ), lambda b, k, i: (b, k, i, 0, 0)),
            pl.BlockSpec((None, None, None, FEAT // 2, NSA_TQ), lambda b, k, i: (b, k, i, 0, 0)),
        ],
        out_shape=[
            jax.ShapeDtypeStruct((batch, hk, nqt, ns, NSA_TQ), F32),
            jax.ShapeDtypeStruct((batch, hk, nqt, seq // SLC_CHUNK, 1), jnp.int32),
            jax.ShapeDtypeStruct((batch, hk, nqt, NSA_DV, NSA_NQ), F32),
            jax.ShapeDtypeStruct((batch, hk, nqt, FEAT // 2, NSA_TQ), F32),
        ],
        compiler_params=_cparams(("parallel", "parallel", "parallel"), 48),
    )(qt_arr, kc, _key_features(nc, "cmp"), vct, jnp.asarray(mm, BF16), slopes)
    return negsel, flags, oc, near, slopes


def _nsa_attn_kernel(flag_ref, qt_ref, ks_ref, vst_ref, kw_ref, vwt_ref, negsel_ref, near_ref, oc_ref, gt_ref,
                     slope_ref, go_ref, kfs_ref, kfw_ref, kfo_ref, o_ref, m_ref, l_ref, acc_ref,
                     sa_ref, sb_ref, lst_ref):
    qt = pl.program_id(2)
    t0 = qt * NSA_TQ
    t = _query_rows(qt)
    slope = slope_ref[...]
    q = qt_ref[...]
    blocks = SLC_CHUNK // SEL_BLOCK
    a = lax.broadcasted_iota(jnp.int32, (NSA_TQ, 1), 0)
    u = t - t0
    causal = jnp.where(a <= u, 0.0, NEG)
    too_old = jnp.where(a > u, 0.0, NEG)
    dot = functools.partial(jnp.dot, preferred_element_type=F32)
    per_group = lambda rows8: jnp.concatenate([rows8] * NSA_GROUP, axis=1)
    r8 = lax.broadcasted_iota(jnp.int32, (FEAT // 2, NSA_TQ), 0)
    slope2 = slope * LOG2E
    alibi8 = _alibi_rows(slope, 1, SEL_BLOCK)

    lst_ref[0] = 0

    def listing(c, n):
        lst_ref[n] = c
        return n + (flag_ref[c, 0] > 0).astype(jnp.int32)

    n = lax.fori_loop(0, (t0 + SLC_CHUNK - 1) // SLC_CHUNK, listing, 0)
    lst_ref[n] = lst_ref[jnp.maximum(n - 1, 0)]

    def scores(c, dst_ref):
        k0 = pl.multiple_of(c * SLC_CHUNK, SLC_CHUNK)
        neg8 = negsel_ref[pl.ds(pl.multiple_of(c * blocks, blocks), blocks), :]
        neg8 = jnp.where((r8 < FIRST_KEYS // SEL_BLOCK) & (c == 0), NEG, neg8)
        lhs = jnp.concatenate([ks_ref[pl.ds(k0, SLC_CHUNK), :], kfs_ref[...]], axis=1)
        dst_ref[...] = dot(lhs, _aug_rhs(q, per_group(neg8), alibi8))

    scores(lst_ref[0], sa_ref)

    lhs = jnp.concatenate(
        [jnp.concatenate([ks_ref[pl.ds(pl.multiple_of(t0, NSA_TQ), NSA_TQ), :], ks_ref[:FIRST_KEYS, :]], axis=0),
         kfo_ref[...]], axis=1)
    s = dot(lhs, _aug_rhs(q, per_group(near_ref[...]), alibi8))
    s_own, s_first = s[:NSA_TQ] + causal, s[NSA_TQ:]
    first_off = (-t0).astype(F32) * slope2
    m = jnp.maximum(jnp.max(s_own, axis=0, keepdims=True), jnp.max(s_first, axis=0, keepdims=True) + first_off)
    p = jnp.concatenate([jnp.exp2(s_own - m), jnp.exp2(s_first - (m - first_off))], axis=0)
    m_ref[...] = m
    l_ref[...] = jnp.sum(p, axis=0, keepdims=True)
    acc_ref[...] = dot(jnp.concatenate([vst_ref[qt], vst_ref[0][:, :FIRST_KEYS]], axis=1), p.astype(BF16))

    wsub = WIN_KEYS // NSA_TQ
    before_start = jnp.where(r8 < WINDOW // NSA_TQ - qt, NEG, 0.0)
    lhs = jnp.concatenate([kw_ref[pl.ds(pl.multiple_of(t0, NSA_TQ), WIN_KEYS), :], kfw_ref[...]], axis=1)
    sw = dot(lhs, _aug_rhs(q, per_group(before_start), alibi8))
    sw = jnp.concatenate([sw[:NSA_TQ] + too_old, sw[NSA_TQ:WINDOW], sw[WINDOW:] + causal], axis=0)
    pw = jnp.exp2(sw - jnp.max(sw, axis=0, keepdims=True))
    vt = jnp.concatenate([vwt_ref[qt + i] for i in range(wsub)], axis=1)
    o_w = dot(vt, pw.astype(BF16)) / jnp.sum(pw, axis=0, keepdims=True)

    sub = SLC_CHUNK // NSA_TQ

    def update(c, src_ref):
        s = src_ref[...]
        off = (c * SLC_CHUNK - t0).astype(F32) * slope2
        m_old = m_ref[...]
        m_new = jnp.maximum(m_old, jnp.max(s, axis=0, keepdims=True) + off)
        p = jnp.exp2(s - (m_new - off))
        alpha = jnp.exp2(m_old - m_new)
        l_ref[...] = alpha * l_ref[...] + jnp.sum(p, axis=0, keepdims=True)
        vt = jnp.concatenate([vst_ref[c * sub + i] for i in range(sub)], axis=1)
        acc_ref[...] = alpha * acc_ref[...] + dot(vt, p.astype(BF16))
        m_ref[...] = m_new

    def pair(tp, carry):
        i = 2 * tp
        scores(lst_ref[i + 1], sb_ref)
        update(lst_ref[i], sa_ref)

        @pl.when(i + 1 < n)
        def _():
            scores(lst_ref[i + 2], sa_ref)
            update(lst_ref[i + 1], sb_ref)

        return carry

    lax.fori_loop(0, (n + 1) // 2, pair, 0)

    def gate(br):
        return jnp.concatenate([gt_ref[br * NSA_GROUP + g:br * NSA_GROUP + g + 1, :]
                                for g in range(NSA_GROUP)], axis=1)

    o = gate(0) * oc_ref[...] + gate(1) * (acc_ref[...] / l_ref[...]) + gate(2) * o_w
    o = o * lax.rsqrt(jnp.mean(o * o, axis=0, keepdims=True) + EPS) * go_ref[...]
    for g in range(NSA_GROUP):
        sl = slice(g * NSA_TQ, (g + 1) * NSA_TQ)
        o_ref[:, g * NSA_DV:(g + 1) * NSA_DV] = o[:, sl].T.astype(o_ref.dtype)


def _nsa_attend(flags, qt_arr, ks, vst, kw, vwt, negsel, near, oc, gates_t, slopes, g_nsa_out, batch, seq):
    hk = NSA_KV_HEADS
    nqt = seq // NSA_TQ
    ns = seq // SEL_BLOCK
    per_tile = lambda shape: pl.BlockSpec((None, None, None) + shape, lambda b, k, i: (b, k, i, 0, 0))
    tok = lambda n: pl.BlockSpec((None, None, n, NSA_DK), lambda b, k, i: (b, k, 0, 0))
    tr = lambda n: pl.BlockSpec((None, None, n, NSA_DV, NSA_TQ), lambda b, k, i: (b, k, 0, 0, 0))
    wpad = WINDOW // NSA_TQ
    kw = jnp.pad(kw, ((0, 0), (0, 0), (WINDOW, 0), (0, 0)))
    vwt = jnp.pad(vwt, ((0, 0), (0, 0), (wpad, 0), (0, 0), (0, 0)))
    return pl.pallas_call(
        _nsa_attn_kernel,
        name="nsa_attend",
        grid=(batch, hk, nqt),
        in_specs=[
            pl.BlockSpec((None, None, None, seq // SLC_CHUNK, 1), lambda b, k, i: (b, k, i, 0, 0),
                         memory_space=pltpu.SMEM),
            per_tile((NSA_DK, NSA_NQ)),
            tok(seq), tr(nqt), tok(seq + WINDOW), tr(nqt + wpad),
            per_tile((ns, NSA_TQ)),
            per_tile((FEAT // 2, NSA_TQ)),
            per_tile((NSA_DV, NSA_NQ)),
            pl.BlockSpec((None, GATE_ROWS, NSA_TQ), lambda b, k, i: (b, k, i)),
            pl.BlockSpec((None, 1, NSA_NQ), lambda b, k, i: (k, 0, 0)),
            pl.BlockSpec((NSA_DV, 1), lambda b, k, i: (0, 0)),
            pl.BlockSpec((SLC_CHUNK, FEAT), lambda b, k, i: (0, 0)),
            pl.BlockSpec((WIN_KEYS, FEAT), lambda b, k, i: (0, 0)),
            pl.BlockSpec((NSA_TQ + FIRST_KEYS, FEAT), lambda b, k, i: (0, 0)),
        ],
        out_specs=pl.BlockSpec((NSA_TQ, NSA_GROUP * NSA_DV), lambda b, k, i: (b * nqt + i, k)),
        out_shape=jax.ShapeDtypeStruct((batch * seq, D_NSA_OUT), BF16),
        scratch_shapes=[pltpu.VMEM((1, NSA_NQ), F32), pltpu.VMEM((1, NSA_NQ), F32),
                        pltpu.VMEM((NSA_DV, NSA_NQ), F32),
                        pltpu.VMEM((SLC_CHUNK, NSA_NQ), F32), pltpu.VMEM((SLC_CHUNK, NSA_NQ), F32),
                        pltpu.SMEM((seq // SLC_CHUNK + 1,), jnp.int32)],
        compiler_params=_cparams(("parallel", "parallel", "arbitrary"), 48),
    )(flags, qt_arr, ks, vst, kw, vwt, negsel, near, oc, gates_t, slopes, g_nsa_out.reshape(NSA_DV, 1),
      _key_features(SLC_CHUNK, "slc"), _key_features(WIN_KEYS, "win"),
      _key_features(NSA_TQ + FIRST_KEYS, "own"))


def _nsa(proj, b_gate, g_q, g_kc, g_ks, g_kw, pe_k, pe_v, w_ck1, w_ck2, w_cv1, w_cv2, g_nsa_out,
         batch, seq):
    assert seq % (CMP_CHUNK * CMP_STRIDE) == 0 and seq >= WIN_KEYS
    qt_arr, k_tok, v_tok, ks, vst, kw, vwt, gates_t = _nsa_prep(proj, g_q, g_ks, g_kw, b_gate, batch, seq)
    kc, vct = _compress(k_tok, v_tok, pe_k, pe_v, w_ck1, w_ck2, w_cv1, w_cv2, g_kc)
    negsel, flags, oc, near, slopes = _nsa_select(qt_arr, kc, vct, batch, seq)
    return _nsa_attend(flags, qt_arr, ks, vst, kw, vwt, negsel, near, oc, gates_t, slopes, g_nsa_out,
                       batch, seq)


def kernel(x, mem, g_mix, w_in, b_nsa_gate, g_q, g_kc, g_ks, g_kw, pe_k, pe_v, w_ck1, w_ck2,
           w_cv1, w_cv2, g_nsa_out, w_gk2, b_gk, g_gla_out, w_out, g_cross, g_mem, w_cq, w_ck,
           w_cv, g_cq, g_ck, w_co, g_ffn, w_gu, w_down):
    batch, seq, _ = x.shape
    x2d = x.reshape(batch * seq, D_MODEL)
    for l in range(w_in.shape[0]):
        proj = _in_proj(x2d, g_mix[l], *_pack_w_in(w_in[l]))
        o_gla = _gla(proj, w_gk2[l], b_gk[l], g_gla_out[l], batch, seq)
        o_nsa = _nsa(proj, b_nsa_gate[l], g_q[l], g_kc[l], g_ks[l], g_kw[l], pe_k[l], pe_v[l],
                     w_ck1[l], w_ck2[l], w_cv1[l], w_cv2[l], g_nsa_out[l], batch, seq)
        ck, cv = _mem_kv(mem, g_mem[l], w_ck[l], w_cv[l], g_ck[l])
        x2d = _out_cross(x2d, o_nsa, o_gla, w_out[l], g_cross[l], w_cq[l], g_cq[l], ck, cv,
                         w_co[l], seq)
        x2d = _ffn(x2d, g_ffn[l], w_gu[l], w_down[l])
    return x2d.reshape(batch, seq, D_MODEL)
```

```python
import functools

import numpy as np
import jax
import jax.numpy as jnp
from jax import lax
from jax.experimental import pallas as pl
from jax.experimental.pallas import tpu as pltpu

F32 = jnp.float32
BF16 = jnp.bfloat16

D_MODEL = 2048
EPS = 1e-6
NSA_HEADS = 8
NSA_KV_HEADS = 2
NSA_GROUP = NSA_HEADS // NSA_KV_HEADS
NSA_DK = 128
NSA_DV = 128
CMP_LEN = 32
CMP_STRIDE = 16
CMP_HIDDEN = 256
SEL_BLOCK = 64
SEL_TOP = 16
WINDOW = 512
GLA_HEADS = 4
GLA_DK = 128
GLA_DV = 256
GLA_GATE_RANK = 16
GLA_GATE_NORM = 16.0
GLA_CHUNK = 64
MEM_HEADS = 4
MEM_DH = 128
D_FF = -(-8 * D_MODEL // (3 * 256)) * 256
D_NSA_OUT = NSA_HEADS * NSA_DV
D_GLA_OUT = GLA_HEADS * GLA_DV

LANES = 128

COL_Q = 0
COL_KV = COL_Q + NSA_HEADS * NSA_DK
COL_QL = COL_KV + 6 * NSA_KV_HEADS * NSA_DK
COL_KL = COL_QL + GLA_HEADS * GLA_DK
COL_VL = COL_KL + GLA_HEADS * GLA_DK
COL_RL = COL_VL + GLA_HEADS * GLA_DV
COL_MISC = COL_RL + GLA_HEADS * GLA_DV
MISC_AL = 32
D_IN_PACKED = COL_MISC + LANES


def _cparams(semantics, vmem_mb):
    return pltpu.CompilerParams(dimension_semantics=semantics,
                                vmem_limit_bytes=vmem_mb * 1024 * 1024)


def _rms(u, g):
    return u * lax.rsqrt(jnp.mean(u * u, axis=-1, keepdims=True) + EPS) * g


def _inproj_kernel(x_ref, g_ref, w_nsa_ref, w_rest_ref, o_ref):
    h = _rms(x_ref[...], g_ref[...]).astype(BF16)
    o_ref[:, :COL_QL] = jnp.dot(h, w_nsa_ref[...], preferred_element_type=F32)
    o_ref[:, COL_QL:] = jnp.dot(h, w_rest_ref[...], preferred_element_type=F32)


def _in_proj(x2d, g_mix, w_in, w_rest, tm=256):
    n = x2d.shape[0]
    resident = lambda cols: pl.BlockSpec((D_MODEL, cols), lambda i: (0, 0), pipeline_mode=pl.Buffered(1))
    return pl.pallas_call(
        _inproj_kernel,
        name="in_proj",
        grid=(n // tm,),
        in_specs=[
            pl.BlockSpec((tm, D_MODEL), lambda i: (i, 0)),
            pl.BlockSpec((1, D_MODEL), lambda i: (0, 0)),
            resident(COL_QL),
            resident(D_IN_PACKED - COL_QL),
        ],
        out_specs=pl.BlockSpec((tm, D_IN_PACKED), lambda i: (i, 0)),
        out_shape=jax.ShapeDtypeStruct((n, D_IN_PACKED), F32),
        compiler_params=_cparams(("parallel",), 48),
    )(x2d, g_mix.reshape(1, D_MODEL), w_in, w_rest)


GATE_ROWS = 16


def _gate_layout(u):
    pos = np.arange(MISC_AL)
    k, r = pos // GATE_ROWS, pos % GATE_ROWS
    br, g = r // NSA_GROUP, r % NSA_GROUP
    used = r < 3 * NSA_GROUP
    src = np.where(used, (k * NSA_GROUP + g) * 3 + br, 0)
    return jnp.where(jnp.asarray(used), jnp.take(u, jnp.asarray(src), axis=-1), 0.0)


def _pack_w_in(w_in):
    w_in = w_in.astype(BF16)
    sizes = (NSA_HEADS * NSA_DK,) + (NSA_KV_HEADS * NSA_DK,) * 6 + (3 * NSA_HEADS,) + (
        GLA_HEADS * GLA_DK, GLA_HEADS * GLA_DK, GLA_HEADS * GLA_DV, GLA_GATE_RANK, GLA_HEADS * GLA_DV)
    offs = np.concatenate([[0], np.cumsum(sizes)])
    assert offs[7] == COL_QL
    seg = lambda i: w_in[:, offs[i]:offs[i + 1]]
    zeros = lambda n: jnp.zeros((D_MODEL, n), w_in.dtype)
    misc = jnp.concatenate([_gate_layout(seg(7)), seg(11),
                            zeros(LANES - MISC_AL - GLA_GATE_RANK)], axis=1)
    return w_in, jnp.concatenate([seg(8), seg(9), seg(10), seg(12), misc], axis=1)


def _memkv_kernel(mem_ref, gm_ref, wk_ref, wv_ref, gk_ref, ck_ref, cv_ref):
    hm = _rms(mem_ref[...], gm_ref[...]).astype(BF16)
    k = jnp.dot(hm, wk_ref[...], preferred_element_type=F32)
    v = jnp.dot(hm, wv_ref[...], preferred_element_type=F32)
    for h in range(MEM_HEADS):
        sl = slice(h * MEM_DH, (h + 1) * MEM_DH)
        ck_ref[h] = _rms(k[:, sl], gk_ref[...]).astype(BF16)
        cv_ref[h] = v[:, sl].astype(BF16)


def _mem_kv(mem, g_mem, w_ck, w_cv, g_ck):
    b, m, _ = mem.shape
    dm = MEM_HEADS * MEM_DH
    out = jax.ShapeDtypeStruct((b, MEM_HEADS, m, MEM_DH), BF16)
    return pl.pallas_call(
        _memkv_kernel,
        name="mem_kv",
        grid=(b,),
        in_specs=[
            pl.BlockSpec((None, m, D_MODEL), lambda i: (i, 0, 0)),
            pl.BlockSpec((1, D_MODEL), lambda i: (0, 0)),
            pl.BlockSpec((D_MODEL, dm), lambda i: (0, 0)),
            pl.BlockSpec((D_MODEL, dm), lambda i: (0, 0)),
            pl.BlockSpec((1, MEM_DH), lambda i: (0, 0)),
        ],
        out_specs=[pl.BlockSpec((None, MEM_HEADS, m, MEM_DH), lambda i: (i, 0, 0, 0))] * 2,
        out_shape=[out, out],
        compiler_params=_cparams(("parallel",), 32),
    )(mem, g_mem.reshape(1, D_MODEL), w_ck.astype(BF16), w_cv.astype(BF16), g_ck.reshape(1, MEM_DH))


def _outx_kernel(x_ref, nsa_ref, gla_ref, wo1_ref, wo2_ref, gc_ref, wcq_ref, gcq_ref,
                 ck_ref, cv_ref, wco_ref, o_ref):
    x1 = (x_ref[...]
          + jnp.dot(nsa_ref[...], wo1_ref[...], preferred_element_type=F32)
          + jnp.dot(gla_ref[...], wo2_ref[...], preferred_element_type=F32))
    hq = _rms(x1, gc_ref[...]).astype(BF16)
    cq = jnp.dot(hq, wcq_ref[...], preferred_element_type=F32)
    outs = []
    for h in range(MEM_HEADS):
        c = _rms(cq[:, h * MEM_DH:(h + 1) * MEM_DH], gcq_ref[...]) * (MEM_DH ** -0.5)
        s = lax.dot_general(c.astype(BF16), ck_ref[h], (((1,), (1,)), ((), ())),
                            preferred_element_type=F32)
        p = jnp.exp(s - jnp.max(s, axis=-1, keepdims=True))
        p = p / jnp.sum(p, axis=-1, keepdims=True)
        outs.append(jnp.dot(p.astype(BF16), cv_ref[h], preferred_element_type=F32))
    oc = jnp.concatenate(outs, axis=-1).astype(BF16)
    o_ref[...] = x1 + jnp.dot(oc, wco_ref[...], preferred_element_type=F32)


def _out_cross(x2d, o_nsa, o_gla, w_out, g_cross, w_cq, g_cq, ck, cv, w_co, seq, tm=512):
    n = x2d.shape[0]
    tiles_per_batch = seq // tm
    dm = MEM_HEADS * MEM_DH
    m = ck.shape[2]
    full = lambda shape: pl.BlockSpec(shape, lambda i: (0,) * len(shape))
    kv_spec = pl.BlockSpec((None, MEM_HEADS, m, MEM_DH), lambda i: (i // tiles_per_batch, 0, 0, 0))
    w_out = w_out.astype(BF16)
    return pl.pallas_call(
        _outx_kernel,
        name="out_cross",
        grid=(n // tm,),
        in_specs=[
            pl.BlockSpec((tm, D_MODEL), lambda i: (i, 0)),
            pl.BlockSpec((tm, D_NSA_OUT), lambda i: (i, 0)),
            pl.BlockSpec((tm, D_GLA_OUT), lambda i: (i, 0)),
            pl.BlockSpec((D_NSA_OUT, D_MODEL), lambda i: (0, 0)),
            pl.BlockSpec((D_GLA_OUT, D_MODEL), lambda i: (D_NSA_OUT // D_GLA_OUT, 0)),
            full((1, D_MODEL)),
            full((D_MODEL, dm)),
            full((1, MEM_DH)),
            kv_spec,
            kv_spec,
            full((dm, D_MODEL)),
        ],
        out_specs=pl.BlockSpec((tm, D_MODEL), lambda i: (i, 0)),
        out_shape=jax.ShapeDtypeStruct((n, D_MODEL), F32),
        compiler_params=_cparams(("parallel",), 56),
    )(x2d, o_nsa, o_gla, w_out, w_out, g_cross.reshape(1, D_MODEL), w_cq.astype(BF16),
      g_cq.reshape(1, MEM_DH), ck, cv, w_co.astype(BF16))


def _ffn_kernel(x_ref, g_ref, wg_ref, wu_ref, wd_ref, o_ref, h_ref):
    @pl.when(pl.program_id(1) == 0)
    def _():
        x = x_ref[...]
        h_ref[...] = _rms(x, g_ref[...]).astype(BF16)
        o_ref[...] = x

    h = h_ref[...]
    g = jnp.dot(h, wg_ref[...], preferred_element_type=F32)
    u = jnp.dot(h, wu_ref[...], preferred_element_type=F32)
    a = (g * jax.nn.sigmoid(g) * u).astype(BF16)
    o_ref[...] += jnp.dot(a, wd_ref[...], preferred_element_type=F32)


def _ffn(x2d, g_ffn, w_gu, w_down, tm=512, tf=512):
    n = x2d.shape[0]
    w_gu = w_gu.astype(BF16)
    return pl.pallas_call(
        _ffn_kernel,
        name="ffn",
        grid=(n // tm, D_FF // tf),
        in_specs=[
            pl.BlockSpec((tm, D_MODEL), lambda i, j: (i, 0)),
            pl.BlockSpec((1, D_MODEL), lambda i, j: (0, 0)),
            pl.BlockSpec((D_MODEL, tf), lambda i, j: (0, j)),
            pl.BlockSpec((D_MODEL, tf), lambda i, j: (0, D_FF // tf + j)),
            pl.BlockSpec((tf, D_MODEL), lambda i, j: (j, 0)),
        ],
        out_specs=pl.BlockSpec((tm, D_MODEL), lambda i, j: (i, 0)),
        out_shape=jax.ShapeDtypeStruct((n, D_MODEL), F32),
        scratch_shapes=[pltpu.VMEM((tm, D_MODEL), BF16)],
        compiler_params=_cparams(("parallel", "arbitrary"), 48),
    )(x2d, g_ffn.reshape(1, D_MODEL), w_gu, w_gu, w_down.astype(BF16))


def _split3(x):
    hi = x.astype(BF16)
    r = x - hi.astype(F32)
    mid = r.astype(BF16)
    lo = (r - mid.astype(F32)).astype(BF16)
    return hi, mid, lo


def _dot_exact_lhs(a_bf16, x):
    return sum(jnp.dot(a_bf16, p, preferred_element_type=F32) for p in _split3(x))


def _dot_split(a, w):
    a_hi, a_lo, _ = _split3(a)
    w_hi, w_lo, _ = _split3(w)
    d = lambda p, q: jnp.dot(p, q, preferred_element_type=F32)
    return d(a_hi, w_hi) + d(a_hi, w_lo) + d(a_lo, w_hi)


_NT = (((1,), (1,)), ((), ()))
_TN = (((0,), (0,)), ((), ()))
GLA_DIRECT = 8


def _gla_kernel(ql_ref, kl_ref, vl_ref, rl_ref, misc_ref, w2_ref, bg_ref, go_ref, band_ref,
                o_ref, st_ref):
    c_len = GLA_CHUNK
    tc = ql_ref.shape[0]
    n_chunks = tc // c_len

    @pl.when(pl.program_id(2) == 0)
    def _():
        st_ref[...] = jnp.zeros_like(st_ref)

    x = _dot_split(misc_ref[...], w2_ref[...]) + bg_ref[...]
    la = (jnp.minimum(x, 0.0) - jnp.log(1.0 + jnp.exp(-jnp.abs(x)))) * (1.0 / GLA_GATE_NORM)
    pos = lax.broadcasted_iota(jnp.int32, (tc, 1), 0) & (c_len - 1)
    b = la
    sh = 1
    while sh < c_len:
        b = b + jnp.where(pos >= sh, pltpu.roll(b, sh, 0), 0.0)
        sh *= 2
    q = ql_ref[...] * (GLA_DK ** -0.5)
    k = kl_ref[...]
    b3 = b.reshape(n_chunks, c_len, GLA_DK)

    def chunk_row(r, n):
        return jnp.broadcast_to(b3[:, r:r + 1, :], (n_chunks, n, GLA_DK))

    level_q, level_k, level_s = [], [], []
    s = c_len
    while s > GLA_DIRECT:
        half = s // 2
        ref = jnp.concatenate([chunk_row(m0 + half - 1, s) for m0 in range(0, c_len, s)],
                              axis=1).reshape(tc, GLA_DK)
        second = (pos & (s - 1)) >= half
        level_q.append(jnp.where(second, q * jnp.exp(jnp.minimum(b - ref, 0.0)), 0.0).astype(BF16))
        level_k.append(jnp.where(second, 0.0, k * jnp.exp(jnp.minimum(ref - b, 0.0))).astype(BF16))
        level_s.append(s)
        s = half
    sublanes = 8
    group = lambda u: u.reshape(tc // sublanes, sublanes, GLA_DK)
    band = jnp.zeros((tc, c_len), F32)
    for d in range(GLA_DIRECT):
        k_sh = k if d == 0 else pltpu.roll(group(k), d, 1).reshape(tc, GLA_DK)
        b_sh = b if d == 0 else pltpu.roll(group(b), d, 1).reshape(tc, GLA_DK)
        cd = jnp.sum(q * k_sh * jnp.exp(jnp.minimum(b - b_sh, 0.0)), axis=-1, keepdims=True)
        band = band + cd * band_ref[d]
    b_end = chunk_row(c_len - 1, c_len).reshape(tc, GLA_DK)
    q_dec = (q * jnp.exp(b)).astype(BF16)
    k_dec = (k * jnp.exp(b_end - b)).astype(BF16)
    s_dec = jnp.exp(b_end)
    v = vl_ref[...].astype(BF16)
    row = lax.broadcasted_iota(jnp.int32, (c_len, 1), 0)
    col = lax.broadcasted_iota(jnp.int32, (1, c_len), 1)

    st = st_ref[...]
    for c in range(n_chunks):
        sl = slice(c * c_len, (c + 1) * c_len)
        attn = band[sl]
        for qs, ks, s in zip(level_q, level_k, level_s):
            a_s = lax.dot_general(qs[sl], ks[sl], _NT, preferred_element_type=F32)
            attn = attn + (a_s if s == c_len else jnp.where((row // s) == (col // s), a_s, 0.0))
        o = (jnp.dot(attn.astype(BF16), v[sl], preferred_element_type=F32)
             + lax.dot_general(q_dec[sl], st.astype(BF16), _NT, preferred_element_type=F32))
        st = (st * s_dec[c * c_len:c * c_len + 1]
              + lax.dot_general(v[sl], k_dec[sl], _TN, preferred_element_type=F32))
        r = rl_ref[sl, :]
        o_ref[sl, :] = (_rms(o, go_ref[...]) * (r * jax.nn.sigmoid(r))).astype(o_ref.dtype)
    st_ref[...] = st


def _gla(proj, w_gk2, b_gk, g_gla_out, batch, seq, tc=2048):
    n = proj.shape[0]
    nt = seq // tc
    w2 = jnp.zeros((LANES, GLA_HEADS * GLA_DK), F32).at[MISC_AL:MISC_AL + GLA_GATE_RANK].set(w_gk2)
    i_pos = np.arange(tc)[:, None] % GLA_CHUNK
    d_off = np.arange(GLA_DIRECT)[:, None, None]
    band = ((np.arange(GLA_CHUNK)[None, :] == i_pos - d_off) & (i_pos % GLA_DIRECT >= d_off))
    band = jnp.asarray(band, F32)
    rows = lambda b, h, i: b * nt + i
    return pl.pallas_call(
        _gla_kernel,
        name="gla",
        grid=(batch, GLA_HEADS, nt),
        in_specs=[
            pl.BlockSpec((tc, GLA_DK), lambda b, h, i: (rows(b, h, i), COL_QL // GLA_DK + h)),
            pl.BlockSpec((tc, GLA_DK), lambda b, h, i: (rows(b, h, i), COL_KL // GLA_DK + h)),
            pl.BlockSpec((tc, GLA_DV), lambda b, h, i: (rows(b, h, i), COL_VL // GLA_DV + h)),
            pl.BlockSpec((tc, GLA_DV), lambda b, h, i: (rows(b, h, i), COL_RL // GLA_DV + h)),
            pl.BlockSpec((tc, LANES), lambda b, h, i: (rows(b, h, i), COL_MISC // LANES)),
            pl.BlockSpec((LANES, GLA_DK), lambda b, h, i: (0, h)),
            pl.BlockSpec((1, GLA_DK), lambda b, h, i: (0, h)),
            pl.BlockSpec((1, GLA_DV), lambda b, h, i: (0, 0)),
            pl.BlockSpec((GLA_DIRECT, tc, GLA_CHUNK), lambda b, h, i: (0, 0, 0)),
        ],
        out_specs=pl.BlockSpec((tc, GLA_DV), lambda b, h, i: (rows(b, h, i), h)),
        out_shape=jax.ShapeDtypeStruct((n, D_GLA_OUT), BF16),
        scratch_shapes=[pltpu.VMEM((GLA_DV, GLA_DK), F32)],
        compiler_params=_cparams(("parallel", "parallel", "arbitrary"), 32),
    )(proj, proj, proj, proj, proj, w2, b_gk.reshape(1, -1), g_gla_out.reshape(1, GLA_DV), band)


NSA_TQ = 256
NSA_NQ = NSA_TQ * NSA_GROUP
NEG = -1e30
SLC_CHUNK = 512
WIN_KEYS = WINDOW + NSA_TQ
FIRST_KEYS = SEL_BLOCK
CMP_CHUNK = 128
FEAT = 16


def _nsa_prep_kernel(q_ref, slc_ref, win_ref, misc_ref, gq_ref, gks_ref, gkw_ref, bg_ref,
                     qt_ref, ks_ref, vst_ref, kw_ref, vwt_ref, gt_ref):
    q = q_ref[...]
    for k in range(NSA_KV_HEADS):
        for g in range(NSA_GROUP):
            h = k * NSA_GROUP + g
            qn = _rms(q[:, h * NSA_DK:(h + 1) * NSA_DK], gq_ref[...]) * (NSA_DK ** -0.5 * LOG2E)
            qt_ref[k, :, g * NSA_TQ:(g + 1) * NSA_TQ] = qn.T.astype(BF16)
        ksl = slice(k * NSA_DK, (k + 1) * NSA_DK)
        vsl = slice((NSA_KV_HEADS + k) * NSA_DK, (NSA_KV_HEADS + k + 1) * NSA_DK)
        ks_ref[k] =_rms(slc_ref[:, ksl], gks_ref[...]).astype(BF16)
        vst_ref[k] = slc_ref[:, vsl].T.astype(BF16)
        kw_ref[k] = _rms(win_ref[:, ksl], gkw_ref[...]).astype(BF16)
        vwt_ref[k] = win_ref[:, vsl].T.astype(BF16)
    gates = jax.nn.sigmoid(misc_ref[...] + bg_ref[...])
    gt_ref[...] = gates.T[:NSA_KV_HEADS * GATE_ROWS, :]


def _nsa_prep(proj, g_q, g_ks, g_kw, b_gate, batch, seq):
    nqt = seq // NSA_TQ
    hk = NSA_KV_HEADS
    pair = 2 * hk * NSA_DK
    rows = lambda b, i: b * nqt + i
    vec = lambda n: pl.BlockSpec((1, n), lambda b, i: (0, 0))
    bias = jnp.zeros((1, LANES), F32).at[0, :MISC_AL].set(_gate_layout(b_gate))
    tok = jax.ShapeDtypeStruct((batch, hk, seq, NSA_DK), BF16)
    tr = jax.ShapeDtypeStruct((batch, hk, nqt, NSA_DV, NSA_TQ), BF16)
    tok_spec = pl.BlockSpec((None, hk, NSA_TQ, NSA_DK), lambda b, i: (b, 0, i, 0))
    tr_spec = pl.BlockSpec((None, hk, None, NSA_DV, NSA_TQ), lambda b, i: (b, 0, i, 0, 0))
    return pl.pallas_call(
        _nsa_prep_kernel,
        name="nsa_prep",
        grid=(batch, nqt),
        in_specs=[
            pl.BlockSpec((NSA_TQ, NSA_HEADS * NSA_DK), lambda b, i: (rows(b, i), 0)),
            pl.BlockSpec((NSA_TQ, pair), lambda b, i: (rows(b, i), COL_KV // pair + 1)),
            pl.BlockSpec((NSA_TQ, pair), lambda b, i: (rows(b, i), COL_KV // pair + 2)),
            pl.BlockSpec((NSA_TQ, LANES), lambda b, i: (rows(b, i), COL_MISC // LANES)),
            vec(NSA_DK), vec(NSA_DK), vec(NSA_DK), vec(LANES),
        ],
        out_specs=[
            pl.BlockSpec((None, hk, None, NSA_DK, NSA_NQ), lambda b, i: (b, 0, i, 0, 0)),
            tok_spec, tr_spec, tok_spec, tr_spec,
            pl.BlockSpec((None, hk * GATE_ROWS, NSA_TQ), lambda b, i: (b, 0, i)),
        ],
        out_shape=[
            jax.ShapeDtypeStruct((batch, hk, nqt, NSA_DK, NSA_NQ), BF16),
            tok, tr, tok, tr,
            jax.ShapeDtypeStruct((batch, hk * GATE_ROWS, seq), F32),
        ],
        compiler_params=_cparams(("parallel", "parallel"), 32),
    )(proj, proj, proj, proj, g_q.reshape(1, -1), g_ks.reshape(1, -1), g_kw.reshape(1, -1), bias)


def _compress_kernel(uk_ref, uv_ref, pek_ref, pev_ref, wk1_ref, wk2_ref, wv1_ref, wv2t_ref, gk_ref,
                     kc_ref, vct_ref):
    def hidden(u_ref, pe_ref, w1_ref):
        n = u_ref.shape[0] // CMP_STRIDE
        toks = [u_ref[pl.ds(l, n, stride=CMP_STRIDE), :] for l in range(CMP_STRIDE)]

        def half(h):
            rows = [(toks[l] + pe_ref[h * CMP_STRIDE + l:h * CMP_STRIDE + l + 1, :]).astype(BF16)
                    for l in range(CMP_STRIDE)]
            return jnp.dot(jnp.concatenate(rows, axis=1), w1_ref[h], preferred_element_type=F32)

        hid = half(0) + pltpu.roll(half(1), n - 1, 0)
        return (hid * jax.nn.sigmoid(hid)).astype(BF16)

    kc = jnp.dot(hidden(uk_ref, pek_ref, wk1_ref), wk2_ref[...], preferred_element_type=F32)
    kc_ref[...] = _rms(kc, gk_ref[...]).astype(BF16)
    vct = lax.dot_general(wv2t_ref[...], hidden(uv_ref, pev_ref, wv1_ref), _NT,
                          preferred_element_type=F32).astype(BF16)
    for c in range(vct_ref.shape[0]):
        vct_ref[c] = vct[:, c * CMP_CHUNK:(c + 1) * CMP_CHUNK]


def _compress(proj, pe_k, pe_v, w_ck1, w_ck2, w_cv1, w_cv2, g_kc, batch, seq):
    hk, d = NSA_KV_HEADS, NSA_DK
    nc = seq // CMP_STRIDE
    flat = CMP_STRIDE * d
    halves = CMP_LEN // CMP_STRIDE
    tokens = lambda first: pl.BlockSpec((seq, d), lambda b, k: (b, first + k))
    full = lambda shape: pl.BlockSpec(shape, lambda b, k: (0,) * len(shape))
    return pl.pallas_call(
        _compress_kernel,
        name="nsa_compress",
        grid=(batch, hk),
        in_specs=[tokens(COL_KV // d), tokens(COL_KV // d + hk), full((CMP_LEN, d)), full((CMP_LEN, d)),
                  full((halves, flat, CMP_HIDDEN)), full((CMP_HIDDEN, d)),
                  full((halves, flat, CMP_HIDDEN)), full((d, CMP_HIDDEN)), full((1, d))],
        out_specs=[pl.BlockSpec((None, None, nc, d), lambda b, k: (b, k, 0, 0)),
                   pl.BlockSpec((None, None, nc // CMP_CHUNK, d, CMP_CHUNK),
                                lambda b, k: (b, k, 0, 0, 0))],
        out_shape=[jax.ShapeDtypeStruct((batch, hk, nc, d), BF16),
                   jax.ShapeDtypeStruct((batch, hk, nc // CMP_CHUNK, d, CMP_CHUNK), BF16)],
        compiler_params=_cparams(("parallel", "parallel"), 48),
    )(proj, proj, pe_k, pe_v,
      w_ck1.astype(BF16).reshape(halves, flat, CMP_HIDDEN), w_ck2.astype(BF16),
      w_cv1.astype(BF16).reshape(halves, flat, CMP_HIDDEN), w_cv2.T.astype(BF16), g_kc.reshape(1, d))


def _query_rows(qt):
    lane = lax.broadcasted_iota(jnp.int32, (1, NSA_NQ), 1)
    t = qt * NSA_TQ + (lane & (NSA_TQ - 1))
    return t


def _topk_mask(imp, cur, n_top):
    ns, nt = imp.shape
    j = lax.broadcasted_iota(jnp.int32, (ns, nt), 0)
    forced = (j == 0) | (j == cur) | (j == cur - 1)
    visible = j <= cur
    work = jnp.where(visible & jnp.logical_not(forced), imp, -1.0)

    def extract(_, work):
        m = jnp.max(work, axis=0, keepdims=True)
        cand = (work == m) & (m > -0.5)
        idx = jnp.min(jnp.where(cand, j, ns), axis=0, keepdims=True)
        return jnp.where(j == idx, -2.0, work)

    work = lax.fori_loop(0, n_top - 3, extract, work, unroll=True)
    return (forced & visible) | (work == -2.0)


def _feature_rows(rows):
    n = rows[0].shape[1]
    r = lax.broadcasted_iota(jnp.int32, (FEAT // 2, n), 0)
    out = jnp.zeros((FEAT // 2, n), F32)
    for i, row in enumerate(rows):
        out = jnp.where(r == i, row, out)
    return out


def _aug_rhs(q, mask8, bias8):
    return jnp.concatenate([q, jnp.concatenate([mask8, bias8], axis=0).astype(BF16)], axis=0)


def _key_features(n, kind):
    a = np.arange(n)
    f = np.zeros((n, FEAT), np.float32)
    unit = CMP_CHUNK if kind == "cmp" else SEL_BLOCK
    if kind == "own":
        a = a % NSA_TQ
    f[:, 8:11] = (a % unit)[:, None]
    f[:, 11:14] = (a // unit)[:, None]
    rows = np.arange(n)
    if kind == "slc":
        f[rows, a // SEL_BLOCK] = 1.0
    if kind == "win":
        f[rows, a // NSA_TQ] = 1.0
    if kind == "own":
        f[rows, np.where(rows < NSA_TQ, FIRST_KEYS // SEL_BLOCK, 0) + a // SEL_BLOCK] = 1.0
    return jnp.asarray(f, BF16)


LOG2E = float(np.log2(np.e))


def _alibi_rows(slope, lo_unit, hi_unit):
    pieces = [p.astype(F32) for p in _split3(slope * LOG2E)]
    return _feature_rows([lo_unit * p for p in pieces] + [hi_unit * p for p in pieces])


def _nsa_sel_kernel(qt_ref, kc_ref, kf_ref, vct_ref, mmat_ref, slope_ref, negsel_ref, flag_ref,
                    oc_ref, near_ref, *, n_top):
    qt = pl.program_id(2)
    t = _query_rows(qt)
    slope = slope_ref[...]
    q = qt_ref[...]
    zeros8 = jnp.zeros((FEAT // 2, NSA_NQ), F32)
    per_tile = NSA_TQ // CMP_STRIDE
    n_chunks = (per_tile * qt + per_tile + CMP_CHUNK - 1) // CMP_CHUNK
    rhs = _aug_rhs(q, zeros8, _alibi_rows(slope, CMP_STRIDE, CMP_CHUNK * CMP_STRIDE))

    def branch(n):
        rows = n * CMP_CHUNK
        edge = rows - min(n, 2) * CMP_CHUNK

        def run():
            lhs = jnp.concatenate([kc_ref[:rows, :], kf_ref[:rows, :]], axis=1)
            s = jnp.dot(lhs, rhs, preferred_element_type=F32)
            ci = edge + lax.broadcasted_iota(jnp.int32, (rows - edge, 1), 0)
            tail = jnp.where(t >= ci * CMP_STRIDE + (CMP_LEN - 1), s[edge:], NEG)
            s = tail if edge == 0 else jnp.concatenate([s[:edge], tail], axis=0)
            m = jnp.maximum(jnp.max(s, axis=0, keepdims=True), 0.1 * NEG)
            p = jnp.exp2(s - m)
            l = jnp.sum(p, axis=0, keepdims=True)
            inv = 1.0 / jnp.where(l > 0.0, l, 1.0)
            vt = jnp.concatenate([vct_ref[c] for c in range(n)], axis=1)
            oc = jnp.dot(vt, p.astype(BF16), preferred_element_type=F32) * inv
            p = p * inv
            psum = sum(p[:, g * NSA_TQ:(g + 1) * NSA_TQ] for g in range(NSA_GROUP))
            return oc, _dot_exact_lhs(mmat_ref[:, :rows], psum)

        return run

    oc, imp = lax.switch(n_chunks - 1, [branch(n + 1) for n in range(kc_ref.shape[0] // CMP_CHUNK)])
    oc_ref[...] = oc
    cur = lax.shift_right_logical(t[:, :NSA_TQ], 6)
    sel = _topk_mask(imp, cur, n_top)
    j = lax.broadcasted_iota(jnp.int32, sel.shape, 0)
    per_own = NSA_TQ // SEL_BLOCK
    past = sel & (j < per_own * qt)
    negsel = jnp.where(past, 0.0, NEG)
    negsel_ref[...] = negsel
    own = [jnp.max(jnp.where(sel & (j == per_own * qt + r), 1.0, 0.0), axis=0, keepdims=True)
           for r in range(per_own)]
    near_ref[...] = _feature_rows([negsel[r:r + 1, :] for r in range(FIRST_KEYS // SEL_BLOCK)]
                                  + [jnp.where(o > 0.5, 0.0, NEG) for o in own])
    listed = (past & (j >= FIRST_KEYS // SEL_BLOCK)).astype(F32)
    per_chunk = SLC_CHUNK // SEL_BLOCK
    taken = jnp.max(listed.reshape(listed.shape[0] // per_chunk, per_chunk, NSA_TQ), axis=1)
    flag_ref[...] = jnp.max(taken, axis=1, keepdims=True).astype(jnp.int32)


def _nsa_select(qt_arr, kc, vct, batch, seq):
    hk = NSA_KV_HEADS
    nqt = seq // NSA_TQ
    nc = seq // CMP_STRIDE
    ns = seq // SEL_BLOCK
    n_cmp = nc - (CMP_LEN // CMP_STRIDE - 1)
    n_top = min(SEL_TOP, ns)
    ratio, lead = SEL_BLOCK // CMP_STRIDE, CMP_LEN // CMP_STRIDE - 1
    mm = np.zeros((ns, nc), np.float32)
    for r in range(ratio + lead):
        st = CMP_STRIDE * (r - lead)
        ov = (min(st + CMP_LEN, SEL_BLOCK) - max(st, 0)) / CMP_STRIDE
        for jb in range(ns):
            i = jb * ratio + r - lead
            if 0 <= i < n_cmp:
                mm[jb, i] += ov
    slopes = jnp.exp2(-8.0 * jnp.arange(1, NSA_HEADS + 1, dtype=F32) / NSA_HEADS)
    slopes = jnp.repeat(slopes.reshape(hk, 1, NSA_GROUP), NSA_TQ, axis=2)
    negsel, flags, oc, near = pl.pallas_call(
        functools.partial(_nsa_sel_kernel, n_top=n_top),
        name="nsa_select",
        grid=(batch, hk, nqt),
        in_specs=[
            pl.BlockSpec((None, None, None, NSA_DK, NSA_NQ), lambda b, k, i: (b, k, i, 0, 0)),
            pl.BlockSpec((None, None, nc, NSA_DK), lambda b, k, i: (b, k, 0, 0)),
            pl.BlockSpec((nc, FEAT), lambda b, k, i: (0, 0)),
            pl.BlockSpec((None, None, nc // CMP_CHUNK, NSA_DV, CMP_CHUNK), lambda b, k, i: (b, k, 0, 0, 0)),
            pl.BlockSpec((ns, nc), lambda b, k, i: (0, 0)),
            pl.BlockSpec((None, 1, NSA_NQ), lambda b, k, i: (k, 0, 0)),
        ],
        out_specs=[
            pl.BlockSpec((None, None, None, ns, NSA_TQ), lambda b, k, i: (b, k, i, 0, 0)),
            pl.BlockSpec((None, None, None, seq // SLC_CHUNK, 1), lambda b, k, i: (b, k, i, 0, 0)),
            pl.BlockSpec((None, None, None, NSA_DV, NSA_NQ), lambda b, k, i: (b, k, i, 0, 0)),
            pl.BlockSpec((None, None, None, FEAT // 2, NSA_TQ), lambda b, k, i: (b, k, i, 0, 0)),
        ],
        out_shape=[
            jax.ShapeDtypeStruct((batch, hk, nqt, ns, NSA_TQ), F32),
            jax.ShapeDtypeStruct((batch, hk, nqt, seq // SLC_CHUNK, 1), jnp.int32),
            jax.ShapeDtypeStruct((batch, hk, nqt, NSA_DV, NSA_NQ), F32),
            jax.ShapeDtypeStruct((batch, hk, nqt, FEAT // 2, NSA_TQ), F32),
        ],
        compiler_params=_cparams(("parallel", "parallel", "parallel"), 48),
    )(qt_arr, kc, _key_features(nc, "cmp"), vct, jnp.asarray(mm, BF16), slopes)
    return negsel, flags, oc, near, slopes


def _nsa_attn_kernel(flag_ref, qt_ref, ks_ref, vst_ref, kw_ref, vwt_ref, negsel_ref, near_ref, oc_ref, gt_ref,
                     slope_ref, go_ref, kfs_ref, kfw_ref, kfo_ref, o_ref, m_ref, l_ref, acc_ref,
                     sa_ref, sb_ref, lst_ref):
    qt = pl.program_id(2)
    t0 = qt * NSA_TQ
    t = _query_rows(qt)
    slope = slope_ref[...]
    q = qt_ref[...]
    blocks = SLC_CHUNK // SEL_BLOCK
    a = lax.broadcasted_iota(jnp.int32, (NSA_TQ, 1), 0)
    u = t - t0
    causal = jnp.where(a <= u, 0.0, NEG)
    too_old = jnp.where(a > u, 0.0, NEG)
    dot = functools.partial(jnp.dot, preferred_element_type=F32)
    per_group = lambda rows8: jnp.concatenate([rows8] * NSA_GROUP, axis=1)
    r8 = lax.broadcasted_iota(jnp.int32, (FEAT // 2, NSA_TQ), 0)
    slope2 = slope * LOG2E
    alibi8 = _alibi_rows(slope, 1, SEL_BLOCK)

    lst_ref[0] = 0

    def listing(c, n):
        lst_ref[n] = c
        return n + (flag_ref[c, 0] > 0).astype(jnp.int32)

    n = lax.fori_loop(0, (t0 + SLC_CHUNK - 1) // SLC_CHUNK, listing, 0)
    lst_ref[n] = lst_ref[jnp.maximum(n - 1, 0)]

    def scores(c, dst_ref):
        k0 = pl.multiple_of(c * SLC_CHUNK, SLC_CHUNK)
        neg8 = negsel_ref[pl.ds(pl.multiple_of(c * blocks, blocks), blocks), :]
        neg8 = jnp.where((r8 < FIRST_KEYS // SEL_BLOCK) & (c == 0), NEG, neg8)
        lhs = jnp.concatenate([ks_ref[pl.ds(k0, SLC_CHUNK), :], kfs_ref[...]], axis=1)
        dst_ref[...] = dot(lhs, _aug_rhs(q, per_group(neg8), alibi8))

    scores(lst_ref[0], sa_ref)

    lhs = jnp.concatenate(
        [jnp.concatenate([ks_ref[pl.ds(pl.multiple_of(t0, NSA_TQ), NSA_TQ), :], ks_ref[:FIRST_KEYS, :]], axis=0),
         kfo_ref[...]], axis=1)
    s = dot(lhs, _aug_rhs(q, per_group(near_ref[...]), alibi8))
    s_own, s_first = s[:NSA_TQ] + causal, s[NSA_TQ:]
    first_off = (-t0).astype(F32) * slope2
    m = jnp.maximum(jnp.max(s_own, axis=0, keepdims=True), jnp.max(s_first, axis=0, keepdims=True) + first_off)
    p = jnp.concatenate([jnp.exp2(s_own - m), jnp.exp2(s_first - (m - first_off))], axis=0)
    m_ref[...] = m
    l_ref[...] = jnp.sum(p, axis=0, keepdims=True)
    acc_ref[...] = dot(jnp.concatenate([vst_ref[qt], vst_ref[0][:, :FIRST_KEYS]], axis=1), p.astype(BF16))

    wsub = WIN_KEYS // NSA_TQ
    before_start = jnp.where(r8 < WINDOW // NSA_TQ - qt, NEG, 0.0)
    lhs = jnp.concatenate([kw_ref[pl.ds(pl.multiple_of(t0, NSA_TQ), WIN_KEYS), :], kfw_ref[...]], axis=1)
    sw = dot(lhs, _aug_rhs(q, per_group(before_start), alibi8))
    sw = jnp.concatenate([sw[:NSA_TQ] + too_old, sw[NSA_TQ:WINDOW], sw[WINDOW:] + causal], axis=0)
    pw = jnp.exp2(sw - jnp.max(sw, axis=0, keepdims=True))
    vt = jnp.concatenate([vwt_ref[qt + i] for i in range(wsub)], axis=1)
    o_w = dot(vt, pw.astype(BF16)) / jnp.sum(pw, axis=0, keepdims=True)

    sub = SLC_CHUNK // NSA_TQ

    def update(c, src_ref):
        s = src_ref[...]
        off = (c * SLC_CHUNK - t0).astype(F32) * slope2
        m_old = m_ref[...]
        m_new = jnp.maximum(m_old, jnp.max(s, axis=0, keepdims=True) + off)
        p = jnp.exp2(s - (m_new - off))
        alpha = jnp.exp2(m_old - m_new)
        l_ref[...] = alpha * l_ref[...] + jnp.sum(p, axis=0, keepdims=True)
        vt = jnp.concatenate([vst_ref[c * sub + i] for i in range(sub)], axis=1)
        acc_ref[...] = alpha * acc_ref[...] + dot(vt, p.astype(BF16))
        m_ref[...] = m_new

    def pair(tp, carry):
        i = 2 * tp
        scores(lst_ref[i + 1], sb_ref)
        update(lst_ref[i], sa_ref)

        @pl.when(i + 1 < n)
        def _():
            scores(lst_ref[i + 2], sa_ref)
            update(lst_ref[i + 1], sb_ref)

        return carry

    lax.fori_loop(0, (n + 1) // 2, pair, 0)

    def gate(br):
        return jnp.concatenate([gt_ref[br * NSA_GROUP + g:br * NSA_GROUP + g + 1, :]
                                for g in range(NSA_GROUP)], axis=1)

    o = gate(0) * oc_ref[...] + gate(1) * (acc_ref[...] / l_ref[...]) + gate(2) * o_w
    o = o * lax.rsqrt(jnp.mean(o * o, axis=0, keepdims=True) + EPS) * go_ref[...]
    for g in range(NSA_GROUP):
        sl = slice(g * NSA_TQ, (g + 1) * NSA_TQ)
        o_ref[:, g * NSA_DV:(g + 1) * NSA_DV] = o[:, sl].T.astype(o_ref.dtype)


def _nsa_attend(flags, qt_arr, ks, vst, kw, vwt, negsel, near, oc, gates_t, slopes, g_nsa_out, batch, seq):
    hk = NSA_KV_HEADS
    nqt = seq // NSA_TQ
    ns = seq // SEL_BLOCK
    per_tile = lambda shape: pl.BlockSpec((None, None, None) + shape, lambda b, k, i: (b, k, i, 0, 0))
    tok = lambda n: pl.BlockSpec((None, None, n, NSA_DK), lambda b, k, i: (b, k, 0, 0))
    tr = lambda n: pl.BlockSpec((None, None, n, NSA_DV, NSA_TQ), lambda b, k, i: (b, k, 0, 0, 0))
    wpad = WINDOW // NSA_TQ
    kw = jnp.pad(kw, ((0, 0), (0, 0), (WINDOW, 0), (0, 0)))
    vwt = jnp.pad(vwt, ((0, 0), (0, 0), (wpad, 0), (0, 0), (0, 0)))
    return pl.pallas_call(
        _nsa_attn_kernel,
        name="nsa_attend",
        grid=(batch, hk, nqt),
        in_specs=[
            pl.BlockSpec((None, None, None, seq // SLC_CHUNK, 1), lambda b, k, i: (b, k, i, 0, 0),
                         memory_space=pltpu.SMEM),
            per_tile((NSA_DK, NSA_NQ)),
            tok(seq), tr(nqt), tok(seq + WINDOW), tr(nqt + wpad),
            per_tile((ns, NSA_TQ)),
            per_tile((FEAT // 2, NSA_TQ)),
            per_tile((NSA_DV, NSA_NQ)),
            pl.BlockSpec((None, GATE_ROWS, NSA_TQ), lambda b, k, i: (b, k, i)),
            pl.BlockSpec((None, 1, NSA_NQ), lambda b, k, i: (k, 0, 0)),
            pl.BlockSpec((NSA_DV, 1), lambda b, k, i: (0, 0)),
            pl.BlockSpec((SLC_CHUNK, FEAT), lambda b, k, i: (0, 0)),
            pl.BlockSpec((WIN_KEYS, FEAT), lambda b, k, i: (0, 0)),
            pl.BlockSpec((NSA_TQ + FIRST_KEYS, FEAT), lambda b, k, i: (0, 0)),
        ],
        out_specs=pl.BlockSpec((NSA_TQ, NSA_GROUP * NSA_DV), lambda b, k, i: (b * nqt + i, k)),
        out_shape=jax.ShapeDtypeStruct((batch * seq, D_NSA_OUT), BF16),
        scratch_shapes=[pltpu.VMEM((1, NSA_NQ), F32), pltpu.VMEM((1, NSA_NQ), F32),
                        pltpu.VMEM((NSA_DV, NSA_NQ), F32),
                        pltpu.VMEM((SLC_CHUNK, NSA_NQ), F32), pltpu.VMEM((SLC_CHUNK, NSA_NQ), F32),
                        pltpu.SMEM((seq // SLC_CHUNK + 1,), jnp.int32)],
        compiler_params=_cparams(("parallel", "parallel", "arbitrary"), 48),
    )(flags, qt_arr, ks, vst, kw, vwt, negsel, near, oc, gates_t, slopes, g_nsa_out.reshape(NSA_DV, 1),
      _key_features(SLC_CHUNK, "slc"), _key_features(WIN_KEYS, "win"),
      _key_features(NSA_TQ + FIRST_KEYS, "own"))


def _nsa(proj, b_gate, g_q, g_kc, g_ks, g_kw, pe_k, pe_v, w_ck1, w_ck2, w_cv1, w_cv2, g_nsa_out,
         batch, seq):
    assert seq % (CMP_CHUNK * CMP_STRIDE) == 0 and seq >= WIN_KEYS
    qt_arr, ks, vst, kw, vwt, gates_t = _nsa_prep(proj, g_q, g_ks, g_kw, b_gate, batch, seq)
    kc, vct = _compress(proj, pe_k, pe_v, w_ck1, w_ck2, w_cv1, w_cv2, g_kc, batch, seq)
    negsel, flags, oc, near, slopes = _nsa_select(qt_arr, kc, vct, batch, seq)
    return _nsa_attend(flags, qt_arr, ks, vst, kw, vwt, negsel, near, oc, gates_t, slopes, g_nsa_out,
                       batch, seq)


def kernel(x, mem, g_mix, w_in, b_nsa_gate, g_q, g_kc, g_ks, g_kw, pe_k, pe_v, w_ck1, w_ck2,
           w_cv1, w_cv2, g_nsa_out, w_gk2, b_gk, g_gla_out, w_out, g_cross, g_mem, w_cq, w_ck,
           w_cv, g_cq, g_ck, w_co, g_ffn, w_gu, w_down):
    batch, seq, _ = x.shape
    x2d = x.reshape(batch * seq, D_MODEL)
    for l in range(w_in.shape[0]):
        proj = _in_proj(x2d, g_mix[l], *_pack_w_in(w_in[l]))
        o_gla = _gla(proj, w_gk2[l], b_gk[l], g_gla_out[l], batch, seq)
        o_nsa = _nsa(proj, b_nsa_gate[l], g_q[l], g_kc[l], g_ks[l], g_kw[l], pe_k[l], pe_v[l],
                     w_ck1[l], w_ck2[l], w_cv1[l], w_cv2[l], g_nsa_out[l], batch, seq)
        ck, cv = _mem_kv(mem, g_mem[l], w_ck[l], w_cv[l], g_ck[l])
        x2d = _out_cross(x2d, o_nsa, o_gla, w_out[l], g_cross[l], w_cq[l], g_cq[l], ck, cv,
                         w_co[l], seq)
        x2d = _ffn(x2d, g_ffn[l], w_gu[l], w_down[l])
    return x2d.reshape(batch, seq, D_MODEL)
```
